```python
import math
import jax
import jax.numpy as jnp
from jax import lax
import numpy as np

D_MODEL = 1024
BATCH = 4
SEQ = 4096
DEPTH = 2
DEC_BATCH = 32
DEC_SEQ = 1
PAST_LEN = 16384
PAGE_SIZE = 128

N_PAGES = PAST_LEN // PAGE_SIZE
N_POOL = DEC_BATCH * N_PAGES + max(1, (DEC_BATCH * N_PAGES) // 4)

N_EVEN = (DEPTH + 1) // 2
N_ODD = DEPTH // 2

H_A = 4
DK_A = D_MODEL // 16
DV_A = 2 * DK_A
Q_BLOCK = 128
H_B = 4
DK_B = D_MODEL // 16
DV_B = D_MODEL // 8
GLA_RANK = 16
GLA_NORMALIZER = 16.0
GLA_CHUNK = 16
H_C = 4
DH_C = D_MODEL // H_C
MLSTM_CHUNK = 64
D_FF = 256 * int(round(8 * D_MODEL / 3 / 256))
EPS = 1e-6

EVEN_SIZES = (H_A * 2 * DK_A, H_A * 2 * DK_A, H_A * DV_A,
              H_B * DK_B, H_B * DK_B, H_B * DV_B, GLA_RANK, H_B * DV_B)
EVEN_SPLITS = tuple(int(s) for s in np.cumsum(EVEN_SIZES)[:-1])
P_EVEN = sum(EVEN_SIZES)
M_EVEN = H_A * DV_A + H_B * DV_B
ODD_SIZES = (H_C * DH_C, H_C * DH_C, H_C * DH_C, H_C * DH_C, H_C, H_C)
ODD_SPLITS = tuple(int(s) for s in np.cumsum(ODD_SIZES)[:-1])
P_ODD = sum(ODD_SIZES)
M_ODD = H_C * DH_C

kernel_name = 'hybrid_diffattn_gla_mlstm_macaron_step'

F32 = jnp.float32


def rmsnorm(x, g):
    x32 = x.astype(F32)
    y = x32 * lax.rsqrt(jnp.mean(x32 * x32, axis=-1, keepdims=True) + EPS)
    return (y * g.astype(F32)).astype(x.dtype)


def swiglu(x, w_gate, w_up, w_down):
    return (jax.nn.silu(x @ w_gate) * (x @ w_up)) @ w_down


def macaron_half(x, g, w_gate, w_up, w_down):
    return x + (0.5 * swiglu(rmsnorm(x, g), w_gate, w_up, w_down)).astype(x.dtype)


def alibi_slopes(n):
    return jnp.asarray([2.0 ** (-8.0 * (h + 1) / n) for h in range(n)], dtype=F32)


def diff_lambda(lam_vecs, lam_init):
    lv = lam_vecs.astype(F32)
    return jnp.exp(jnp.sum(lv[0] * lv[1])) - jnp.exp(jnp.sum(lv[2] * lv[3])) + lam_init


def alibi_logits(q, k, q_pos, k_pos, slopes):
    s = jnp.einsum('bqhmd,bkhmd->bhmqk', q, k).astype(F32) * (DK_A ** -0.5)
    dist = (q_pos[:, None] - k_pos[None, :]).astype(F32)
    s = s - slopes[None, :, None, None, None] * dist
    return jnp.where(dist >= 0, s, -jnp.inf)


def diff_weights(logits, lam):
    p = jax.nn.softmax(logits, axis=-1)
    return p[:, :, 0] - lam * p[:, :, 1]


def diff_attn_prompt(q, k, v, lam, slopes):
    bsz, seq = q.shape[:2]
    n_blk = seq // Q_BLOCK
    q_blocks = jnp.swapaxes(q.reshape(bsz, n_blk, Q_BLOCK, H_A, 2, DK_A), 0, 1)
    k_pos = jnp.arange(seq)
    v32 = v.astype(F32)

    def block(args):
        q_blk, i = args
        q_pos = i * Q_BLOCK + jnp.arange(Q_BLOCK)
        w = diff_weights(alibi_logits(q_blk, k, q_pos, k_pos, slopes), lam)
        return jnp.einsum('bhqk,bkhv->bqhv', w, v32)

    out = lax.map(block, (q_blocks, jnp.arange(n_blk)))
    return jnp.swapaxes(out, 0, 1).reshape(bsz, seq, H_A, DV_A)


def diff_attn_sample(q, k_new, v_new, k_past, v_past, lam, slopes):
    past = k_past.shape[1]
    n_new = q.shape[1]
    q_pos = past + jnp.arange(n_new)
    logits = jnp.concatenate([
        alibi_logits(q, k_past, q_pos, jnp.arange(past), slopes),
        alibi_logits(q, k_new, q_pos, q_pos, slopes)], axis=-1)
    w = diff_weights(logits, lam)
    return (jnp.einsum('bhqk,bkhv->bqhv', w[..., :past], v_past.astype(F32))
            + jnp.einsum('bhqk,bkhv->bqhv', w[..., past:], v_new.astype(F32)))


def gla_chunked(q, k, v, g, s0):
    bsz, seq = q.shape[:2]
    c = GLA_CHUNK if seq % GLA_CHUNK == 0 else seq
    n = seq // c
    r = lambda t: t.reshape(bsz, n, c, *t.shape[2:]).astype(F32)
    q, k, v, g = r(q), r(k), r(v), r(g)
    b = jnp.cumsum(g, axis=2)
    b_last = b[:, :, -1]
    causal = jnp.tril(jnp.ones((c, c), dtype=bool))
    diff = b[:, :, :, None] - b[:, :, None, :]
    decay = jnp.exp(jnp.where(causal[None, None, :, :, None, None], diff, -jnp.inf))
    a_intra = jnp.einsum('bnthd,bntshd,bnshd->bnhts', q, decay, k)
    o_intra = jnp.einsum('bnhts,bnshv->bnthv', a_intra, v)
    u = jnp.einsum('bnshd,bnshv->bnhdv', k * jnp.exp(b_last[:, :, None] - b), v)

    def step(s, xs):
        u_n, a_n = xs
        return a_n[..., None] * s + u_n, s

    s_fin, s_prev = lax.scan(step, s0.astype(F32),
                             (jnp.swapaxes(u, 0, 1), jnp.swapaxes(jnp.exp(b_last), 0, 1)))
    s_prev = jnp.swapaxes(s_prev, 0, 1)
    o_inter = jnp.einsum('bnthd,bnhdv->bnthv', q * jnp.exp(b), s_prev)
    return (o_intra + o_inter).reshape(bsz, seq, H_B, DV_B), s_fin


def mlstm_chunked(q, k, v, i_pre, logf, c0, n0, m0):
    bsz, seq = q.shape[:2]
    c = MLSTM_CHUNK if seq % MLSTM_CHUNK == 0 else seq
    n = seq // c
    r = lambda t: jnp.moveaxis(t.reshape(bsz, n, c, *t.shape[2:]).astype(F32), 1, 0)
    causal = jnp.tril(jnp.ones((c, c), dtype=bool))

    def step(carry, xs):
        cs, ns, ms = carry
        qc, kc, vc, ic, fc = xs
        b = jnp.cumsum(fc, axis=1)
        dlog = b[:, :, None, :] - b[:, None, :, :] + ic[:, None, :, :]
        dlog = jnp.where(causal[None, :, :, None], dlog, -jnp.inf)
        prev_log = b + ms[:, None, :]
        m_t = jnp.maximum(prev_log, jnp.max(dlog, axis=2))
        dw = jnp.exp(dlog - m_t[:, :, None, :])
        pw = jnp.exp(prev_log - m_t)
        sw = jnp.einsum('bthd,bshd->btsh', qc, kc) * dw
        num = (jnp.einsum('btsh,bshv->bthv', sw, vc)
               + pw[..., None] * jnp.einsum('bhvd,bthd->bthv', cs, qc))
        den = jnp.sum(sw, axis=2) + pw * jnp.einsum('bhd,bthd->bth', ns, qc)
        h = num / jnp.maximum(jnp.abs(den), jnp.exp(-m_t))[..., None]
        m_end = m_t[:, -1]
        wk = jnp.exp(b[:, -1:, :] - b + ic - m_end[:, None, :])
        a = jnp.exp(b[:, -1] + ms - m_end)
        c_new = a[..., None, None] * cs + jnp.einsum('bsh,bshv,bshd->bhvd', wk, vc, kc)
        n_new = a[..., None] * ns + jnp.einsum('bsh,bshd->bhd', wk, kc)
        return (c_new, n_new, m_end), h

    (c_fin, n_fin, m_fin), h = lax.scan(
        step, (c0.astype(F32), n0.astype(F32), m0.astype(F32)),
        (r(q), r(k), r(v), r(i_pre), r(logf)))
    h = jnp.moveaxis(h, 0, 1).reshape(bsz, seq, H_C, DH_C)
    return h, c_fin, n_fin, m_fin


def even_inputs(hn, w_in, qk_norm_g, gate_w2, gate_b):
    bsz, seq = hn.shape[:2]
    aq, ak, av, bq, bk, bv, bg, br = jnp.split(hn @ w_in, EVEN_SPLITS, axis=-1)
    aq = rmsnorm(aq.reshape(bsz, seq, H_A, 2, DK_A), qk_norm_g[0])
    ak = rmsnorm(ak.reshape(bsz, seq, H_A, 2, DK_A), qk_norm_g[1])
    av = av.reshape(bsz, seq, H_A, DV_A)
    bq = bq.reshape(bsz, seq, H_B, DK_B) * (DK_B ** -0.5)
    bk = bk.reshape(bsz, seq, H_B, DK_B)
    bv = bv.reshape(bsz, seq, H_B, DV_B)
    glog = jax.nn.log_sigmoid((bg @ gate_w2 + gate_b).astype(F32)) / GLA_NORMALIZER
    glog = glog.reshape(bsz, seq, H_B, DK_B)
    return aq, ak, av, bq, bk, bv, glog, br


def even_output(a_out, b_out, br, lam_init, a_norm_g, b_norm_g, w_out):
    bsz, seq = a_out.shape[:2]
    a = rmsnorm(a_out, a_norm_g) * (1.0 - lam_init)
    b = rmsnorm(b_out, b_norm_g) * jax.nn.silu(br.astype(F32)).reshape(bsz, seq, H_B, DV_B)
    merged = jnp.concatenate([a.reshape(bsz, seq, H_A * DV_A), b.reshape(bsz, seq, H_B * DV_B)], axis=-1)
    return merged @ w_out


def odd_mixer(hn, w_in, gate_b, norm_g, w_out, c0, n0, m0):
    bsz, seq = hn.shape[:2]
    q, k, v, o, ig, fg = jnp.split(hn @ w_in, ODD_SPLITS, axis=-1)
    heads = lambda t: t.reshape(bsz, seq, H_C, DH_C)
    i_pre = ig.astype(F32) + gate_b[0].astype(F32)
    logf = jax.nn.log_sigmoid(fg.astype(F32) + gate_b[1].astype(F32))
    h, c_fin, n_fin, m_fin = mlstm_chunked(heads(q), heads(k) * (DH_C ** -0.5), heads(v),
                                           i_pre, logf, c0, n0, m0)
    y = jax.nn.sigmoid(heads(o).astype(F32)) * rmsnorm(h, norm_g)
    return y.reshape(bsz, seq, M_ODD) @ w_out, c_fin, n_fin, m_fin


def setup_inputs(seed: int = 0) -> dict:
    key = jax.random.key(seed)
    ks = iter(jax.random.split(key, 40))
    nrm = lambda shape, scale=1.0: scale * jax.random.normal(next(ks), shape, F32)
    page_table = jax.random.permutation(next(ks), N_POOL)[:DEC_BATCH * N_PAGES]
    page_table = page_table.reshape(DEC_BATCH, N_PAGES).astype(jnp.int32)
    return {
        'x_prompt': nrm((BATCH, SEQ, D_MODEL)),
        'x_sample': nrm((DEC_BATCH, DEC_SEQ, D_MODEL)),
        'cache_k': nrm((N_EVEN, N_POOL, PAGE_SIZE, H_A, 2, DK_A)),
        'cache_v': nrm((N_EVEN, N_POOL, PAGE_SIZE, H_A, DV_A)),
        'state_gla': nrm((N_EVEN, DEC_BATCH, H_B, DK_B, DV_B), 0.5),
        'state_mlstm_C': nrm((N_ODD, DEC_BATCH, H_C, DH_C, DH_C), 0.1),
        'state_mlstm_n': nrm((N_ODD, DEC_BATCH, H_C, DH_C), 0.1),
        'state_mlstm_m': nrm((N_ODD, DEC_BATCH, H_C), 1.0),
        'page_table': page_table,
        'norm_g': 1.0 + nrm((DEPTH, 3, D_MODEL), 0.02),
        'ffn_w_gate': nrm((DEPTH, 2, D_MODEL, D_FF), D_MODEL ** -0.5),
        'ffn_w_up': nrm((DEPTH, 2, D_MODEL, D_FF), D_MODEL ** -0.5),
        'ffn_w_down': nrm((DEPTH, 2, D_FF, D_MODEL), D_FF ** -0.5),
        'even_w_in': nrm((N_EVEN, D_MODEL, P_EVEN), D_MODEL ** -0.5),
        'even_w_out': nrm((N_EVEN, M_EVEN, D_MODEL), M_EVEN ** -0.5),
        'a_qk_norm': 1.0 + nrm((N_EVEN, 2, DK_A), 0.02),
        'a_lambda': nrm((N_EVEN, 4, DK_A), 0.1),
        'a_head_norm': 1.0 + nrm((N_EVEN, H_A, DV_A), 0.02),
        'b_gate_w2': nrm((N_EVEN, GLA_RANK, H_B * DK_B), GLA_RANK ** -0.5),
        'b_gate_bias': nrm((N_EVEN, H_B * DK_B), 0.1),
        'b_head_norm': 1.0 + nrm((N_EVEN, H_B, DV_B), 0.02),
        'odd_w_in': nrm((N_ODD, D_MODEL, P_ODD), D_MODEL ** -0.5),
        'odd_w_out': nrm((N_ODD, M_ODD, D_MODEL), M_ODD ** -0.5),
        'c_gate_bias': jnp.stack([nrm((N_ODD, H_C), 0.1),
                                  jnp.linspace(3.0, 6.0, H_C, dtype=F32)[None, :] + nrm((N_ODD, H_C), 0.1)], axis=1),
        'c_head_norm': 1.0 + nrm((N_ODD, H_C, DH_C), 0.02),
    }


def reference(x_prompt, x_sample, cache_k, cache_v, state_gla, state_mlstm_C, state_mlstm_n,
              state_mlstm_m, page_table, norm_g, ffn_w_gate, ffn_w_up, ffn_w_down,
              even_w_in, even_w_out, a_qk_norm, a_lambda, a_head_norm, b_gate_w2, b_gate_bias,
              b_head_norm, odd_w_in, odd_w_out, c_gate_bias, c_head_norm):
    yp, ys = x_prompt, x_sample
    slopes = alibi_slopes(H_A)
    n_pages = page_table.shape[1]
    past = n_pages * cache_k.shape[2]
    dec_b = x_sample.shape[0]
    k_p, v_p, k_s, v_s, gla_p, gla_s = [], [], [], [], [], []
    cm_p, nm_p, mm_p, cm_s, nm_s, mm_s = [], [], [], [], [], []
    for li in range(DEPTH):
        g = norm_g[li]
        yp = macaron_half(yp, g[0], ffn_w_gate[li, 0], ffn_w_up[li, 0], ffn_w_down[li, 0])
        ys = macaron_half(ys, g[0], ffn_w_gate[li, 0], ffn_w_up[li, 0], ffn_w_down[li, 0])
        hp = rmsnorm(yp, g[1])
        hs = rmsnorm(ys, g[1])
        if li % 2 == 0:
            e = li // 2
            lam_init = 0.8 - 0.6 * math.exp(-0.3 * li)
            lam = diff_lambda(a_lambda[e], lam_init)
            aq, ak, av, bq, bk, bv, bg, br = even_inputs(hp, even_w_in[e], a_qk_norm[e], b_gate_w2[e], b_gate_bias[e])
            a_out = diff_attn_prompt(aq, ak, av, lam, slopes)
            b_out, s_fin = gla_chunked(bq, bk, bv, bg, jnp.zeros((yp.shape[0], H_B, DK_B, DV_B), F32))
            yp = yp + even_output(a_out, b_out, br, lam_init, a_head_norm[e], b_head_norm[e], even_w_out[e]).astype(yp.dtype)
            k_p.append(ak)
            v_p.append(av)
            gla_p.append(s_fin)
            aq, ak, av, bq, bk, bv, bg, br = even_inputs(hs, even_w_in[e], a_qk_norm[e], b_gate_w2[e], b_gate_bias[e])
            k_past = cache_k[e, page_table].reshape(dec_b, past, H_A, 2, DK_A)
            v_past = cache_v[e, page_table].reshape(dec_b, past, H_A, DV_A)
            a_out = diff_attn_sample(aq, ak, av, k_past, v_past, lam, slopes)
            b_out, s_fin = gla_chunked(bq, bk, bv, bg, state_gla[e])
            ys = ys + even_output(a_out, b_out, br, lam_init, a_head_norm[e], b_head_norm[e], even_w_out[e]).astype(ys.dtype)
            k_s.append(ak)
            v_s.append(av)
            gla_s.append(s_fin)
        else:
            o = li // 2
            bp = yp.shape[0]
            out, c_fin, n_fin, m_fin = odd_mixer(
                hp, odd_w_in[o], c_gate_bias[o], c_head_norm[o], odd_w_out[o],
                jnp.zeros((bp, H_C, DH_C, DH_C), F32), jnp.zeros((bp, H_C, DH_C), F32), jnp.zeros((bp, H_C), F32))
            yp = yp + out.astype(yp.dtype)
            cm_p.append(c_fin)
            nm_p.append(n_fin)
            mm_p.append(m_fin)
            out, c_fin, n_fin, m_fin = odd_mixer(
                hs, odd_w_in[o], c_gate_bias[o], c_head_norm[o], odd_w_out[o],
                state_mlstm_C[o], state_mlstm_n[o], state_mlstm_m[o])
            ys = ys + out.astype(ys.dtype)
            cm_s.append(c_fin)
            nm_s.append(n_fin)
            mm_s.append(m_fin)
        yp = macaron_half(yp, g[2], ffn_w_gate[li, 1], ffn_w_up[li, 1], ffn_w_down[li, 1])
        ys = macaron_half(ys, g[2], ffn_w_gate[li, 1], ffn_w_up[li, 1], ffn_w_down[li, 1])
    return (yp, ys,
            jnp.stack(k_p), jnp.stack(v_p), jnp.stack(k_s), jnp.stack(v_s),
            jnp.stack(gla_p), jnp.stack(gla_s),
            jnp.stack(cm_p), jnp.stack(nm_p), jnp.stack(mm_p),
            jnp.stack(cm_s), jnp.stack(nm_s), jnp.stack(mm_s))
```

```python
import functools
import math

import jax
import jax.numpy as jnp
from jax import lax
from jax.experimental import pallas as pl
from jax.experimental.pallas import tpu as pltpu

F32 = jnp.float32
BF16 = jnp.bfloat16
EPS = 1e-6
HIGHEST = lax.Precision.HIGHEST

LANES = 128
VMEM_LIMIT_BYTES = 56 * 2**20

H_A, DK_A, DV_A = 4, 64, 128
H_B, DK_B, DV_B = 4, 64, 128
GLA_RANK = 16
GLA_NORMALIZER = 16.0
H_C, DH_C = 4, 256
W_A = H_A * 2 * DK_A
W_BK = H_B * DK_B
W_BV = H_B * DV_B
W_C = H_C * DH_C

ROW_TILE = 512
ATTN_TILE = 256
GLA_CHUNK = 128
MLSTM_CHUNK = 256
DECODE_PAGES = 8

NT_DIMS = (((1,), (1,)), ((), ()))


def _params(*semantics):
    return pltpu.CompilerParams(dimension_semantics=semantics, vmem_limit_bytes=VMEM_LIMIT_BYTES)


def _resident(shape):
    return pl.BlockSpec(shape, lambda *_: (0,) * len(shape), pipeline_mode=pl.Buffered(1))


def _rows(tm, width):
    return pl.BlockSpec((tm, width), lambda i: (i, 0))


def _row_tile(m):
    return ROW_TILE if m % ROW_TILE == 0 else m


def _rms(x, g):
    return x * lax.rsqrt(jnp.mean(x * x, axis=-1, keepdims=True) + EPS) * g


def _log_sigmoid(x):
    return jnp.minimum(x, 0.0) - jnp.log1p(jnp.exp(-jnp.abs(x)))


def _dot(a, b):
    return jnp.dot(a, b, preferred_element_type=F32)


def _dot_nt(a, b):
    return lax.dot_general(a, b, NT_DIMS, preferred_element_type=F32)


def _diff_lambda(lam_vecs, lam_init):
    p1 = jnp.sum(lam_vecs[0:1] * lam_vecs[1:2], axis=-1, keepdims=True)
    p2 = jnp.sum(lam_vecs[2:3] * lam_vecs[3:4], axis=-1, keepdims=True)
    return jnp.exp(p1) - jnp.exp(p2) + lam_init


def _ffn_body(x_ref, g_ref, wg_ref, wu_ref, wd_ref, o_ref, h_ref, *, ff_chunk):
    x = x_ref[...]
    xn = _rms(x, g_ref[...]).astype(BF16)
    for c in range(0, wg_ref.shape[1], ff_chunk):
        hg = _dot(xn, wg_ref[:, c:c + ff_chunk])
        hu = _dot(xn, wu_ref[:, c:c + ff_chunk])
        h_ref[:, c:c + ff_chunk] = (hg * jax.nn.sigmoid(hg) * hu).astype(BF16)
    o_ref[...] = x + 0.5 * _dot(h_ref[...], wd_ref[...])


def ffn_half(x, g, wg, wu, wd):
    m, d = x.shape
    dff = wg.shape[1]
    tm = _row_tile(m)
    return pl.pallas_call(
        functools.partial(_ffn_body, ff_chunk=256),
        grid=(m // tm,),
        in_specs=[_rows(tm, d), _resident((1, d)), _resident((d, dff)), _resident((d, dff)),
                  _resident((dff, d))],
        out_specs=_rows(tm, d),
        out_shape=jax.ShapeDtypeStruct((m, d), F32),
        scratch_shapes=[pltpu.VMEM((tm, dff), BF16)],
        compiler_params=_params("arbitrary"),
        name="ffn_half",
    )(x, g, wg, wu, wd)


E_AQ, E_AK, E_AV = 0, W_A, 2 * W_A
E_BQ = 3 * W_A
E_BK = E_BQ + W_BK
E_BV = E_BK + W_BK
E_BR = E_BV + W_BV
E_BG = E_BR + W_BV
E_COLS = E_BG + LANES


def _even_in_body(x_ref, g_ref, w_ref, seg_ref, gq_ref, gk_ref, w2_ref, gb_ref,
                  aq_ref, ak_ref, av_ref, akb_ref, avb_ref, bq_ref, bk_ref, bv_ref, gl_ref, br_ref):
    xn = _rms(x_ref[...], g_ref[...]).astype(BF16)
    proj = lambda lo, n: _dot(xn, w_ref[:, lo:lo + n])

    def group_rms(t, g):
        ms = _dot((t * t).astype(BF16), seg_ref[...]) * (1.0 / DK_A)
        return t * lax.rsqrt(ms + EPS) * g

    aq = group_rms(proj(E_AQ, W_A), gq_ref[...])
    aq_ref[...] = (aq * DK_A ** -0.5).astype(BF16)
    ak = group_rms(proj(E_AK, W_A), gk_ref[...])
    ak_ref[...] = ak
    akb_ref[...] = ak.astype(BF16)
    av = proj(E_AV, W_A)
    av_ref[...] = av
    avb_ref[...] = av.astype(BF16)
    bq_ref[...] = proj(E_BQ, W_BK) * DK_B ** -0.5
    bk_ref[...] = proj(E_BK, W_BK)
    bv_ref[...] = proj(E_BV, W_BV)
    br_ref[...] = proj(E_BR, W_BV)
    bg = proj(E_BG, LANES).astype(BF16)
    gl_ref[...] = _log_sigmoid(_dot(bg, w2_ref[...]) + gb_ref[...]) * (1.0 / GLA_NORMALIZER)


def even_inputs(x, g, w, seg, gq, gk, w2, gb):
    m, d = x.shape
    tm = _row_tile(m)
    widths = [(W_A, BF16), (W_A, F32), (W_A, F32), (W_A, BF16), (W_A, BF16),
              (W_BK, F32), (W_BK, F32), (W_BV, F32), (W_BK, F32), (W_BV, F32)]
    return pl.pallas_call(
        _even_in_body,
        grid=(m // tm,),
        in_specs=[_rows(tm, d), _resident((1, d)), _resident(w.shape), _resident(seg.shape),
                  _resident(gq.shape), _resident(gk.shape), _resident(w2.shape), _resident(gb.shape)],
        out_specs=[_rows(tm, n) for n, _ in widths],
        out_shape=[jax.ShapeDtypeStruct((m, n), dt) for n, dt in widths],
        compiler_params=_params("arbitrary"),
        name="even_inputs",
    )(x, g, w, seg, gq, gk, w2, gb)


def _attn_body(slope_ref, lam_ref, q_ref, k_ref, v_ref, o_ref, m_ref, l_ref, acc_ref, *, tile, lam_init):
    h = pl.program_id(1)
    qi = pl.program_id(2)
    slope = slope_ref[h]
    q = q_ref[...]
    lane = lax.broadcasted_iota(jnp.int32, q.shape, 1)
    zero = jnp.zeros_like(q)
    qq = jnp.concatenate([jnp.where(lane < DK_A, q, zero), jnp.where(lane >= DK_A, q, zero)], axis=0)
    m_ref[...] = jnp.full(m_ref.shape, -jnp.inf, F32)
    l_ref[...] = jnp.zeros(l_ref.shape, F32)
    acc_ref[...] = jnp.zeros(acc_ref.shape, F32)
    col = lax.broadcasted_iota(jnp.int32, (1, tile), 1)
    row = lax.broadcasted_iota(jnp.int32, (2 * tile, 1), 0)
    row = jnp.where(row >= tile, row - tile, row)

    def step(j, masked):
        k = k_ref[pl.ds(pl.multiple_of(j * tile, tile), tile), :]
        v = v_ref[pl.ds(pl.multiple_of(j * tile, tile), tile), :]
        s = _dot_nt(qq, k) + slope * (col + j * tile).astype(F32)
        if masked:
            s = jnp.where(col <= row, s, -jnp.inf)
        m_old = m_ref[...]
        m_new = jnp.maximum(m_old, jnp.max(s, axis=1, keepdims=True))
        p = jnp.exp(s - m_new)
        alpha = jnp.exp(m_old - m_new)
        l_ref[...] = alpha * l_ref[...] + jnp.sum(p, axis=1, keepdims=True)
        acc_ref[...] = alpha * acc_ref[...] + _dot(p.astype(BF16), v)
        m_ref[...] = m_new

    def body(j, carry):
        step(j, False)
        return carry

    lax.fori_loop(0, qi, body, 0)
    step(qi, True)
    out = acc_ref[...] / l_ref[...]
    lam = _diff_lambda(lam_ref[...], lam_init)
    o_ref[...] = out[:tile] - lam * out[tile:]


def diff_attn_prompt(aq, ak, av, slopes, lam_vecs, lam_init, batch):
    m = aq.shape[0]
    seq = m // batch
    tile = min(ATTN_TILE, seq)
    nq = seq // tile
    return pl.pallas_call(
        functools.partial(_attn_body, tile=tile, lam_init=lam_init),
        grid=(batch, H_A, nq),
        in_specs=[pl.BlockSpec(memory_space=pltpu.SMEM),
                  pl.BlockSpec(lam_vecs.shape, lambda b, h, i: (0, 0)),
                  pl.BlockSpec((tile, DV_A), lambda b, h, i: (b * nq + i, h)),
                  pl.BlockSpec((seq, DV_A), lambda b, h, i: (b, h)),
                  pl.BlockSpec((seq, DV_A), lambda b, h, i: (b, h))],
        out_specs=pl.BlockSpec((tile, DV_A), lambda b, h, i: (b * nq + i, h)),
        out_shape=jax.ShapeDtypeStruct((m, W_A), F32),
        scratch_shapes=[pltpu.VMEM((2 * tile, 1), F32), pltpu.VMEM((2 * tile, 1), F32),
                        pltpu.VMEM((2 * tile, DV_A), F32)],
        compiler_params=_params("arbitrary", "arbitrary", "arbitrary"),
        name="diff_attn_prompt",
    )(slopes, lam_vecs, aq, ak, av)


def _decode_body(pt_ref, q_ref, kn_ref, vn_ref, slope_ref, lam_ref, *rest, pages, past, lam_init):
    kt_refs = rest[:pages]
    v_refs = rest[pages:2 * pages]
    o_ref, m_ref, l_ref, acc_ref = rest[2 * pages:]
    j = pl.program_id(1)
    page = kt_refs[0].shape[3]
    groups = 2 * H_A

    @pl.when(j == 0)
    def _():
        m_ref[...] = jnp.full(m_ref.shape, -jnp.inf, F32)
        l_ref[...] = jnp.zeros(l_ref.shape, F32)
        acc_ref[...] = jnp.zeros(acc_ref.shape, F32)

    qrow = q_ref[0].astype(F32)
    lane = lax.broadcasted_iota(jnp.int32, (groups, W_A), 1)
    r8 = lax.broadcasted_iota(jnp.int32, (groups, W_A), 0)
    q8 = jnp.where(lane // DK_A == r8, jnp.broadcast_to(qrow, (groups, W_A)), 0.0)
    q8_bf = q8.astype(BF16)
    slope = slope_ref[...]
    col = lax.broadcasted_iota(jnp.int32, (1, page), 1)
    row_head = lax.broadcasted_iota(jnp.int32, (groups, 1), 0) // 2

    s_parts = []
    for i in range(pages):
        kpos = (col + (j * pages + i) * page).astype(F32)
        s_parts.append(_dot(q8_bf, kt_refs[i][0, 0].astype(BF16)) + slope * kpos)
    s = jnp.concatenate(s_parts, axis=1)
    m_old = m_ref[...]
    m_new = jnp.maximum(m_old, jnp.max(s, axis=1, keepdims=True))
    p = jnp.exp(s - m_new).astype(BF16)
    alpha = jnp.exp(m_old - m_new)
    pv = jnp.zeros(acc_ref.shape, F32)
    for i in range(pages):
        p_i = p[:, i * page:(i + 1) * page]
        for h in range(H_A):
            v_h = v_refs[i][0, 0, pl.ds(h, page, stride=H_A), :].astype(BF16)
            pv = pv + jnp.where(row_head == h, _dot(p_i, v_h), 0.0)
    l_ref[...] = alpha * l_ref[...] + jnp.sum(p.astype(F32), axis=1, keepdims=True)
    acc_ref[...] = alpha * acc_ref[...] + pv
    m_ref[...] = m_new

    @pl.when(j == pl.num_programs(1) - 1)
    def _():
        s_new = jnp.sum(q8 * kn_ref[0], axis=1, keepdims=True) + slope * float(past)
        v_new = jnp.zeros(acc_ref.shape, F32)
        for h in range(H_A):
            v_row = jnp.broadcast_to(vn_ref[0, :, h * DV_A:(h + 1) * DV_A], acc_ref.shape)
            v_new = jnp.where(row_head == h, v_row, v_new)
        m_old = m_ref[...]
        m_fin = jnp.maximum(m_old, s_new)
        p_new = jnp.exp(s_new - m_fin)
        alpha = jnp.exp(m_old - m_fin)
        l_fin = alpha * l_ref[...] + p_new
        o8 = (alpha * acc_ref[...] + p_new * v_new) / l_fin
        lam = _diff_lambda(lam_ref[...], lam_init)
        for h in range(H_A):
            o_ref[0, :, h * DV_A:(h + 1) * DV_A] = o8[2 * h:2 * h + 1] - lam * o8[2 * h + 1:2 * h + 2]


def diff_attn_decode(page_table, aq, ak, av, slope8, lam_vecs, lam_init, cache_kt, cache_v, layer):
    bd = aq.shape[0]
    n_pages = page_table.shape[1]
    page = cache_kt.shape[3]
    pages = math.gcd(DECODE_PAGES, n_pages)
    past = n_pages * page

    def page_spec(i, shape):
        return pl.BlockSpec((1, 1) + shape, lambda b, j, pt: (layer, pt[b * n_pages + j * pages + i], 0, 0))

    row3 = lambda: pl.BlockSpec((1, 1, W_A), lambda b, j, pt: (b, 0, 0))
    const = lambda a: pl.BlockSpec(a.shape, lambda b, j, pt: (0,) * a.ndim)
    grid_spec = pltpu.PrefetchScalarGridSpec(
        num_scalar_prefetch=1,
        grid=(bd, n_pages // pages),
        in_specs=[row3(), row3(), row3(), const(slope8), const(lam_vecs)]
        + [page_spec(i, (W_A, page)) for i in range(pages)]
        + [page_spec(i, (page * H_A, DV_A)) for i in range(pages)],
        out_specs=row3(),
        scratch_shapes=[pltpu.VMEM((2 * H_A, 1), F32), pltpu.VMEM((2 * H_A, 1), F32),
                        pltpu.VMEM((2 * H_A, DV_A), F32)],
    )
    out = pl.pallas_call(
        functools.partial(_decode_body, pages=pages, past=past, lam_init=lam_init),
        grid_spec=grid_spec,
        out_shape=jax.ShapeDtypeStruct((bd, 1, W_A), F32),
        compiler_params=_params("arbitrary", "arbitrary"),
        name="diff_attn_decode",
    )(page_table.reshape(-1), aq.reshape(bd, 1, W_A), ak.reshape(bd, 1, W_A), av.reshape(bd, 1, W_A),
      slope8, lam_vecs, *([cache_kt] * pages), *([cache_v] * pages))
    return out.reshape(bd, W_A)


def _gla_body(q_ref, k_ref, g_ref, v_ref, s0_ref, o_ref, sfin_ref, st_ref, *, chunk):
    n = pl.program_id(0)
    nb = q_ref.shape[0]
    pair = 2 * DK_B

    @pl.when(n == 0)
    def _():
        for b in range(nb):
            for p in range(H_B // 2):
                st_ref[b, p] = s0_ref[b, p * pair:(p + 1) * pair, :].T

    r = lax.broadcasted_iota(jnp.int32, (chunk, chunk), 0)
    c = lax.broadcasted_iota(jnp.int32, (chunk, chunk), 1)
    causal = r >= c
    tri = jnp.where(causal, 1.0, 0.0)
    lane_head = lax.broadcasted_iota(jnp.int32, (1, pair), 1) // DK_B
    mid = chunk // 2
    for b in range(nb):
        cum = jnp.dot(tri, g_ref[b], precision=HIGHEST, preferred_element_type=F32)
        c_mid = cum[mid - 1:mid, :]
        c_last = cum[chunk - 1:chunk, :]
        q = q_ref[b]
        k = k_ref[b]
        q_intra = (q * jnp.exp(cum - c_mid)).astype(BF16)
        k_intra = (k * jnp.exp(c_mid - cum)).astype(BF16)
        q_inter = (q * jnp.exp(cum)).astype(BF16)
        k_end = (k * jnp.exp(c_last - cum)).astype(BF16)
        decay = jnp.exp(c_last)
        zero = jnp.zeros((), BF16)
        for p in range(H_B // 2):
            sl = slice(p * pair, (p + 1) * pair)
            st = st_ref[b, p]
            st_bf = st.astype(BF16)
            st_new = st * decay[:, sl]
            for hh in range(2):
                h = 2 * p + hh
                own = lane_head == hh
                vsl = slice(h * DV_B, (h + 1) * DV_B)
                v = v_ref[b, :, vsl]
                a = _dot_nt(jnp.where(own, q_intra[:, sl], zero), k_intra[:, sl])
                a = jnp.where(causal, a, 0.0).astype(BF16)
                o = _dot(a, v.astype(BF16))
                o = o + _dot_nt(jnp.where(own, q_inter[:, sl], zero), st_bf)
                o_ref[b, :, vsl] = o
                st_new = st_new + _dot(v.T.astype(BF16), jnp.where(own, k_end[:, sl], zero))
            st_ref[b, p] = st_new

    @pl.when(n == pl.num_programs(0) - 1)
    def _():
        for b in range(nb):
            for p in range(H_B // 2):
                sfin_ref[b, p * pair:(p + 1) * pair, :] = st_ref[b, p].T


def gla_prompt(bq, bk, glog, bv, s0, batch):
    m = bq.shape[0]
    seq = m // batch
    chunk = min(GLA_CHUNK, seq)
    blk = lambda w: pl.BlockSpec((batch, chunk, w), lambda n: (0, n, 0))
    state = pl.BlockSpec((batch, W_BK, DV_B), lambda n: (0, 0, 0))
    o, s_fin = pl.pallas_call(
        functools.partial(_gla_body, chunk=chunk),
        grid=(seq // chunk,),
        in_specs=[blk(W_BK), blk(W_BK), blk(W_BK), blk(W_BV), state],
        out_specs=[blk(W_BV), state],
        out_shape=[jax.ShapeDtypeStruct((batch, seq, W_BV), F32),
                   jax.ShapeDtypeStruct((batch, W_BK, DV_B), F32)],
        scratch_shapes=[pltpu.VMEM((batch, H_B // 2, DV_B, 2 * DK_B), F32)],
        compiler_params=_params("arbitrary"),
        name="gla_prompt",
    )(bq.reshape(batch, seq, W_BK), bk.reshape(batch, seq, W_BK), glog.reshape(batch, seq, W_BK),
      bv.reshape(batch, seq, W_BV), s0)
    return o.reshape(m, W_BV), s_fin


def _column(row):
    n = row.shape[1]
    r = lax.broadcasted_iota(jnp.int32, (n, n), 0)
    c = lax.broadcasted_iota(jnp.int32, (n, n), 1)
    return jnp.sum(jnp.where(r == c, jnp.broadcast_to(row, (n, n)), 0.0), axis=1, keepdims=True)


def _gla_step_body(q_ref, k_ref, g_ref, v_ref, s0_ref, o_ref, s_ref):
    pair = 2 * DK_B
    for p in range(H_B // 2):
        sl = slice(p * pair, (p + 1) * pair)
        qc = _column(q_ref[0, :, sl])
        kc = _column(k_ref[0, :, sl])
        dc = jnp.exp(_column(g_ref[0, :, sl]))
        for hh in range(2):
            h = 2 * p + hh
            rs = slice(hh * DK_B, (hh + 1) * DK_B)
            v = v_ref[0, :, h * DV_B:(h + 1) * DV_B]
            s_new = dc[rs] * s0_ref[0, h] + kc[rs] * v
            s_ref[0, h] = s_new
            o_ref[0, :, h * DV_B:(h + 1) * DV_B] = jnp.sum(qc[rs] * s_new, axis=0, keepdims=True)


def gla_step(bq, bk, glog, bv, s0):
    bd = bq.shape[0]
    row = lambda w: pl.BlockSpec((1, 1, w), lambda b: (b, 0, 0))
    state = pl.BlockSpec((1, H_B, DK_B, DV_B), lambda b: (b, 0, 0, 0))
    o, s_fin = pl.pallas_call(
        _gla_step_body,
        grid=(bd,),
        in_specs=[row(W_BK), row(W_BK), row(W_BK), row(W_BV), state],
        out_specs=[row(W_BV), state],
        out_shape=[jax.ShapeDtypeStruct((bd, 1, W_BV), F32), jax.ShapeDtypeStruct(s0.shape, F32)],
        compiler_params=_params("arbitrary"),
        name="gla_step",
    )(bq.reshape(bd, 1, W_BK), bk.reshape(bd, 1, W_BK), glog.reshape(bd, 1, W_BK),
      bv.reshape(bd, 1, W_BV), s0)
    return o.reshape(bd, W_BV), s_fin


def _head_rms(t, g, width):
    parts = []
    for lo in range(0, t.shape[1], width):
        parts.append(_rms(t[:, lo:lo + width], g[:, lo:lo + width]))
    return jnp.concatenate(parts, axis=1)


def _even_out_body(x_ref, a_ref, b_ref, br_ref, ga_ref, gb_ref, w_ref, o_ref, *, lam_init):
    a = _head_rms(a_ref[...], ga_ref[...], DV_A) * (1.0 - lam_init)
    br = br_ref[...]
    b = _head_rms(b_ref[...], gb_ref[...], DV_B) * (br * jax.nn.sigmoid(br))
    merged = jnp.concatenate([a, b], axis=1).astype(BF16)
    o_ref[...] = x_ref[...] + _dot(merged, w_ref[...])


def even_output(x, a_out, b_out, br, ga, gb, w, lam_init):
    m, d = x.shape
    tm = _row_tile(m)
    return pl.pallas_call(
        functools.partial(_even_out_body, lam_init=lam_init),
        grid=(m // tm,),
        in_specs=[_rows(tm, d), _rows(tm, W_A), _rows(tm, W_BV), _rows(tm, W_BV),
                  _resident(ga.shape), _resident(gb.shape), _resident(w.shape)],
        out_specs=_rows(tm, d),
        out_shape=jax.ShapeDtypeStruct((m, d), F32),
        compiler_params=_params("arbitrary"),
        name="even_output",
    )(x, a_out, b_out, br, ga, gb, w)


O_Q, O_K, O_V, O_O, O_G = 0, W_C, 2 * W_C, 3 * W_C, 4 * W_C
O_COLS = O_G + LANES


def _odd_in_body(x_ref, g_ref, w_ref, bias_ref, q_ref, k_ref, v_ref, o_ref, gt_ref, *maybe_gtt_ref):
    xn = _rms(x_ref[...], g_ref[...]).astype(BF16)
    proj = lambda lo, n: _dot(xn, w_ref[:, lo:lo + n])
    q_ref[...] = proj(O_Q, W_C).astype(BF16)
    k_ref[...] = (proj(O_K, W_C) * DH_C ** -0.5).astype(BF16)
    v_ref[...] = proj(O_V, W_C)
    o_ref[...] = proj(O_O, W_C)
    z = proj(O_G, LANES) + bias_ref[...]
    lane = lax.broadcasted_iota(jnp.int32, z.shape, 1)
    gates = jnp.where(lane < H_C, z, _log_sigmoid(z))
    gt_ref[...] = gates
    if maybe_gtt_ref:
        maybe_gtt_ref[0][...] = gates.T[:2 * H_C, :]


def odd_inputs(x, g, w, bias):
    m, d = x.shape
    tm = _row_tile(m)
    with_transposed_gates = tm % LANES == 0
    widths = [(W_C, BF16), (W_C, BF16), (W_C, F32), (W_C, F32), (LANES, F32)]
    out_specs = [_rows(tm, n) for n, _ in widths]
    out_shape = [jax.ShapeDtypeStruct((m, n), dt) for n, dt in widths]
    if with_transposed_gates:
        out_specs.append(pl.BlockSpec((2 * H_C, tm), lambda i: (0, i)))
        out_shape.append(jax.ShapeDtypeStruct((2 * H_C, m), F32))
    return pl.pallas_call(
        _odd_in_body,
        grid=(m // tm,),
        in_specs=[_rows(tm, d), _resident((1, d)), _resident(w.shape), _resident(bias.shape)],
        out_specs=out_specs,
        out_shape=out_shape,
        compiler_params=_params("arbitrary"),
        name="odd_inputs",
    )(x, g, w, bias)


def _mlstm_body(q_ref, k_ref, v_ref, gt_ref, gtt_ref, c0_ref, n0_ref, m0_ref,
                h_ref, cfin_ref, nfin_ref, mfin_ref, c_sc, n_sc, m_sc, *, chunk):
    n = pl.program_id(1)

    @pl.when(n == 0)
    def _():
        c_sc[...] = c0_ref[0]
        n_sc[...] = n0_ref[0]
        m_sc[...] = m0_ref[0]

    r = lax.broadcasted_iota(jnp.int32, (chunk, chunk), 0)
    c = lax.broadcasted_iota(jnp.int32, (chunk, chunk), 1)
    causal = r >= c
    gates = gt_ref[...]
    gates_t = gtt_ref[...]
    cum_col = jnp.dot(jnp.where(causal, 1.0, 0.0), gates, precision=HIGHEST, preferred_element_type=F32)
    cum_row = jnp.dot(gates_t, jnp.where(r <= c, 1.0, 0.0), precision=HIGHEST, preferred_element_type=F32)
    for h in range(H_C):
        sl = slice(h * DH_C, (h + 1) * DH_C)
        i_col = gates[:, h:h + 1]
        i_row = gates_t[h:h + 1, :]
        b_col = cum_col[:, H_C + h:H_C + h + 1]
        b_row = cum_row[H_C + h:H_C + h + 1, :]
        b_last = b_col[chunk - 1:chunk, :]
        m_prev = m_sc[h:h + 1, 0:1]
        dlog = jnp.where(causal, b_col - b_row + i_row, -jnp.inf)
        prev_log = b_col + m_prev
        m_t = jnp.maximum(prev_log, jnp.max(dlog, axis=1, keepdims=True))
        dw = jnp.exp(dlog - m_t)
        pw = jnp.exp(prev_log - m_t)
        q = q_ref[:, sl]
        k = k_ref[:, sl]
        v = v_ref[:, sl]
        sw = _dot_nt(q, k) * dw
        c_old = c_sc[h]
        n_old = n_sc[h:h + 1, :]
        num = _dot(sw.astype(BF16), v.astype(BF16)) + pw * _dot_nt(q, c_old.astype(BF16))
        den = (jnp.sum(sw, axis=1, keepdims=True)
               + pw * jnp.sum(q.astype(F32) * n_old, axis=1, keepdims=True))
        h_ref[:, sl] = num / jnp.maximum(jnp.abs(den), jnp.exp(-m_t))
        m_end = m_t[chunk - 1:chunk, :]
        wk_col = jnp.exp(b_last - b_col + i_col - m_end)
        wk_row = jnp.exp(b_last - b_row + i_row - m_end)
        a = jnp.exp(b_last + m_prev - m_end)
        c_sc[h] = a * c_old + _dot((wk_col * v).T.astype(BF16), k)
        wk8 = jnp.broadcast_to(wk_row, (8, chunk)).astype(BF16)
        n_sc[h:h + 1, :] = a * n_old + _dot(wk8, k)[0:1]
        m_sc[h:h + 1, :] = jnp.broadcast_to(m_end, (1, LANES))

    @pl.when(n == pl.num_programs(1) - 1)
    def _():
        cfin_ref[0] = c_sc[...]
        nfin_ref[0] = n_sc[...]
        mfin_ref[0] = m_sc[...]


def mlstm_prompt(q, k, v, gates, gates_t, c0, n0, m0, batch):
    m = q.shape[0]
    seq = m // batch
    chunk = min(MLSTM_CHUNK, seq)
    nc = seq // chunk
    rows = lambda w: pl.BlockSpec((chunk, w), lambda b, n: (b * nc + n, 0))
    st_c = pl.BlockSpec((1, H_C, DH_C, DH_C), lambda b, n: (b, 0, 0, 0))
    st_n = pl.BlockSpec((1, H_C, DH_C), lambda b, n: (b, 0, 0))
    st_m = pl.BlockSpec((1, H_C, LANES), lambda b, n: (b, 0, 0))
    return pl.pallas_call(
        functools.partial(_mlstm_body, chunk=chunk),
        grid=(batch, nc),
        in_specs=[rows(W_C), rows(W_C), rows(W_C), rows(LANES),
                  pl.BlockSpec((2 * H_C, chunk), lambda b, n: (0, b * nc + n)), st_c, st_n, st_m],
        out_specs=[rows(W_C), st_c, st_n, st_m],
        out_shape=[jax.ShapeDtypeStruct((m, W_C), F32), jax.ShapeDtypeStruct(c0.shape, F32),
                   jax.ShapeDtypeStruct(n0.shape, F32), jax.ShapeDtypeStruct(m0.shape, F32)],
        scratch_shapes=[pltpu.VMEM((H_C, DH_C, DH_C), F32), pltpu.VMEM((H_C, DH_C), F32),
                        pltpu.VMEM((H_C, LANES), F32)],
        compiler_params=_params("arbitrary", "arbitrary"),
        name="mlstm_prompt",
    )(q, k, v, gates, gates_t, c0, n0, m0)


def _mlstm_step_body(q_ref, k_ref, v_ref, gt_ref, c0_ref, n0_ref, m0_ref, h_ref, c_ref, n_ref, m_ref):
    gates = gt_ref[0]
    for h in range(H_C):
        sl = slice(h * DH_C, (h + 1) * DH_C)
        i_pre = gates[:, h:h + 1]
        logf = gates[:, H_C + h:H_C + h + 1]
        m_prev = m0_ref[0, h:h + 1, 0:1]
        prev_log = logf + m_prev
        m_t = jnp.maximum(prev_log, i_pre)
        dw = jnp.exp(i_pre - m_t)
        pw = jnp.exp(prev_log - m_t)
        q = q_ref[0, :, sl]
        k = k_ref[0, :, sl]
        v = v_ref[0, :, sl]
        qf = q.astype(F32)
        kf = k.astype(F32)
        c_old = c0_ref[0, h]
        n_old = n0_ref[0, h:h + 1, :]
        sw = jnp.sum(qf * kf, axis=1, keepdims=True) * dw
        cq = _dot_nt(jnp.broadcast_to(q, (8, DH_C)), c_old.astype(BF16))[0:1]
        num = sw * v + pw * cq
        den = sw + pw * jnp.sum(qf * n_old, axis=1, keepdims=True)
        h_ref[0, :, sl] = num / jnp.maximum(jnp.abs(den), jnp.exp(-m_t))
        c_ref[0, h] = pw * c_old + _column(dw * v) * kf
        n_ref[0, h:h + 1, :] = pw * n_old + dw * kf
        m_ref[0, h:h + 1, :] = jnp.broadcast_to(m_t, (1, LANES))


def mlstm_step(q, k, v, gates, c0, n0, m0):
    bd = q.shape[0]
    row = lambda w: pl.BlockSpec((1, 1, w), lambda b: (b, 0, 0))
    st_c = pl.BlockSpec((1, H_C, DH_C, DH_C), lambda b: (b, 0, 0, 0))
    st_n = pl.BlockSpec((1, H_C, DH_C), lambda b: (b, 0, 0))
    st_m = pl.BlockSpec((1, H_C, LANES), lambda b: (b, 0, 0))
    h, c_fin, n_fin, m_fin = pl.pallas_call(
        _mlstm_step_body,
        grid=(bd,),
        in_specs=[row(W_C), row(W_C), row(W_C), row(LANES), st_c, st_n, st_m],
        out_specs=[row(W_C), st_c, st_n, st_m],
        out_shape=[jax.ShapeDtypeStruct((bd, 1, W_C), F32), jax.ShapeDtypeStruct(c0.shape, F32),
                   jax.ShapeDtypeStruct(n0.shape, F32), jax.ShapeDtypeStruct(m0.shape, F32)],
        compiler_params=_params("arbitrary"),
        name="mlstm_step",
    )(q.reshape(bd, 1, W_C), k.reshape(bd, 1, W_C), v.reshape(bd, 1, W_C), gates.reshape(bd, 1, LANES),
      c0, n0, m0)
    return h.reshape(bd, W_C), c_fin, n_fin, m_fin


def _odd_out_body(x_ref, h_ref, o_ref, g_ref, w_ref, y_ref):
    y = jax.nn.sigmoid(o_ref[...]) * _head_rms(h_ref[...], g_ref[...], DH_C)
    y_ref[...] = x_ref[...] + _dot(y.astype(BF16), w_ref[...])


def odd_output(x, h, o, g, w):
    m, d = x.shape
    tm = _row_tile(m)
    return pl.pallas_call(
        _odd_out_body,
        grid=(m // tm,),
        in_specs=[_rows(tm, d), _rows(tm, W_C), _rows(tm, W_C), _resident(g.shape), _resident(w.shape)],
        out_specs=_rows(tm, d),
        out_shape=jax.ShapeDtypeStruct((m, d), F32),
        compiler_params=_params("arbitrary"),
        name="odd_output",
    )(x, h, o, g, w)


def _even_weights(w_in, w_out, qk_norm, gate_w2, gate_b, a_norm, b_norm):
    main = 3 * W_A + 2 * W_BK + W_BV
    w = jnp.concatenate([w_in[:, :main], w_in[:, main + GLA_RANK:], w_in[:, main:main + GLA_RANK],
                         jnp.zeros((w_in.shape[0], LANES - GLA_RANK), w_in.dtype)], axis=1).astype(BF16)
    w2 = jnp.concatenate([gate_w2, jnp.zeros((LANES - GLA_RANK, W_BK), gate_w2.dtype)], axis=0).astype(BF16)
    idx = jnp.arange(W_A) // DK_A
    seg = (idx[:, None] == idx[None, :]).astype(BF16)
    gq = jnp.tile(qk_norm[0], W_A // DK_A).reshape(1, W_A)
    gk = jnp.tile(qk_norm[1], W_A // DK_A).reshape(1, W_A)
    return dict(w=w, w2=w2, seg=seg, gq=gq, gk=gk, gb=gate_b.reshape(1, W_BK), w_out=w_out.astype(BF16),
                ga=a_norm.reshape(1, W_A), gbn=b_norm.reshape(1, W_BV))


def _odd_weights(w_in, w_out, gate_b, norm_g):
    w = jnp.concatenate([w_in, jnp.zeros((w_in.shape[0], LANES - 2 * H_C), w_in.dtype)], axis=1).astype(BF16)
    bias = jnp.concatenate([gate_b[0], gate_b[1], jnp.zeros((LANES - 2 * H_C,), gate_b.dtype)]).reshape(1, LANES)
    return dict(w=w, bias=bias, w_out=w_out.astype(BF16), g=norm_g.reshape(1, W_C))


def kernel(x_prompt, x_sample, cache_k, cache_v, state_gla, state_mlstm_C, state_mlstm_n, state_mlstm_m,
           page_table, norm_g, ffn_w_gate, ffn_w_up, ffn_w_down, even_w_in, even_w_out, a_qk_norm, a_lambda,
           a_head_norm, b_gate_w2, b_gate_bias, b_head_norm, odd_w_in, odd_w_out, c_gate_bias, c_head_norm):
    bp, seq, d = x_prompt.shape
    bd = x_sample.shape[0]
    depth = norm_g.shape[0]
    assert x_sample.shape[1] == 1
    yp = x_prompt.reshape(bp * seq, d)
    ys = x_sample.reshape(bd, d)
    n_even, n_pool, page = cache_k.shape[:3]
    ck = jnp.transpose(cache_k, (0, 1, 3, 4, 5, 2)).reshape(n_even, n_pool, W_A, page)
    cv = cache_v.reshape(n_even, n_pool, page * H_A, DV_A)
    slopes = jnp.asarray([2.0 ** (-8.0 * (h + 1) / H_A) for h in range(H_A)], F32)
    slope8 = jnp.repeat(slopes, 2).reshape(2 * H_A, 1)
    wg, wu, wd = ffn_w_gate.astype(BF16), ffn_w_up.astype(BF16), ffn_w_down.astype(BF16)

    k_p, v_p, k_s, v_s, gla_p, gla_s = [], [], [], [], [], []
    cm_p, nm_p, mm_p, cm_s, nm_s, mm_s = [], [], [], [], [], []
    for li in range(depth):
        g = norm_g[li].reshape(3, 1, d)
        yp = ffn_half(yp, g[0], wg[li, 0], wu[li, 0], wd[li, 0])
        ys = ffn_half(ys, g[0], wg[li, 0], wu[li, 0], wd[li, 0])
        if li % 2 == 0:
            e = li // 2
            lam_init = 0.8 - 0.6 * math.exp(-0.3 * li)
            ew = _even_weights(even_w_in[e], even_w_out[e], a_qk_norm[e], b_gate_w2[e], b_gate_bias[e],
                               a_head_norm[e], b_head_norm[e])
            lam_vecs = a_lambda[e]
            aq, ak, av, akb, avb, bq, bk, bv, gl, br = even_inputs(
                yp, g[1], ew["w"], ew["seg"], ew["gq"], ew["gk"], ew["w2"], ew["gb"])
            a_out = diff_attn_prompt(aq, akb, avb, slopes, lam_vecs, lam_init, bp)
            b_out, s_fin = gla_prompt(bq, bk, gl, bv, jnp.zeros((bp, W_BK, DV_B), F32), bp)
            yp = even_output(yp, a_out, b_out, br, ew["ga"], ew["gbn"], ew["w_out"], lam_init)
            k_p.append(ak.reshape(bp, seq, H_A, 2, DK_A))
            v_p.append(av.reshape(bp, seq, H_A, DV_A))
            gla_p.append(s_fin.reshape(bp, H_B, DK_B, DV_B))
            aq, ak, av, akb, avb, bq, bk, bv, gl, br = even_inputs(
                ys, g[1], ew["w"], ew["seg"], ew["gq"], ew["gk"], ew["w2"], ew["gb"])
            a_out = diff_attn_decode(page_table, aq, ak, av, slope8, lam_vecs, lam_init, ck, cv, e)
            b_out, s_fin = gla_step(bq, bk, gl, bv, state_gla[e])
            ys = even_output(ys, a_out, b_out, br, ew["ga"], ew["gbn"], ew["w_out"], lam_init)
            k_s.append(ak.reshape(bd, 1, H_A, 2, DK_A))
            v_s.append(av.reshape(bd, 1, H_A, DV_A))
            gla_s.append(s_fin)
        else:
            o = li // 2
            ow = _odd_weights(odd_w_in[o], odd_w_out[o], c_gate_bias[o], c_head_norm[o])
            q, k, v, og, gates, gates_t = odd_inputs(yp, g[1], ow["w"], ow["bias"])
            h, c_fin, n_fin, m_fin = mlstm_prompt(
                q, k, v, gates, gates_t, jnp.zeros((bp, H_C, DH_C, DH_C), F32),
                jnp.zeros((bp, H_C, DH_C), F32), jnp.zeros((bp, H_C, LANES), F32), bp)
            yp = odd_output(yp, h, og, ow["g"], ow["w_out"])
            cm_p.append(c_fin)
            nm_p.append(n_fin)
            mm_p.append(m_fin[:, :, 0])
            q, k, v, og, gates = odd_inputs(ys, g[1], ow["w"], ow["bias"])
            m0 = jnp.broadcast_to(state_mlstm_m[o][:, :, None], (bd, H_C, LANES))
            h, c_fin, n_fin, m_fin = mlstm_step(q, k, v, gates, state_mlstm_C[o], state_mlstm_n[o], m0)
            ys = odd_output(ys, h, og, ow["g"], ow["w_out"])
            cm_s.append(c_fin)
            nm_s.append(n_fin)
            mm_s.append(m_fin[:, :, 0])
        yp = ffn_half(yp, g[2], wg[li, 1], wu[li, 1], wd[li, 1])
        ys = ffn_half(ys, g[2], wg[li, 1], wu[li, 1], wd[li, 1])
    return (yp.reshape(bp, seq, d), ys.reshape(bd, 1, d),
            jnp.stack(k_p), jnp.stack(v_p), jnp.stack(k_s), jnp.stack(v_s),
            jnp.stack(gla_p), jnp.stack(gla_s),
            jnp.stack(cm_p), jnp.stack(nm_p), jnp.stack(mm_p),
            jnp.stack(cm_s), jnp.stack(nm_s), jnp.stack(mm_s))
```

```python
import functools
import math

import jax
import jax.numpy as jnp
from jax import lax
from jax.experimental import pallas as pl
from jax.experimental.pallas import tpu as pltpu

F32 = jnp.float32
BF16 = jnp.bfloat16
EPS = 1e-6
HIGHEST = lax.Precision.HIGHEST

LANES = 128
VMEM_LIMIT_BYTES = 56 * 2**20

H_A, DK_A, DV_A = 4, 64, 128
H_B, DK_B, DV_B = 4, 64, 128
GLA_RANK = 16
GLA_NORMALIZER = 16.0
H_C, DH_C = 4, 256
W_A = H_A * 2 * DK_A
W_BK = H_B * DK_B
W_BV = H_B * DV_B
W_C = H_C * DH_C

ROW_TILE = 512
ATTN_TILE = 256
ATTN_KV_CHUNK = 1024
ATTN_HEADS_PER_STEP = 2
GLA_CHUNK = 128
MLSTM_CHUNK = 256
DECODE_PAGES = 16

NT_DIMS = (((1,), (1,)), ((), ()))


def _params(*semantics):
    return pltpu.CompilerParams(dimension_semantics=semantics, vmem_limit_bytes=VMEM_LIMIT_BYTES)


def _resident(shape):
    return pl.BlockSpec(shape, lambda *_: (0,) * len(shape), pipeline_mode=pl.Buffered(1))


def _rows(tm, width):
    return pl.BlockSpec((tm, width), lambda i: (i, 0))


def _row_tile(m):
    return ROW_TILE if m % ROW_TILE == 0 else m


def _rms(x, g):
    return x * lax.rsqrt(jnp.mean(x * x, axis=-1, keepdims=True) + EPS) * g


def _log_sigmoid(x):
    return jnp.minimum(x, 0.0) - jnp.log1p(jnp.exp(-jnp.abs(x)))


def _dot(a, b):
    return jnp.dot(a, b, preferred_element_type=F32)


def _dot_nt(a, b):
    return lax.dot_general(a, b, NT_DIMS, preferred_element_type=F32)


def _diff_lambda(lam_vecs, lam_init):
    p1 = jnp.sum(lam_vecs[0:1] * lam_vecs[1:2], axis=-1, keepdims=True)
    p2 = jnp.sum(lam_vecs[2:3] * lam_vecs[3:4], axis=-1, keepdims=True)
    return jnp.exp(p1) - jnp.exp(p2) + lam_init


def _ffn_body(x_ref, g_ref, wg_ref, wu_ref, wd_ref, o_ref, h_ref, *, ff_chunk):
    x = x_ref[...]
    xn = _rms(x, g_ref[...]).astype(BF16)
    for c in range(0, wg_ref.shape[1], ff_chunk):
        hg = _dot(xn, wg_ref[:, c:c + ff_chunk])
        hu = _dot(xn, wu_ref[:, c:c + ff_chunk])
        h_ref[:, c:c + ff_chunk] = (hg * jax.nn.sigmoid(hg) * hu).astype(BF16)
    o_ref[...] = x + 0.5 * _dot(h_ref[...], wd_ref[...])


def ffn_half(x, g, wg, wu, wd, layer, half):
    m, d = x.shape
    dff = wg.shape[-1]
    tm = _row_tile(m)
    weight = lambda r, c: pl.BlockSpec((None, None, r, c), lambda i: (layer, half, 0, 0),
                                       pipeline_mode=pl.Buffered(1))
    return pl.pallas_call(
        functools.partial(_ffn_body, ff_chunk=256),
        grid=(m // tm,),
        in_specs=[_rows(tm, d), _resident((1, d)), weight(d, dff), weight(d, dff), weight(dff, d)],
        out_specs=_rows(tm, d),
        out_shape=jax.ShapeDtypeStruct((m, d), F32),
        scratch_shapes=[pltpu.VMEM((tm, dff), BF16)],
        compiler_params=_params("arbitrary"),
        name="ffn_half",
    )(x, g, wg, wu, wd)


E_AQ, E_AK, E_AV = 0, W_A, 2 * W_A
E_BQ = 3 * W_A
E_BK = E_BQ + W_BK
E_BV = E_BK + W_BK
E_BR = E_BV + W_BV
E_BG = E_BR + W_BV
E_COLS = E_BG + LANES


def _even_in_body(x_ref, g_ref, w_ref, seg_ref, gq_ref, gk_ref, w2_ref, gb_ref, *out_refs, attn_tile):
    xn = _rms(x_ref[...], g_ref[...]).astype(BF16)
    proj = lambda lo, n: _dot(xn, w_ref[:, lo:lo + n])

    def group_rms(t, g):
        ms = _dot((t * t).astype(BF16), seg_ref[...]) * (1.0 / DK_A)
        return t * lax.rsqrt(ms + EPS) * g

    aq = group_rms(proj(E_AQ, W_A), gq_ref[...]) * DK_A ** -0.5
    ak = group_rms(proj(E_AK, W_A), gk_ref[...])
    av = proj(E_AV, W_A)
    if attn_tile is None:
        aq_ref, ak_ref, av_ref, bq_ref, bk_ref, bv_ref, gl_ref, br_ref = out_refs
        aq_ref[...] = aq.astype(BF16)
        ak_ref[...] = ak
    else:
        aqt_ref, akb_ref, akt_ref, av_ref, avt_ref, bq_ref, bk_ref, bv_ref, gl_ref, br_ref = out_refs
        aqt = aq.T.astype(BF16)
        avt = av.T.astype(BF16)
        for c in range(aqt_ref.shape[1]):
            aqt_ref[0, c] = aqt[:, c * attn_tile:(c + 1) * attn_tile]
            avt_ref[0, c] = avt[:, c * attn_tile:(c + 1) * attn_tile]
        akb_ref[...] = ak.astype(BF16)
        akt_ref[0] = ak.T
    av_ref[...] = av
    bq_ref[...] = proj(E_BQ, W_BK) * DK_B ** -0.5
    bk_ref[...] = proj(E_BK, W_BK)
    bv_ref[...] = proj(E_BV, W_BV)
    br_ref[...] = proj(E_BR, W_BV)
    bg = proj(E_BG, LANES).astype(BF16)
    gl_ref[...] = _log_sigmoid(_dot(bg, w2_ref[...]) + gb_ref[...]) * (1.0 / GLA_NORMALIZER)


def even_inputs(x, g, w, seg, gq, gk, w2, gb, prompt_batch=None):
    m, d = x.shape
    tm = _row_tile(m)
    common = [(W_BK, F32), (W_BK, F32), (W_BV, F32), (W_BK, F32), (W_BV, F32)]
    if prompt_batch is None:
        attn_tile = None
        widths = [(W_A, BF16), (W_A, F32), (W_A, F32)] + common
        out_specs = [_rows(tm, n) for n, _ in widths]
        out_shape = [jax.ShapeDtypeStruct((m, n), dt) for n, dt in widths]
    else:
        seq = m // prompt_batch
        attn_tile = min(ATTN_TILE, seq)
        per_seq = seq // tm
        per_tile = tm // attn_tile
        chunked = pl.BlockSpec((1, per_tile, W_A, attn_tile), lambda i: (i // per_seq, i % per_seq, 0, 0))
        chunked_shape = jax.ShapeDtypeStruct((prompt_batch, seq // attn_tile, W_A, attn_tile), BF16)
        out_specs = ([chunked, _rows(tm, W_A),
                      pl.BlockSpec((1, W_A, tm), lambda i: (i // per_seq, 0, i % per_seq)),
                      _rows(tm, W_A), chunked] + [_rows(tm, n) for n, _ in common])
        out_shape = ([chunked_shape, jax.ShapeDtypeStruct((m, W_A), BF16),
                      jax.ShapeDtypeStruct((prompt_batch, W_A, seq), F32),
                      jax.ShapeDtypeStruct((m, W_A), F32), chunked_shape]
                     + [jax.ShapeDtypeStruct((m, n), dt) for n, dt in common])
    return pl.pallas_call(
        functools.partial(_even_in_body, attn_tile=attn_tile),
        grid=(m // tm,),
        in_specs=[_rows(tm, d), _resident((1, d)), _resident(w.shape), _resident(seg.shape),
                  _resident(gq.shape), _resident(gk.shape), _resident(w2.shape), _resident(gb.shape)],
        out_specs=out_specs,
        out_shape=out_shape,
        compiler_params=_params("arbitrary"),
        name="even_inputs",
    )(x, g, w, seg, gq, gk, w2, gb)


def _attn_body(slope_ref, lam_ref, qt_ref, k_ref, posb_ref, vt_ref, o_ref, m_ref, acc_ref, *, tile, kv, lam_init):
    qi = pl.program_id(2)
    heads = m_ref.shape[0] // 2
    row = lax.broadcasted_iota(jnp.int32, (DV_A, tile), 0)
    erow = lax.broadcasted_iota(jnp.int32, (LANES, tile), 0)
    rhs = []
    for hh in range(heads):
        slope = slope_ref[pl.program_id(1) * heads + hh]
        qt = qt_ref[0, 0, hh * DV_A:(hh + 1) * DV_A, :]
        ext = jnp.where(erow < 2, slope, 0.0).astype(BF16)
        for mm in range(2):
            own = (row >= mm * DK_A) & (row < (mm + 1) * DK_A)
            rhs.append(jnp.concatenate([jnp.where(own, qt, jnp.zeros_like(qt)), ext], axis=0))
    ones = jnp.ones((16, kv), BF16)
    m_ref[...] = jnp.full(m_ref.shape, -jnp.inf, F32)
    acc_ref[...] = jnp.zeros(acc_ref.shape, F32)
    per_kv = kv // tile

    def step(j, masked):
        rows = pl.ds(pl.multiple_of(j * kv, kv), kv)
        posb = posb_ref[rows, :]
        scores = []
        for hh in range(heads):
            kx = jnp.concatenate([k_ref[rows, hh * DV_A:(hh + 1) * DV_A], posb], axis=1)
            scores += [_dot(kx, rhs[2 * hh]), _dot(kx, rhs[2 * hh + 1])]
        for st, s in enumerate(scores):
            hs = slice(st // 2 * DV_A, (st // 2 + 1) * DV_A)
            if masked:
                kk = lax.broadcasted_iota(jnp.int32, s.shape, 0) + (j * kv - qi * tile)
                qq = lax.broadcasted_iota(jnp.int32, s.shape, 1)
                s = jnp.where(kk <= qq, s, -jnp.inf)
            m_old = m_ref[st]
            m_new = jnp.maximum(m_old, jnp.max(s, axis=0, keepdims=True))
            p = jnp.exp(s - m_new).astype(BF16)
            vt = jnp.concatenate([vt_ref[0, j * per_kv + c, hs, :] for c in range(per_kv)], axis=1)
            vx = jnp.concatenate([vt, ones], axis=0)
            acc_ref[st] = jnp.exp(m_old - m_new) * acc_ref[st] + _dot(vx, p)
            m_ref[st] = m_new

    def body(j, carry):
        step(j, False)
        return carry

    n_full = (qi * tile) // kv
    lax.fori_loop(0, n_full, body, 0)
    step(n_full, True)
    lam = _diff_lambda(lam_ref[...], lam_init)
    for hh in range(heads):
        acc0 = acc_ref[2 * hh]
        acc1 = acc_ref[2 * hh + 1]
        out0 = acc0[:DV_A] * (1.0 / acc0[DV_A:DV_A + 1])
        out1 = acc1[:DV_A] * (1.0 / acc1[DV_A:DV_A + 1])
        o_ref[:, hh * DV_A:(hh + 1) * DV_A] = (out0 - lam * out1).T


def diff_attn_prompt(aqt, akb, avt, posb, slopes, lam_vecs, lam_init):
    batch, nq, _, tile = aqt.shape
    seq = nq * tile
    kv = min(ATTN_KV_CHUNK, seq)
    heads = ATTN_HEADS_PER_STEP
    wide = heads * DV_A
    return pl.pallas_call(
        functools.partial(_attn_body, tile=tile, kv=kv, lam_init=lam_init),
        grid=(batch, H_A // heads, nq),
        in_specs=[pl.BlockSpec(memory_space=pltpu.SMEM),
                  pl.BlockSpec(lam_vecs.shape, lambda b, h, i: (0, 0)),
                  pl.BlockSpec((1, 1, wide, tile), lambda b, h, i: (b, i, h, 0)),
                  pl.BlockSpec((seq, wide), lambda b, h, i: (b, h)),
                  pl.BlockSpec((seq, LANES), lambda b, h, i: (0, 0)),
                  pl.BlockSpec((1, nq, wide, tile), lambda b, h, i: (b, 0, h, 0))],
        out_specs=pl.BlockSpec((tile, wide), lambda b, h, i: (b * nq + i, h)),
        out_shape=jax.ShapeDtypeStruct((batch * seq, W_A), F32),
        scratch_shapes=[pltpu.VMEM((2 * heads, 1, tile), F32), pltpu.VMEM((2 * heads, DV_A + 16, tile), F32)],
        compiler_params=_params("arbitrary", "arbitrary", "arbitrary"),
        name="diff_attn_prompt",
    )(slopes, lam_vecs, aqt, akb, posb, avt)


def _decode_body(pt_ref, q_ref, kn_ref, vn_ref, slope_ref, lam_ref, *rest, pages, past, lam_init):
    kt_refs = rest[:pages]
    v_refs = rest[pages:2 * pages]
    o_ref, m_ref, l_ref, acc_ref = rest[2 * pages:]
    j = pl.program_id(1)
    page = kt_refs[0].shape[3]
    groups = 2 * H_A

    @pl.when(j == 0)
    def _():
        m_ref[...] = jnp.full(m_ref.shape, -jnp.inf, F32)
        l_ref[...] = jnp.zeros(l_ref.shape, F32)
        acc_ref[...] = jnp.zeros(acc_ref.shape, F32)

    qrow = q_ref[0].astype(F32)
    lane = lax.broadcasted_iota(jnp.int32, (groups, W_A), 1)
    r8 = lax.broadcasted_iota(jnp.int32, (groups, W_A), 0)
    q8 = jnp.where(lane // DK_A == r8, jnp.broadcast_to(qrow, (groups, W_A)), 0.0)
    q8_bf = q8.astype(BF16)
    slope = slope_ref[...]
    col = lax.broadcasted_iota(jnp.int32, (1, page), 1)
    row_head = lax.broadcasted_iota(jnp.int32, (groups, 1), 0) // 2

    s_parts = []
    for i in range(pages):
        kpos = (col + (j * pages + i) * page).astype(F32)
        s_parts.append(_dot(q8_bf, kt_refs[i][0, 0].astype(BF16)) + slope * kpos)
    s = jnp.concatenate(s_parts, axis=1)
    m_old = m_ref[...]
    m_new = jnp.maximum(m_old, jnp.max(s, axis=1, keepdims=True))
    p = jnp.exp(s - m_new).astype(BF16)
    alpha = jnp.exp(m_old - m_new)
    pv = jnp.zeros(acc_ref.shape, F32)
    for i in range(pages):
        p_i = p[:, i * page:(i + 1) * page]
        for h in range(H_A):
            v_h = v_refs[i][0, 0, pl.ds(h, page, stride=H_A), :].astype(BF16)
            pv = pv + jnp.where(row_head == h, _dot(p_i, v_h), 0.0)
    l_ref[...] = alpha * l_ref[...] + jnp.sum(p.astype(F32), axis=1, keepdims=True)
    acc_ref[...] = alpha * acc_ref[...] + pv
    m_ref[...] = m_new

    @pl.when(j == pl.num_programs(1) - 1)
    def _():
        s_new = jnp.sum(q8 * kn_ref[0], axis=1, keepdims=True) + slope * float(past)
        v_new = jnp.zeros(acc_ref.shape, F32)
        for h in range(H_A):
            v_row = jnp.broadcast_to(vn_ref[0, :, h * DV_A:(h + 1) * DV_A], acc_ref.shape)
            v_new = jnp.where(row_head == h, v_row, v_new)
        m_old = m_ref[...]
        m_fin = jnp.maximum(m_old, s_new)
        p_new = jnp.exp(s_new - m_fin)
        alpha = jnp.exp(m_old - m_fin)
        l_fin = alpha * l_ref[...] + p_new
        o8 = (alpha * acc_ref[...] + p_new * v_new) / l_fin
        lam = _diff_lambda(lam_ref[...], lam_init)
        for h in range(H_A):
            o_ref[0, :, h * DV_A:(h + 1) * DV_A] = o8[2 * h:2 * h + 1] - lam * o8[2 * h + 1:2 * h + 2]


def diff_attn_decode(page_table, aq, ak, av, slope8, lam_vecs, lam_init, cache_kt, cache_v, layer):
    bd = aq.shape[0]
    n_pages = page_table.shape[1]
    page = cache_kt.shape[3]
    pages = math.gcd(DECODE_PAGES, n_pages)
    past = n_pages * page

    def page_spec(i, shape):
        return pl.BlockSpec((1, 1) + shape, lambda b, j, pt: (layer, pt[b * n_pages + j * pages + i], 0, 0))

    row3 = lambda: pl.BlockSpec((1, 1, W_A), lambda b, j, pt: (b, 0, 0))
    const = lambda a: pl.BlockSpec(a.shape, lambda b, j, pt: (0,) * a.ndim)
    grid_spec = pltpu.PrefetchScalarGridSpec(
        num_scalar_prefetch=1,
        grid=(bd, n_pages // pages),
        in_specs=[row3(), row3(), row3(), const(slope8), const(lam_vecs)]
        + [page_spec(i, (W_A, page)) for i in range(pages)]
        + [page_spec(i, (page * H_A, DV_A)) for i in range(pages)],
        out_specs=row3(),
        scratch_shapes=[pltpu.VMEM((2 * H_A, 1), F32), pltpu.VMEM((2 * H_A, 1), F32),
                        pltpu.VMEM((2 * H_A, DV_A), F32)],
    )
    out = pl.pallas_call(
        functools.partial(_decode_body, pages=pages, past=past, lam_init=lam_init),
        grid_spec=grid_spec,
        out_shape=jax.ShapeDtypeStruct((bd, 1, W_A), F32),
        compiler_params=_params("arbitrary", "arbitrary"),
        name="diff_attn_decode",
    )(page_table.reshape(-1), aq.reshape(bd, 1, W_A), ak.reshape(bd, 1, W_A), av.reshape(bd, 1, W_A),
      slope8, lam_vecs, *([cache_kt] * pages), *([cache_v] * pages))
    return out.reshape(bd, W_A)


def _gla_body(q_ref, k_ref, g_ref, v_ref, s0_ref, o_ref, sfin_ref, st_ref, *, chunk):
    n = pl.program_id(0)
    nb = q_ref.shape[0]
    pair = 2 * DK_B

    @pl.when(n == 0)
    def _():
        for b in range(nb):
            for p in range(H_B // 2):
                st_ref[b, p] = s0_ref[b, p * pair:(p + 1) * pair, :].T

    r = lax.broadcasted_iota(jnp.int32, (chunk, chunk), 0)
    c = lax.broadcasted_iota(jnp.int32, (chunk, chunk), 1)
    causal = r >= c
    tri = jnp.where(causal, 1.0, 0.0)
    lane_head = lax.broadcasted_iota(jnp.int32, (1, pair), 1) // DK_B
    mid = chunk // 2
    for b in range(nb):
        cum = jnp.dot(tri, g_ref[b], precision=HIGHEST, preferred_element_type=F32)
        c_mid = cum[mid - 1:mid, :]
        c_last = cum[chunk - 1:chunk, :]
        q = q_ref[b]
        k = k_ref[b]
        q_intra = (q * jnp.exp(cum - c_mid)).astype(BF16)
        k_intra = (k * jnp.exp(c_mid - cum)).astype(BF16)
        q_inter = (q * jnp.exp(cum)).astype(BF16)
        k_end = (k * jnp.exp(c_last - cum)).astype(BF16)
        decay = jnp.exp(c_last)
        zero = jnp.zeros((), BF16)
        for p in range(H_B // 2):
            sl = slice(p * pair, (p + 1) * pair)
            st = st_ref[b, p]
            st_bf = st.astype(BF16)
            st_new = st * decay[:, sl]
            for hh in range(2):
                h = 2 * p + hh
                own = lane_head == hh
                vsl = slice(h * DV_B, (h + 1) * DV_B)
                v = v_ref[b, :, vsl]
                a = _dot_nt(jnp.where(own, q_intra[:, sl], zero), k_intra[:, sl])
                a = jnp.where(causal, a, 0.0).astype(BF16)
                o = _dot(a, v.astype(BF16))
                o = o + _dot_nt(jnp.where(own, q_inter[:, sl], zero), st_bf)
                o_ref[b, :, vsl] = o
                st_new = st_new + _dot(v.T.astype(BF16), jnp.where(own, k_end[:, sl], zero))
            st_ref[b, p] = st_new

    @pl.when(n == pl.num_programs(0) - 1)
    def _():
        for b in range(nb):
            for p in range(H_B // 2):
                sfin_ref[b, p * pair:(p + 1) * pair, :] = st_ref[b, p].T


def gla_prompt(bq, bk, glog, bv, s0, batch):
    m = bq.shape[0]
    seq = m // batch
    chunk = min(GLA_CHUNK, seq)
    blk = lambda w: pl.BlockSpec((batch, chunk, w), lambda n: (0, n, 0))
    state = pl.BlockSpec((batch, W_BK, DV_B), lambda n: (0, 0, 0))
    o, s_fin = pl.pallas_call(
        functools.partial(_gla_body, chunk=chunk),
        grid=(seq // chunk,),
        in_specs=[blk(W_BK), blk(W_BK), blk(W_BK), blk(W_BV), state],
        out_specs=[blk(W_BV), state],
        out_shape=[jax.ShapeDtypeStruct((batch, seq, W_BV), F32),
                   jax.ShapeDtypeStruct((batch, W_BK, DV_B), F32)],
        scratch_shapes=[pltpu.VMEM((batch, H_B // 2, DV_B, 2 * DK_B), F32)],
        compiler_params=_params("arbitrary"),
        name="gla_prompt",
    )(bq.reshape(batch, seq, W_BK), bk.reshape(batch, seq, W_BK), glog.reshape(batch, seq, W_BK),
      bv.reshape(batch, seq, W_BV), s0)
    return o.reshape(m, W_BV), s_fin


def _column(row):
    n = row.shape[1]
    r = lax.broadcasted_iota(jnp.int32, (n, n), 0)
    c = lax.broadcasted_iota(jnp.int32, (n, n), 1)
    return jnp.sum(jnp.where(r == c, jnp.broadcast_to(row, (n, n)), 0.0), axis=1, keepdims=True)


def _gla_step_body(q_ref, k_ref, g_ref, v_ref, s0_ref, o_ref, s_ref):
    pair = 2 * DK_B
    for p in range(H_B // 2):
        sl = slice(p * pair, (p + 1) * pair)
        qc = _column(q_ref[0, :, sl])
        kc = _column(k_ref[0, :, sl])
        dc = jnp.exp(_column(g_ref[0, :, sl]))
        for hh in range(2):
            h = 2 * p + hh
            rs = slice(hh * DK_B, (hh + 1) * DK_B)
            v = v_ref[0, :, h * DV_B:(h + 1) * DV_B]
            s_new = dc[rs] * s0_ref[0, h] + kc[rs] * v
            s_ref[0, h] = s_new
            o_ref[0, :, h * DV_B:(h + 1) * DV_B] = jnp.sum(qc[rs] * s_new, axis=0, keepdims=True)


def gla_step(bq, bk, glog, bv, s0):
    bd = bq.shape[0]
    row = lambda w: pl.BlockSpec((1, 1, w), lambda b: (b, 0, 0))
    state = pl.BlockSpec((1, H_B, DK_B, DV_B), lambda b: (b, 0, 0, 0))
    o, s_fin = pl.pallas_call(
        _gla_step_body,
        grid=(bd,),
        in_specs=[row(W_BK), row(W_BK), row(W_BK), row(W_BV), state],
        out_specs=[row(W_BV), state],
        out_shape=[jax.ShapeDtypeStruct((bd, 1, W_BV), F32), jax.ShapeDtypeStruct(s0.shape, F32)],
        compiler_params=_params("arbitrary"),
        name="gla_step",
    )(bq.reshape(bd, 1, W_BK), bk.reshape(bd, 1, W_BK), glog.reshape(bd, 1, W_BK),
      bv.reshape(bd, 1, W_BV), s0)
    return o.reshape(bd, W_BV), s_fin


def _head_rms(t, g, width):
    parts = []
    for lo in range(0, t.shape[1], width):
        parts.append(_rms(t[:, lo:lo + width], g[:, lo:lo + width]))
    return jnp.concatenate(parts, axis=1)


def _even_out_body(x_ref, a_ref, b_ref, br_ref, ga_ref, gb_ref, w_ref, o_ref, *, lam_init):
    a = _head_rms(a_ref[...], ga_ref[...], DV_A) * (1.0 - lam_init)
    br = br_ref[...]
    b = _head_rms(b_ref[...], gb_ref[...], DV_B) * (br * jax.nn.sigmoid(br))
    merged = jnp.concatenate([a, b], axis=1).astype(BF16)
    o_ref[...] = x_ref[...] + _dot(merged, w_ref[...])


def even_output(x, a_out, b_out, br, ga, gb, w, lam_init):
    m, d = x.shape
    tm = _row_tile(m)
    return pl.pallas_call(
        functools.partial(_even_out_body, lam_init=lam_init),
        grid=(m // tm,),
        in_specs=[_rows(tm, d), _rows(tm, W_A), _rows(tm, W_BV), _rows(tm, W_BV),
                  _resident(ga.shape), _resident(gb.shape), _resident(w.shape)],
        out_specs=_rows(tm, d),
        out_shape=jax.ShapeDtypeStruct((m, d), F32),
        compiler_params=_params("arbitrary"),
        name="even_output",
    )(x, a_out, b_out, br, ga, gb, w)


O_Q, O_K, O_V, O_O, O_G = 0, W_C, 2 * W_C, 3 * W_C, 4 * W_C
O_COLS = O_G + LANES


def _odd_in_body(x_ref, g_ref, w_ref, bias_ref, q_ref, k_ref, v_ref, o_ref, gt_ref, *maybe_gtt_ref):
    xn = _rms(x_ref[...], g_ref[...]).astype(BF16)
    proj = lambda lo, n: _dot(xn, w_ref[:, lo:lo + n])
    q_ref[...] = proj(O_Q, W_C).astype(BF16)
    k_ref[...] = (proj(O_K, W_C) * DH_C ** -0.5).astype(BF16)
    v_ref[...] = proj(O_V, W_C)
    o_ref[...] = proj(O_O, W_C)
    z = proj(O_G, LANES) + bias_ref[...]
    lane = lax.broadcasted_iota(jnp.int32, z.shape, 1)
    gates = jnp.where(lane < H_C, z, _log_sigmoid(z))
    gt_ref[...] = gates
    if maybe_gtt_ref:
        maybe_gtt_ref[0][...] = gates.T[:2 * H_C, :]


def odd_inputs(x, g, w, bias):
    m, d = x.shape
    tm = _row_tile(m)
    with_transposed_gates = tm % LANES == 0
    widths = [(W_C, BF16), (W_C, BF16), (W_C, F32), (W_C, F32), (LANES, F32)]
    out_specs = [_rows(tm, n) for n, _ in widths]
    out_shape = [jax.ShapeDtypeStruct((m, n), dt) for n, dt in widths]
    if with_transposed_gates:
        out_specs.append(pl.BlockSpec((2 * H_C, tm), lambda i: (0, i)))
        out_shape.append(jax.ShapeDtypeStruct((2 * H_C, m), F32))
    return pl.pallas_call(
        _odd_in_body,
        grid=(m // tm,),
        in_specs=[_rows(tm, d), _resident((1, d)), _resident(w.shape), _resident(bias.shape)],
        out_specs=out_specs,
        out_shape=out_shape,
        compiler_params=_params("arbitrary"),
        name="odd_inputs",
    )(x, g, w, bias)


def _mlstm_body(q_ref, k_ref, v_ref, gt_ref, gtt_ref, c0_ref, n0_ref, m0_ref,
                h_ref, cfin_ref, nfin_ref, mfin_ref, c_sc, n_sc, m_sc, *, chunk):
    n = pl.program_id(1)

    @pl.when(n == 0)
    def _():
        c_sc[...] = c0_ref[0]
        n_sc[...] = n0_ref[0]
        m_sc[...] = m0_ref[0]

    r = lax.broadcasted_iota(jnp.int32, (chunk, chunk), 0)
    c = lax.broadcasted_iota(jnp.int32, (chunk, chunk), 1)
    causal = r >= c
    gates = gt_ref[...]
    gates_t = gtt_ref[...]
    cum_col = jnp.dot(jnp.where(causal, 1.0, 0.0), gates, precision=HIGHEST, preferred_element_type=F32)
    cum_row = jnp.dot(gates_t, jnp.where(r <= c, 1.0, 0.0), precision=HIGHEST, preferred_element_type=F32)
    for h in range(H_C):
        sl = slice(h * DH_C, (h + 1) * DH_C)
        i_col = gates[:, h:h + 1]
        i_row = gates_t[h:h + 1, :]
        b_col = cum_col[:, H_C + h:H_C + h + 1]
        b_row = cum_row[H_C + h:H_C + h + 1, :]
        b_last = b_col[chunk - 1:chunk, :]
        m_prev = m_sc[h:h + 1, 0:1]
        dlog = jnp.where(causal, b_col - b_row + i_row, -jnp.inf)
        prev_log = b_col + m_prev
        m_t = jnp.maximum(prev_log, jnp.max(dlog, axis=1, keepdims=True))
        dw = jnp.exp(dlog - m_t)
        pw = jnp.exp(prev_log - m_t)
        q = q_ref[:, sl]
        k = k_ref[:, sl]
        v = v_ref[:, sl]
        sw = _dot_nt(q, k) * dw
        c_old = c_sc[h]
        n_old = n_sc[h:h + 1, :]
        num = _dot(sw.astype(BF16), v.astype(BF16)) + pw * _dot_nt(q, c_old.astype(BF16))
        den = (jnp.sum(sw, axis=1, keepdims=True)
               + pw * jnp.sum(q.astype(F32) * n_old, axis=1, keepdims=True))
        h_ref[:, sl] = num / jnp.maximum(jnp.abs(den), jnp.exp(-m_t))
        m_end = m_t[chunk - 1:chunk, :]
        wk_col = jnp.exp(b_last - b_col + i_col - m_end)
        wk_row = jnp.exp(b_last - b_row + i_row - m_end)
        a = jnp.exp(b_last + m_prev - m_end)
        c_sc[h] = a * c_old + _dot((wk_col * v).T.astype(BF16), k)
        wk8 = jnp.broadcast_to(wk_row, (8, chunk)).astype(BF16)
        n_sc[h:h + 1, :] = a * n_old + _dot(wk8, k)[0:1]
        m_sc[h:h + 1, :] = jnp.broadcast_to(m_end, (1, LANES))

    @pl.when(n == pl.num_programs(1) - 1)
    def _():
        cfin_ref[0] = c_sc[...]
        nfin_ref[0] = n_sc[...]
        mfin_ref[0] = m_sc[...]


def mlstm_prompt(q, k, v, gates, gates_t, c0, n0, m0, batch):
    m = q.shape[0]
    seq = m // batch
    chunk = min(MLSTM_CHUNK, seq)
    nc = seq // chunk
    rows = lambda w: pl.BlockSpec((chunk, w), lambda b, n: (b * nc + n, 0))
    st_c = pl.BlockSpec((1, H_C, DH_C, DH_C), lambda b, n: (b, 0, 0, 0))
    st_n = pl.BlockSpec((1, H_C, DH_C), lambda b, n: (b, 0, 0))
    st_m = pl.BlockSpec((1, H_C, LANES), lambda b, n: (b, 0, 0))
    return pl.pallas_call(
        functools.partial(_mlstm_body, chunk=chunk),
        grid=(batch, nc),
        in_specs=[rows(W_C), rows(W_C), rows(W_C), rows(LANES),
                  pl.BlockSpec((2 * H_C, chunk), lambda b, n: (0, b * nc + n)), st_c, st_n, st_m],
        out_specs=[rows(W_C), st_c, st_n, st_m],
        out_shape=[jax.ShapeDtypeStruct((m, W_C), F32), jax.ShapeDtypeStruct(c0.shape, F32),
                   jax.ShapeDtypeStruct(n0.shape, F32), jax.ShapeDtypeStruct(m0.shape, F32)],
        scratch_shapes=[pltpu.VMEM((H_C, DH_C, DH_C), F32), pltpu.VMEM((H_C, DH_C), F32),
                        pltpu.VMEM((H_C, LANES), F32)],
        compiler_params=_params("arbitrary", "arbitrary"),
        name="mlstm_prompt",
    )(q, k, v, gates, gates_t, c0, n0, m0)


def _mlstm_step_body(q_ref, k_ref, v_ref, gt_ref, c0_ref, n0_ref, m0_ref, h_ref, c_ref, n_ref, m_ref):
    gates = gt_ref[0]
    for h in range(H_C):
        sl = slice(h * DH_C, (h + 1) * DH_C)
        i_pre = gates[:, h:h + 1]
        logf = gates[:, H_C + h:H_C + h + 1]
        m_prev = m0_ref[0, h:h + 1, 0:1]
        prev_log = logf + m_prev
        m_t = jnp.maximum(prev_log, i_pre)
        dw = jnp.exp(i_pre - m_t)
        pw = jnp.exp(prev_log - m_t)
        q = q_ref[0, :, sl]
        k = k_ref[0, :, sl]
        v = v_ref[0, :, sl]
        qf = q.astype(F32)
        kf = k.astype(F32)
        c_old = c0_ref[0, h]
        n_old = n0_ref[0, h:h + 1, :]
        sw = jnp.sum(qf * kf, axis=1, keepdims=True) * dw
        cq = _dot_nt(jnp.broadcast_to(q, (8, DH_C)), c_old.astype(BF16))[0:1]
        num = sw * v + pw * cq
        den = sw + pw * jnp.sum(qf * n_old, axis=1, keepdims=True)
        h_ref[0, :, sl] = num / jnp.maximum(jnp.abs(den), jnp.exp(-m_t))
        c_ref[0, h] = pw * c_old + _column(dw * v) * kf
        n_ref[0, h:h + 1, :] = pw * n_old + dw * kf
        m_ref[0, h:h + 1, :] = jnp.broadcast_to(m_t, (1, LANES))


def mlstm_step(q, k, v, gates, c0, n0, m0):
    bd = q.shape[0]
    row = lambda w: pl.BlockSpec((1, 1, w), lambda b: (b, 0, 0))
    st_c = pl.BlockSpec((1, H_C, DH_C, DH_C), lambda b: (b, 0, 0, 0))
    st_n = pl.BlockSpec((1, H_C, DH_C), lambda b: (b, 0, 0))
    st_m = pl.BlockSpec((1, H_C, LANES), lambda b: (b, 0, 0))
    h, c_fin, n_fin, m_fin = pl.pallas_call(
        _mlstm_step_body,
        grid=(bd,),
        in_specs=[row(W_C), row(W_C), row(W_C), row(LANES), st_c, st_n, st_m],
        out_specs=[row(W_C), st_c, st_n, st_m],
        out_shape=[jax.ShapeDtypeStruct((bd, 1, W_C), F32), jax.ShapeDtypeStruct(c0.shape, F32),
                   jax.ShapeDtypeStruct(n0.shape, F32), jax.ShapeDtypeStruct(m0.shape, F32)],
        compiler_params=_params("arbitrary"),
        name="mlstm_step",
    )(q.reshape(bd, 1, W_C), k.reshape(bd, 1, W_C), v.reshape(bd, 1, W_C), gates.reshape(bd, 1, LANES),
      c0, n0, m0)
    return h.reshape(bd, W_C), c_fin, n_fin, m_fin


def _odd_out_body(x_ref, h_ref, o_ref, g_ref, w_ref, y_ref):
    y = jax.nn.sigmoid(o_ref[...]) * _head_rms(h_ref[...], g_ref[...], DH_C)
    y_ref[...] = x_ref[...] + _dot(y.astype(BF16), w_ref[...])


def odd_output(x, h, o, g, w):
    m, d = x.shape
    tm = _row_tile(m)
    return pl.pallas_call(
        _odd_out_body,
        grid=(m // tm,),
        in_specs=[_rows(tm, d), _rows(tm, W_C), _rows(tm, W_C), _resident(g.shape), _resident(w.shape)],
        out_specs=_rows(tm, d),
        out_shape=jax.ShapeDtypeStruct((m, d), F32),
        compiler_params=_params("arbitrary"),
        name="odd_output",
    )(x, h, o, g, w)


def _even_weights(w_in, w_out, qk_norm, gate_w2, gate_b, a_norm, b_norm):
    main = 3 * W_A + 2 * W_BK + W_BV
    w = jnp.concatenate([w_in[:, :main], w_in[:, main + GLA_RANK:], w_in[:, main:main + GLA_RANK],
                         jnp.zeros((w_in.shape[0], LANES - GLA_RANK), w_in.dtype)], axis=1).astype(BF16)
    w2 = jnp.concatenate([gate_w2, jnp.zeros((LANES - GLA_RANK, W_BK), gate_w2.dtype)], axis=0).astype(BF16)
    idx = jnp.arange(W_A) // DK_A
    seg = (idx[:, None] == idx[None, :]).astype(BF16)
    gq = jnp.tile(qk_norm[0], W_A // DK_A).reshape(1, W_A)
    gk = jnp.tile(qk_norm[1], W_A // DK_A).reshape(1, W_A)
    return dict(w=w, w2=w2, seg=seg, gq=gq, gk=gk, gb=gate_b.reshape(1, W_BK), w_out=w_out.astype(BF16),
                ga=a_norm.reshape(1, W_A), gbn=b_norm.reshape(1, W_BV))


def _odd_weights(w_in, w_out, gate_b, norm_g):
    w = jnp.concatenate([w_in, jnp.zeros((w_in.shape[0], LANES - 2 * H_C), w_in.dtype)], axis=1).astype(BF16)
    bias = jnp.concatenate([gate_b[0], gate_b[1], jnp.zeros((LANES - 2 * H_C,), gate_b.dtype)]).reshape(1, LANES)
    return dict(w=w, bias=bias, w_out=w_out.astype(BF16), g=norm_g.reshape(1, W_C))


def kernel(x_prompt, x_sample, cache_k, cache_v, state_gla, state_mlstm_C, state_mlstm_n, state_mlstm_m,
           page_table, norm_g, ffn_w_gate, ffn_w_up, ffn_w_down, even_w_in, even_w_out, a_qk_norm, a_lambda,
           a_head_norm, b_gate_w2, b_gate_bias, b_head_norm, odd_w_in, odd_w_out, c_gate_bias, c_head_norm):
    bp, seq, d = x_prompt.shape
    bd = x_sample.shape[0]
    depth = norm_g.shape[0]
    assert x_sample.shape[1] == 1
    yp = x_prompt.reshape(bp * seq, d)
    ys = x_sample.reshape(bd, d)
    n_even, n_pool, page = cache_k.shape[:3]
    ck = jnp.transpose(cache_k, (0, 1, 3, 4, 5, 2)).reshape(n_even, n_pool, W_A, page)
    cv = cache_v.reshape(n_even, n_pool, page * H_A, DV_A)
    slopes = jnp.asarray([2.0 ** (-8.0 * (h + 1) / H_A) for h in range(H_A)], F32)
    slope8 = jnp.repeat(slopes, 2).reshape(2 * H_A, 1)
    wg, wu, wd = ffn_w_gate.astype(BF16), ffn_w_up.astype(BF16), ffn_w_down.astype(BF16)
    pos = jnp.arange(seq)[:, None]
    lane = jnp.arange(LANES)[None, :]
    posb = jnp.where(lane == 0, pos // 64 * 64, jnp.where(lane == 1, pos % 64, 0)).astype(BF16)

    k_p, v_p, k_s, v_s, gla_p, gla_s = [], [], [], [], [], []
    cm_p, nm_p, mm_p, cm_s, nm_s, mm_s = [], [], [], [], [], []
    for li in range(depth):
        g = norm_g[li].reshape(3, 1, d)
        yp = ffn_half(yp, g[0], wg, wu, wd, li, 0)
        ys = ffn_half(ys, g[0], wg, wu, wd, li, 0)
        if li % 2 == 0:
            e = li // 2
            lam_init = 0.8 - 0.6 * math.exp(-0.3 * li)
            ew = _even_weights(even_w_in[e], even_w_out[e], a_qk_norm[e], b_gate_w2[e], b_gate_bias[e],
                               a_head_norm[e], b_head_norm[e])
            lam_vecs = a_lambda[e]
            aqt, akb, akt, av, avt, bq, bk, bv, gl, br = even_inputs(
                yp, g[1], ew["w"], ew["seg"], ew["gq"], ew["gk"], ew["w2"], ew["gb"], prompt_batch=bp)
            a_out = diff_attn_prompt(aqt, akb, avt, posb, slopes, lam_vecs, lam_init)
            b_out, s_fin = gla_prompt(bq, bk, gl, bv, jnp.zeros((bp, W_BK, DV_B), F32), bp)
            yp = even_output(yp, a_out, b_out, br, ew["ga"], ew["gbn"], ew["w_out"], lam_init)
            k_p.append(jnp.transpose(akt.reshape(bp, H_A, 2, DK_A, seq), (0, 4, 1, 2, 3)))
            v_p.append(av.reshape(bp, seq, H_A, DV_A))
            gla_p.append(s_fin.reshape(bp, H_B, DK_B, DV_B))
            aq, ak, av, bq, bk, bv, gl, br = even_inputs(
                ys, g[1], ew["w"], ew["seg"], ew["gq"], ew["gk"], ew["w2"], ew["gb"])
            a_out = diff_attn_decode(page_table, aq, ak, av, slope8, lam_vecs, lam_init, ck, cv, e)
            b_out, s_fin = gla_step(bq, bk, gl, bv, state_gla[e])
            ys = even_output(ys, a_out, b_out, br, ew["ga"], ew["gbn"], ew["w_out"], lam_init)
            k_s.append(ak.reshape(bd, 1, H_A, 2, DK_A))
            v_s.append(av.reshape(bd, 1, H_A, DV_A))
            gla_s.append(s_fin)
        else:
            o = li // 2
            ow = _odd_weights(odd_w_in[o], odd_w_out[o], c_gate_bias[o], c_head_norm[o])
            q, k, v, og, gates, gates_t = odd_inputs(yp, g[1], ow["w"], ow["bias"])
            h, c_fin, n_fin, m_fin = mlstm_prompt(
                q, k, v, gates, gates_t, jnp.zeros((bp, H_C, DH_C, DH_C), F32),
                jnp.zeros((bp, H_C, DH_C), F32), jnp.zeros((bp, H_C, LANES), F32), bp)
            yp = odd_output(yp, h, og, ow["g"], ow["w_out"])
            cm_p.append(c_fin)
            nm_p.append(n_fin)
            mm_p.append(m_fin[:, :, 0])
            q, k, v, og, gates = odd_inputs(ys, g[1], ow["w"], ow["bias"])
            m0 = jnp.broadcast_to(state_mlstm_m[o][:, :, None], (bd, H_C, LANES))
            h, c_fin, n_fin, m_fin = mlstm_step(q, k, v, gates, state_mlstm_C[o], state_mlstm_n[o], m0)
            ys = odd_output(ys, h, og, ow["g"], ow["w_out"])
            cm_s.append(c_fin)
            nm_s.append(n_fin)
            mm_s.append(m_fin[:, :, 0])
        yp = ffn_half(yp, g[2], wg, wu, wd, li, 1)
        ys = ffn_half(ys, g[2], wg, wu, wd, li, 1)
    return (yp.reshape(bp, seq, d), ys.reshape(bd, 1, d),
            jnp.stack(k_p), jnp.stack(v_p), jnp.stack(k_s), jnp.stack(v_s),
            jnp.stack(gla_p), jnp.stack(gla_s),
            jnp.stack(cm_p), jnp.stack(nm_p), jnp.stack(mm_p),
            jnp.stack(cm_s), jnp.stack(nm_s), jnp.stack(mm_s))
```

```python
import functools
import math

import jax
import jax.numpy as jnp
from jax import lax
from jax.experimental import pallas as pl
from jax.experimental.pallas import tpu as pltpu

F32 = jnp.float32
BF16 = jnp.bfloat16
EPS = 1e-6
HIGHEST = lax.Precision.HIGHEST
LOG2E = math.log2(math.e)

LANES = 128
VMEM_LIMIT_BYTES = 56 * 2**20

H_A, DK_A, DV_A = 4, 64, 128
H_B, DK_B, DV_B = 4, 64, 128
GLA_RANK = 16
GLA_NORMALIZER = 16.0
H_C, DH_C = 4, 256
W_A = H_A * 2 * DK_A
W_BK = H_B * DK_B
W_BV = H_B * DV_B
W_C = H_C * DH_C

ROW_TILE = 512
ATTN_TILE = 256
ATTN_KV_CHUNK = 1024
ATTN_HEADS_PER_STEP = 4
GLA_CHUNK = 128
MLSTM_CHUNK = 256
DECODE_PAGES = 16

NT_DIMS = (((1,), (1,)), ((), ()))


def _params(*semantics):
    return pltpu.CompilerParams(dimension_semantics=semantics, vmem_limit_bytes=VMEM_LIMIT_BYTES)


def _resident(shape):
    return pl.BlockSpec(shape, lambda *_: (0,) * len(shape), pipeline_mode=pl.Buffered(1))


def _rows(tm, width):
    return pl.BlockSpec((tm, width), lambda i: (i, 0))


def _row_tile(m):
    return ROW_TILE if m % ROW_TILE == 0 else m


def _rms(x, g):
    return x * lax.rsqrt(jnp.mean(x * x, axis=-1, keepdims=True) + EPS) * g


def _log_sigmoid(x):
    return jnp.minimum(x, 0.0) - jnp.log1p(jnp.exp(-jnp.abs(x)))


def _dot(a, b):
    return jnp.dot(a, b, preferred_element_type=F32)


def _dot_nt(a, b):
    return lax.dot_general(a, b, NT_DIMS, preferred_element_type=F32)


def _diff_lambda(lam_vecs, lam_init):
    p1 = jnp.sum(lam_vecs[0:1] * lam_vecs[1:2], axis=-1, keepdims=True)
    p2 = jnp.sum(lam_vecs[2:3] * lam_vecs[3:4], axis=-1, keepdims=True)
    return jnp.exp(p1) - jnp.exp(p2) + lam_init


def _ffn_body(x_ref, g_ref, wg_ref, wu_ref, wd_ref, o_ref, h_ref, *, ff_chunk):
    x = x_ref[...]
    xn = _rms(x, g_ref[...]).astype(BF16)
    for c in range(0, wg_ref.shape[1], ff_chunk):
        hg = _dot(xn, wg_ref[:, c:c + ff_chunk])
        hu = _dot(xn, wu_ref[:, c:c + ff_chunk])
        h_ref[:, c:c + ff_chunk] = (hg * jax.nn.sigmoid(hg) * hu).astype(BF16)
    o_ref[...] = x + 0.5 * _dot(h_ref[...], wd_ref[...])


def ffn_half(x, g, wg, wu, wd, layer, half):
    m, d = x.shape
    dff = wg.shape[-1]
    tm = _row_tile(m)
    weight = lambda r, c: pl.BlockSpec((None, None, r, c), lambda i: (layer, half, 0, 0),
                                       pipeline_mode=pl.Buffered(1))
    return pl.pallas_call(
        functools.partial(_ffn_body, ff_chunk=256),
        grid=(m // tm,),
        in_specs=[_rows(tm, d), _resident((1, d)), weight(d, dff), weight(d, dff), weight(dff, d)],
        out_specs=_rows(tm, d),
        out_shape=jax.ShapeDtypeStruct((m, d), F32),
        scratch_shapes=[pltpu.VMEM((tm, dff), BF16)],
        compiler_params=_params("arbitrary"),
        name="ffn_half",
    )(x, g, wg, wu, wd)


E_AQ, E_AK, E_AV = 0, W_A, 2 * W_A
E_BQ = 3 * W_A
E_BK = E_BQ + W_BK
E_BV = E_BK + W_BK
E_BR = E_BV + W_BV
E_BG = E_BR + W_BV
E_COLS = E_BG + LANES


def _even_in_body(x_ref, g_ref, w_ref, seg_ref, gq_ref, gk_ref, w2_ref, gb_ref, *out_refs, attn_tile):
    xn = _rms(x_ref[...], g_ref[...]).astype(BF16)
    proj = lambda lo, n: _dot(xn, w_ref[:, lo:lo + n])

    def group_rms(t, g):
        ms = _dot((t * t).astype(BF16), seg_ref[...]) * (1.0 / DK_A)
        return t * lax.rsqrt(ms + EPS) * g

    aq = group_rms(proj(E_AQ, W_A), gq_ref[...]) * DK_A ** -0.5
    ak = group_rms(proj(E_AK, W_A), gk_ref[...])
    av = proj(E_AV, W_A)
    if attn_tile is None:
        aq_ref, ak_ref, av_ref, bq_ref, bk_ref, bv_ref, gl_ref, br_ref = out_refs
        aq_ref[...] = aq.astype(BF16)
        ak_ref[...] = ak
        av_ref[...] = av
    else:
        aqt_ref, akb_ref, akt_ref, av_ref, avt_ref, bq_ref, bk_ref, bv_ref, gl_ref, br_ref = out_refs
        aqt = (aq * LOG2E).T.astype(BF16)
        avt = av.T.astype(BF16)
        for c in range(aqt_ref.shape[1]):
            aqt_ref[0, c] = aqt[:, c * attn_tile:(c + 1) * attn_tile]
            avt_ref[0, c] = avt[:, c * attn_tile:(c + 1) * attn_tile]
        akb_ref[...] = ak.astype(BF16)
        akt_ref[0] = ak.T
        for h in range(H_A):
            av_ref[pl.ds(h, av.shape[0], stride=H_A), :] = av[:, h * DV_A:(h + 1) * DV_A]
    bq_ref[...] = proj(E_BQ, W_BK) * DK_B ** -0.5
    bk_ref[...] = proj(E_BK, W_BK)
    bv_ref[...] = proj(E_BV, W_BV)
    br_ref[...] = proj(E_BR, W_BV)
    bg = proj(E_BG, LANES).astype(BF16)
    gl_ref[...] = _log_sigmoid(_dot(bg, w2_ref[...]) + gb_ref[...]) * (1.0 / GLA_NORMALIZER)


def even_inputs(x, g, w, seg, gq, gk, w2, gb, prompt_batch=None):
    m, d = x.shape
    tm = _row_tile(m)
    common = [(W_BK, F32), (W_BK, F32), (W_BV, F32), (W_BK, F32), (W_BV, F32)]
    if prompt_batch is None:
        attn_tile = None
        widths = [(W_A, BF16), (W_A, F32), (W_A, F32)] + common
        out_specs = [_rows(tm, n) for n, _ in widths]
        out_shape = [jax.ShapeDtypeStruct((m, n), dt) for n, dt in widths]
    else:
        seq = m // prompt_batch
        attn_tile = min(ATTN_TILE, seq)
        per_seq = seq // tm
        per_tile = tm // attn_tile
        chunked = pl.BlockSpec((1, per_tile, W_A, attn_tile), lambda i: (i // per_seq, i % per_seq, 0, 0))
        chunked_shape = jax.ShapeDtypeStruct((prompt_batch, seq // attn_tile, W_A, attn_tile), BF16)
        out_specs = ([chunked, _rows(tm, W_A),
                      pl.BlockSpec((1, W_A, tm), lambda i: (i // per_seq, 0, i % per_seq)),
                      _rows(tm * H_A, DV_A), chunked] + [_rows(tm, n) for n, _ in common])
        out_shape = ([chunked_shape, jax.ShapeDtypeStruct((m, W_A), BF16),
                      jax.ShapeDtypeStruct((prompt_batch, W_A, seq), F32),
                      jax.ShapeDtypeStruct((m * H_A, DV_A), F32), chunked_shape]
                     + [jax.ShapeDtypeStruct((m, n), dt) for n, dt in common])
    return pl.pallas_call(
        functools.partial(_even_in_body, attn_tile=attn_tile),
        grid=(m // tm,),
        in_specs=[_rows(tm, d), _resident((1, d)), _resident(w.shape), _resident(seg.shape),
                  _resident(gq.shape), _resident(gk.shape), _resident(w2.shape), _resident(gb.shape)],
        out_specs=out_specs,
        out_shape=out_shape,
        compiler_params=_params("arbitrary"),
        name="even_inputs",
    )(x, g, w, seg, gq, gk, w2, gb)


def _attn_body(slope_ref, lam_ref, qt_ref, k_ref, posb_ref, vt_ref, o_ref, m_ref, acc_ref, *, tile, kv, lam_init):
    qi = pl.program_id(2)
    heads = m_ref.shape[0] // 2
    row = lax.broadcasted_iota(jnp.int32, (DV_A, tile), 0)
    erow = lax.broadcasted_iota(jnp.int32, (LANES, tile), 0)
    rhs = []
    for hh in range(heads):
        slope = slope_ref[pl.program_id(1) * heads + hh]
        qt = qt_ref[0, 0, hh * DV_A:(hh + 1) * DV_A, :]
        c_all = jnp.full((LANES, tile), slope * LOG2E, F32)
        c_1 = c_all.astype(BF16).astype(F32)
        c_2 = (c_all - c_1).astype(BF16).astype(F32)
        c_3 = (c_all - c_1 - c_2).astype(BF16).astype(F32)
        ext = jnp.where(erow < 2, c_1, jnp.where(erow < 4, c_2, jnp.where(erow < 6, c_3, 0.0))).astype(BF16)
        for mm in range(2):
            own = (row >= mm * DK_A) & (row < (mm + 1) * DK_A)
            rhs.append(jnp.concatenate([jnp.where(own, qt, jnp.zeros_like(qt)), ext], axis=0))
    ones = jnp.ones((16, kv), BF16)
    m_ref[...] = jnp.full(m_ref.shape, -jnp.inf, F32)
    acc_ref[...] = jnp.zeros(acc_ref.shape, F32)
    per_kv = kv // tile

    def step(j, masked):
        rows = pl.ds(pl.multiple_of(j * kv, kv), kv)
        posb = posb_ref[rows, :]
        scores = []
        for hh in range(heads):
            kx = jnp.concatenate([k_ref[rows, hh * DV_A:(hh + 1) * DV_A], posb], axis=1)
            scores += [_dot(kx, rhs[2 * hh]), _dot(kx, rhs[2 * hh + 1])]
        for st, s in enumerate(scores):
            hs = slice(st // 2 * DV_A, (st // 2 + 1) * DV_A)
            if masked:
                kk = lax.broadcasted_iota(jnp.int32, s.shape, 0) + (j * kv - qi * tile)
                qq = lax.broadcasted_iota(jnp.int32, s.shape, 1)
                s = jnp.where(kk <= qq, s, -jnp.inf)
            m_old = m_ref[st]
            m_new = jnp.maximum(m_old, jnp.max(s, axis=0, keepdims=True))
            p = jnp.exp2(s - m_new).astype(BF16)
            vt = jnp.concatenate([vt_ref[0, j * per_kv + c, hs, :] for c in range(per_kv)], axis=1)
            vx = jnp.concatenate([vt, ones], axis=0)
            acc_ref[st] = jnp.exp2(m_old - m_new) * acc_ref[st] + _dot(vx, p)
            m_ref[st] = m_new

    def body(j, carry):
        step(j, False)
        return carry

    n_full = (qi * tile) // kv
    lax.fori_loop(0, n_full, body, 0)
    step(n_full, True)
    lam = _diff_lambda(lam_ref[...], lam_init)
    for hh in range(heads):
        acc0 = acc_ref[2 * hh]
        acc1 = acc_ref[2 * hh + 1]
        out0 = acc0[:DV_A] * (1.0 / acc0[DV_A:DV_A + 1])
        out1 = acc1[:DV_A] * (1.0 / acc1[DV_A:DV_A + 1])
        o_ref[:, hh * DV_A:(hh + 1) * DV_A] = (out0 - lam * out1).T


def diff_attn_prompt(aqt, akb, avt, posb, slopes, lam_vecs, lam_init):
    batch, nq, _, tile = aqt.shape
    seq = nq * tile
    kv = min(ATTN_KV_CHUNK, seq)
    heads = ATTN_HEADS_PER_STEP
    wide = heads * DV_A
    return pl.pallas_call(
        functools.partial(_attn_body, tile=tile, kv=kv, lam_init=lam_init),
        grid=(batch, H_A // heads, nq),
        in_specs=[pl.BlockSpec(memory_space=pltpu.SMEM),
                  pl.BlockSpec(lam_vecs.shape, lambda b, h, i: (0, 0)),
                  pl.BlockSpec((1, 1, wide, tile), lambda b, h, i: (b, i, h, 0)),
                  pl.BlockSpec((seq, wide), lambda b, h, i: (b, h)),
                  pl.BlockSpec((seq, LANES), lambda b, h, i: (0, 0)),
                  pl.BlockSpec((1, nq, wide, tile), lambda b, h, i: (b, 0, h, 0))],
        out_specs=pl.BlockSpec((tile, wide), lambda b, h, i: (b * nq + i, h)),
        out_shape=jax.ShapeDtypeStruct((batch * seq, W_A), F32),
        scratch_shapes=[pltpu.VMEM((2 * heads, 1, tile), F32), pltpu.VMEM((2 * heads, DV_A + 16, tile), F32)],
        compiler_params=_params("arbitrary", "arbitrary", "arbitrary"),
        name="diff_attn_prompt",
    )(slopes, lam_vecs, aqt, akb, posb, avt)


def _decode_body(pt_ref, q_ref, kn_ref, vn_ref, slope_ref, lam_ref, *rest, pages, past, lam_init):
    kt_refs = rest[:pages]
    v_refs = rest[pages:2 * pages]
    o_ref, m_ref, l_ref, acc_ref = rest[2 * pages:]
    j = pl.program_id(1)
    page = kt_refs[0].shape[3]
    groups = 2 * H_A

    @pl.when(j == 0)
    def _():
        m_ref[...] = jnp.full(m_ref.shape, -jnp.inf, F32)
        l_ref[...] = jnp.zeros(l_ref.shape, F32)
        acc_ref[...] = jnp.zeros(acc_ref.shape, F32)

    qrow = q_ref[0].astype(F32)
    lane = lax.broadcasted_iota(jnp.int32, (groups, W_A), 1)
    r8 = lax.broadcasted_iota(jnp.int32, (groups, W_A), 0)
    q8 = jnp.where(lane // DK_A == r8, jnp.broadcast_to(qrow, (groups, W_A)), 0.0)
    q8_bf = q8.astype(BF16)
    slope = slope_ref[...]
    col = lax.broadcasted_iota(jnp.int32, (1, page), 1)
    row_head = lax.broadcasted_iota(jnp.int32, (groups, 1), 0) // 2

    s_parts = []
    for i in range(pages):
        kpos = (col + (j * pages + i) * page).astype(F32)
        s_parts.append(_dot(q8_bf, kt_refs[i][0, 0].astype(BF16)) + slope * kpos)
    s = jnp.concatenate(s_parts, axis=1)
    m_old = m_ref[...]
    m_new = jnp.maximum(m_old, jnp.max(s, axis=1, keepdims=True))
    p = jnp.exp(s - m_new).astype(BF16)
    alpha = jnp.exp(m_old - m_new)
    pv = jnp.zeros(acc_ref.shape, F32)
    for i in range(pages):
        p_i = p[:, i * page:(i + 1) * page]
        for h in range(H_A):
            v_h = v_refs[i][0, 0, pl.ds(h, page, stride=H_A), :].astype(BF16)
            pv = pv + jnp.where(row_head == h, _dot(p_i, v_h), 0.0)
    l_ref[...] = alpha * l_ref[...] + jnp.sum(p.astype(F32), axis=1, keepdims=True)
    acc_ref[...] = alpha * acc_ref[...] + pv
    m_ref[...] = m_new

    @pl.when(j == pl.num_programs(1) - 1)
    def _():
        s_new = jnp.sum(q8 * kn_ref[0], axis=1, keepdims=True) + slope * float(past)
        v_new = jnp.zeros(acc_ref.shape, F32)
        for h in range(H_A):
            v_row = jnp.broadcast_to(vn_ref[0, :, h * DV_A:(h + 1) * DV_A], acc_ref.shape)
            v_new = jnp.where(row_head == h, v_row, v_new)
        m_old = m_ref[...]
        m_fin = jnp.maximum(m_old, s_new)
        p_new = jnp.exp(s_new - m_fin)
        alpha = jnp.exp(m_old - m_fin)
        l_fin = alpha * l_ref[...] + p_new
        o8 = (alpha * acc_ref[...] + p_new * v_new) / l_fin
        lam = _diff_lambda(lam_ref[...], lam_init)
        for h in range(H_A):
            o_ref[0, :, h * DV_A:(h + 1) * DV_A] = o8[2 * h:2 * h + 1] - lam * o8[2 * h + 1:2 * h + 2]


def diff_attn_decode(page_table, aq, ak, av, slope8, lam_vecs, lam_init, cache_kt, cache_v, layer):
    bd = aq.shape[0]
    n_pages = page_table.shape[1]
    page = cache_kt.shape[3]
    pages = math.gcd(DECODE_PAGES, n_pages)
    past = n_pages * page

    def page_spec(i, shape):
        return pl.BlockSpec((1, 1) + shape, lambda b, j, pt: (layer, pt[b * n_pages + j * pages + i], 0, 0))

    row3 = lambda: pl.BlockSpec((1, 1, W_A), lambda b, j, pt: (b, 0, 0))
    const = lambda a: pl.BlockSpec(a.shape, lambda b, j, pt: (0,) * a.ndim)
    grid_spec = pltpu.PrefetchScalarGridSpec(
        num_scalar_prefetch=1,
        grid=(bd, n_pages // pages),
        in_specs=[row3(), row3(), row3(), const(slope8), const(lam_vecs)]
        + [page_spec(i, (W_A, page)) for i in range(pages)]
        + [page_spec(i, (page * H_A, DV_A)) for i in range(pages)],
        out_specs=row3(),
        scratch_shapes=[pltpu.VMEM((2 * H_A, 1), F32), pltpu.VMEM((2 * H_A, 1), F32),
                        pltpu.VMEM((2 * H_A, DV_A), F32)],
    )
    out = pl.pallas_call(
        functools.partial(_decode_body, pages=pages, past=past, lam_init=lam_init),
        grid_spec=grid_spec,
        out_shape=jax.ShapeDtypeStruct((bd, 1, W_A), F32),
        compiler_params=_params("arbitrary", "arbitrary"),
        name="diff_attn_decode",
    )(page_table.reshape(-1), aq.reshape(bd, 1, W_A), ak.reshape(bd, 1, W_A), av.reshape(bd, 1, W_A),
      slope8, lam_vecs, *([cache_kt] * pages), *([cache_v] * pages))
    return out.reshape(bd, W_A)


def _gla_body(q_ref, k_ref, g_ref, v_ref, s0_ref, o_ref, sfin_ref, st_ref, *, chunk):
    n = pl.program_id(0)
    nb = q_ref.shape[0]
    pair = 2 * DK_B

    @pl.when(n == 0)
    def _():
        for b in range(nb):
            for p in range(H_B // 2):
                st_ref[b, p] = s0_ref[b, p * pair:(p + 1) * pair, :].T

    r = lax.broadcasted_iota(jnp.int32, (chunk, chunk), 0)
    c = lax.broadcasted_iota(jnp.int32, (chunk, chunk), 1)
    causal = r >= c
    tri = jnp.where(causal, 1.0, 0.0)
    lane_head = lax.broadcasted_iota(jnp.int32, (1, pair), 1) // DK_B
    mid = chunk // 2
    for b in range(nb):
        cum = jnp.dot(tri, g_ref[b], precision=HIGHEST, preferred_element_type=F32)
        c_mid = cum[mid - 1:mid, :]
        c_last = cum[chunk - 1:chunk, :]
        q = q_ref[b]
        k = k_ref[b]
        q_intra = (q * jnp.exp(cum - c_mid)).astype(BF16)
        k_intra = (k * jnp.exp(c_mid - cum)).astype(BF16)
        q_inter = (q * jnp.exp(cum)).astype(BF16)
        k_end = (k * jnp.exp(c_last - cum)).astype(BF16)
        decay = jnp.exp(c_last)
        zero = jnp.zeros((), BF16)
        for p in range(H_B // 2):
            sl = slice(p * pair, (p + 1) * pair)
            st = st_ref[b, p]
            st_bf = st.astype(BF16)
            st_new = st * decay[:, sl]
            for hh in range(2):
                h = 2 * p + hh
                own = lane_head == hh
                vsl = slice(h * DV_B, (h + 1) * DV_B)
                v = v_ref[b, :, vsl]
                a = _dot_nt(jnp.where(own, q_intra[:, sl], zero), k_intra[:, sl])
                a = jnp.where(causal, a, 0.0).astype(BF16)
                o = _dot(a, v.astype(BF16))
                o = o + _dot_nt(jnp.where(own, q_inter[:, sl], zero), st_bf)
                o_ref[b, :, vsl] = o
                st_new = st_new + _dot(v.T.astype(BF16), jnp.where(own, k_end[:, sl], zero))
            st_ref[b, p] = st_new

    @pl.when(n == pl.num_programs(0) - 1)
    def _():
        for b in range(nb):
            for p in range(H_B // 2):
                sfin_ref[b, p * pair:(p + 1) * pair, :] = st_ref[b, p].T


def gla_prompt(bq, bk, glog, bv, s0, batch):
    m = bq.shape[0]
    seq = m // batch
    chunk = min(GLA_CHUNK, seq)
    blk = lambda w: pl.BlockSpec((batch, chunk, w), lambda n: (0, n, 0))
    state = pl.BlockSpec((batch, W_BK, DV_B), lambda n: (0, 0, 0))
    o, s_fin = pl.pallas_call(
        functools.partial(_gla_body, chunk=chunk),
        grid=(seq // chunk,),
        in_specs=[blk(W_BK), blk(W_BK), blk(W_BK), blk(W_BV), state],
        out_specs=[blk(W_BV), state],
        out_shape=[jax.ShapeDtypeStruct((batch, seq, W_BV), F32),
                   jax.ShapeDtypeStruct((batch, W_BK, DV_B), F32)],
        scratch_shapes=[pltpu.VMEM((batch, H_B // 2, DV_B, 2 * DK_B), F32)],
        compiler_params=_params("arbitrary"),
        name="gla_prompt",
    )(bq.reshape(batch, seq, W_BK), bk.reshape(batch, seq, W_BK), glog.reshape(batch, seq, W_BK),
      bv.reshape(batch, seq, W_BV), s0)
    return o.reshape(m, W_BV), s_fin


def _column(row):
    n = row.shape[1]
    r = lax.broadcasted_iota(jnp.int32, (n, n), 0)
    c = lax.broadcasted_iota(jnp.int32, (n, n), 1)
    return jnp.sum(jnp.where(r == c, jnp.broadcast_to(row, (n, n)), 0.0), axis=1, keepdims=True)


def _gla_step_body(q_ref, k_ref, g_ref, v_ref, s0_ref, o_ref, s_ref):
    pair = 2 * DK_B
    for p in range(H_B // 2):
        sl = slice(p * pair, (p + 1) * pair)
        qc = _column(q_ref[0, :, sl])
        kc = _column(k_ref[0, :, sl])
        dc = jnp.exp(_column(g_ref[0, :, sl]))
        for hh in range(2):
            h = 2 * p + hh
            rs = slice(hh * DK_B, (hh + 1) * DK_B)
            v = v_ref[0, :, h * DV_B:(h + 1) * DV_B]
            s_new = dc[rs] * s0_ref[0, h] + kc[rs] * v
            s_ref[0, h] = s_new
            o_ref[0, :, h * DV_B:(h + 1) * DV_B] = jnp.sum(qc[rs] * s_new, axis=0, keepdims=True)


def gla_step(bq, bk, glog, bv, s0):
    bd = bq.shape[0]
    row = lambda w: pl.BlockSpec((1, 1, w), lambda b: (b, 0, 0))
    state = pl.BlockSpec((1, H_B, DK_B, DV_B), lambda b: (b, 0, 0, 0))
    o, s_fin = pl.pallas_call(
        _gla_step_body,
        grid=(bd,),
        in_specs=[row(W_BK), row(W_BK), row(W_BK), row(W_BV), state],
        out_specs=[row(W_BV), state],
        out_shape=[jax.ShapeDtypeStruct((bd, 1, W_BV), F32), jax.ShapeDtypeStruct(s0.shape, F32)],
        compiler_params=_params("arbitrary"),
        name="gla_step",
    )(bq.reshape(bd, 1, W_BK), bk.reshape(bd, 1, W_BK), glog.reshape(bd, 1, W_BK),
      bv.reshape(bd, 1, W_BV), s0)
    return o.reshape(bd, W_BV), s_fin


def _head_rms(t, g, width):
    parts = []
    for lo in range(0, t.shape[1], width):
        parts.append(_rms(t[:, lo:lo + width], g[:, lo:lo + width]))
    return jnp.concatenate(parts, axis=1)


def _even_out_body(x_ref, a_ref, b_ref, br_ref, ga_ref, gb_ref, w_ref, o_ref, *, lam_init):
    a = _head_rms(a_ref[...], ga_ref[...], DV_A) * (1.0 - lam_init)
    br = br_ref[...]
    b = _head_rms(b_ref[...], gb_ref[...], DV_B) * (br * jax.nn.sigmoid(br))
    merged = jnp.concatenate([a, b], axis=1).astype(BF16)
    o_ref[...] = x_ref[...] + _dot(merged, w_ref[...])


def even_output(x, a_out, b_out, br, ga, gb, w, lam_init):
    m, d = x.shape
    tm = _row_tile(m)
    return pl.pallas_call(
        functools.partial(_even_out_body, lam_init=lam_init),
        grid=(m // tm,),
        in_specs=[_rows(tm, d), _rows(tm, W_A), _rows(tm, W_BV), _rows(tm, W_BV),
                  _resident(ga.shape), _resident(gb.shape), _resident(w.shape)],
        out_specs=_rows(tm, d),
        out_shape=jax.ShapeDtypeStruct((m, d), F32),
        compiler_params=_params("arbitrary"),
        name="even_output",
    )(x, a_out, b_out, br, ga, gb, w)


O_Q, O_K, O_V, O_O, O_G = 0, W_C, 2 * W_C, 3 * W_C, 4 * W_C
O_COLS = O_G + LANES


def _odd_gates(xn, w_ref, bias_ref):
    z = _dot(xn, w_ref[:, O_G:O_G + LANES]) + bias_ref[...]
    lane = lax.broadcasted_iota(jnp.int32, z.shape, 1)
    return jnp.where(lane < H_C, z, _log_sigmoid(z))


def _odd_in_step_body(x_ref, g_ref, w_ref, bias_ref, q_ref, k_ref, v_ref, o_ref, gt_ref):
    xn = _rms(x_ref[...], g_ref[...]).astype(BF16)
    proj = lambda lo, n: _dot(xn, w_ref[:, lo:lo + n])
    q_ref[...] = proj(O_Q, W_C).astype(BF16)
    k_ref[...] = (proj(O_K, W_C) * DH_C ** -0.5).astype(BF16)
    v_ref[...] = proj(O_V, W_C)
    o_ref[...] = proj(O_O, W_C)
    gt_ref[...] = _odd_gates(xn, w_ref, bias_ref)


def _odd_in_chunk_body(x_ref, g_ref, w_ref, wqt_ref, wvt_ref, bias_ref,
                       qt_ref, k_ref, vt_ref, o_ref, gt_ref, gtt_ref, *, chunk):
    xn = _rms(x_ref[...], g_ref[...]).astype(BF16)
    proj = lambda lo, n: _dot(xn, w_ref[:, lo:lo + n])
    qt = _dot_nt(wqt_ref[...], xn).astype(BF16)
    vt = _dot_nt(wvt_ref[...], xn).astype(BF16)
    for c in range(qt_ref.shape[1]):
        qt_ref[0, c] = qt[:, c * chunk:(c + 1) * chunk]
        vt_ref[0, c] = vt[:, c * chunk:(c + 1) * chunk]
    k_ref[...] = (proj(O_K, W_C) * DH_C ** -0.5).astype(BF16)
    o_ref[...] = proj(O_O, W_C)
    gates = _odd_gates(xn, w_ref, bias_ref)
    gt_ref[...] = gates
    gtt_ref[...] = gates.T[:2 * H_C, :]


def odd_inputs(x, g, w, bias, prompt_batch=None):
    m, d = x.shape
    tm = _row_tile(m)
    if prompt_batch is None:
        widths = [(W_C, BF16), (W_C, BF16), (W_C, F32), (W_C, F32), (LANES, F32)]
        return pl.pallas_call(
            _odd_in_step_body,
            grid=(m // tm,),
            in_specs=[_rows(tm, d), _resident((1, d)), _resident(w.shape), _resident(bias.shape)],
            out_specs=[_rows(tm, n) for n, _ in widths],
            out_shape=[jax.ShapeDtypeStruct((m, n), dt) for n, dt in widths],
            compiler_params=_params("arbitrary"),
            name="odd_inputs",
        )(x, g, w, bias)
    seq = m // prompt_batch
    chunk = min(MLSTM_CHUNK, seq)
    per_seq = seq // tm
    chunked = pl.BlockSpec((1, tm // chunk, W_C, chunk), lambda i: (i // per_seq, i % per_seq, 0, 0))
    chunked_shape = jax.ShapeDtypeStruct((prompt_batch, seq // chunk, W_C, chunk), BF16)
    wqt = w[:, O_Q:O_Q + W_C].T
    wvt = w[:, O_V:O_V + W_C].T
    return pl.pallas_call(
        functools.partial(_odd_in_chunk_body, chunk=chunk),
        grid=(m // tm,),
        in_specs=[_rows(tm, d), _resident((1, d)), _resident(w.shape), _resident(wqt.shape),
                  _resident(wvt.shape), _resident(bias.shape)],
        out_specs=[chunked, _rows(tm, W_C), chunked, _rows(tm, W_C), _rows(tm, LANES),
                   pl.BlockSpec((2 * H_C, tm), lambda i: (0, i))],
        out_shape=[chunked_shape, jax.ShapeDtypeStruct((m, W_C), BF16), chunked_shape,
                   jax.ShapeDtypeStruct((m, W_C), F32), jax.ShapeDtypeStruct((m, LANES), F32),
                   jax.ShapeDtypeStruct((2 * H_C, m), F32)],
        compiler_params=_params("arbitrary"),
        name="odd_inputs",
    )(x, g, w, wqt, wvt, bias)


def _mlstm_body(qt_ref, k_ref, vt_ref, gt_ref, gtt_ref, c0_ref, n0_ref, m0_ref,
                h_ref, cfin_ref, nfin_ref, mfin_ref, c_sc, n_sc, m_sc, *, chunk):
    n = pl.program_id(1)

    @pl.when(n == 0)
    def _():
        c_sc[...] = c0_ref[0]
        n_sc[...] = n0_ref[0]
        m_sc[...] = m0_ref[0]

    r = lax.broadcasted_iota(jnp.int32, (chunk, chunk), 0)
    c = lax.broadcasted_iota(jnp.int32, (chunk, chunk), 1)
    past = r <= c
    gates = gt_ref[...]
    gates_t = gtt_ref[...]
    cum_col = jnp.dot(jnp.where(r >= c, 1.0, 0.0), gates, precision=HIGHEST, preferred_element_type=F32)
    cum_row = jnp.dot(gates_t, jnp.where(past, 1.0, 0.0), precision=HIGHEST, preferred_element_type=F32)
    ones = jnp.ones((16, chunk), BF16)
    for h in range(H_C):
        sl = slice(h * DH_C, (h + 1) * DH_C)
        i_row = gates_t[h:h + 1, :]
        b_row = cum_row[H_C + h:H_C + h + 1, :]
        b_last = b_row[:, chunk - 1:chunk]
        m_prev = m_sc[h:h + 1, 0:1]
        src_col = gates[:, h:h + 1] - cum_col[:, H_C + h:H_C + h + 1]
        dlog = jnp.where(past, src_col + b_row, -jnp.inf)
        prev_log = b_row + m_prev
        m_t = jnp.maximum(prev_log, jnp.max(dlog, axis=0, keepdims=True))
        dw = jnp.exp(dlog - m_t)
        pw = jnp.exp(prev_log - m_t)
        qt = qt_ref[0, 0, sl, :]
        vt = vt_ref[0, 0, sl, :]
        k = k_ref[:, sl]
        sw = (_dot(k, qt) * dw).astype(BF16)
        c_old = c_sc[h]
        n_old = n_sc[h:h + 1, :]
        intra = _dot(jnp.concatenate([vt, ones], axis=0), sw)
        n8 = jnp.broadcast_to(n_old, (8, DH_C)).astype(BF16)
        num = intra[:DH_C] + pw * _dot(c_old.astype(BF16), qt)
        den = intra[DH_C:DH_C + 1] + pw * _dot(n8, qt)[0:1]
        h_t = num * (1.0 / jnp.maximum(jnp.abs(den), jnp.exp(-m_t)))
        h_ref[:, sl] = h_t.T
        m_end = m_t[:, chunk - 1:chunk]
        wk = jnp.exp(b_last - b_row + i_row - m_end)
        a = jnp.exp(b_last + m_prev - m_end)
        c_sc[h] = a * c_old + _dot((vt.astype(F32) * wk).astype(BF16), k)
        wk8 = jnp.broadcast_to(wk, (8, chunk)).astype(BF16)
        n_sc[h:h + 1, :] = a * n_old + _dot(wk8, k)[0:1]
        m_sc[h:h + 1, :] = jnp.broadcast_to(m_end, (1, LANES))

    @pl.when(n == pl.num_programs(1) - 1)
    def _():
        cfin_ref[0] = c_sc[...]
        nfin_ref[0] = n_sc[...]
        mfin_ref[0] = m_sc[...]


def mlstm_prompt(qt, k, vt, gates, gates_t, c0, n0, m0):
    batch, nc, _, chunk = qt.shape
    m = k.shape[0]
    rows = lambda w: pl.BlockSpec((chunk, w), lambda b, n: (b * nc + n, 0))
    chunked = pl.BlockSpec((1, 1, W_C, chunk), lambda b, n: (b, n, 0, 0))
    st_c = pl.BlockSpec((1, H_C, DH_C, DH_C), lambda b, n: (b, 0, 0, 0))
    st_n = pl.BlockSpec((1, H_C, DH_C), lambda b, n: (b, 0, 0))
    st_m = pl.BlockSpec((1, H_C, LANES), lambda b, n: (b, 0, 0))
    return pl.pallas_call(
        functools.partial(_mlstm_body, chunk=chunk),
        grid=(batch, nc),
        in_specs=[chunked, rows(W_C), chunked, rows(LANES),
                  pl.BlockSpec((2 * H_C, chunk), lambda b, n: (0, b * nc + n)), st_c, st_n, st_m],
        out_specs=[rows(W_C), st_c, st_n, st_m],
        out_shape=[jax.ShapeDtypeStruct((m, W_C), F32), jax.ShapeDtypeStruct(c0.shape, F32),
                   jax.ShapeDtypeStruct(n0.shape, F32), jax.ShapeDtypeStruct(m0.shape, F32)],
        scratch_shapes=[pltpu.VMEM((H_C, DH_C, DH_C), F32), pltpu.VMEM((H_C, DH_C), F32),
                        pltpu.VMEM((H_C, LANES), F32)],
        compiler_params=_params("arbitrary", "arbitrary"),
        name="mlstm_prompt",
    )(qt, k, vt, gates, gates_t, c0, n0, m0)


def _mlstm_step_body(q_ref, k_ref, v_ref, gt_ref, c0_ref, n0_ref, m0_ref, h_ref, c_ref, n_ref, m_ref):
    gates = gt_ref[0]
    for h in range(H_C):
        sl = slice(h * DH_C, (h + 1) * DH_C)
        i_pre = gates[:, h:h + 1]
        logf = gates[:, H_C + h:H_C + h + 1]
        m_prev = m0_ref[0, h:h + 1, 0:1]
        prev_log = logf + m_prev
        m_t = jnp.maximum(prev_log, i_pre)
        dw = jnp.exp(i_pre - m_t)
        pw = jnp.exp(prev_log - m_t)
        q = q_ref[0, :, sl]
        k = k_ref[0, :, sl]
        v = v_ref[0, :, sl]
        qf = q.astype(F32)
        kf = k.astype(F32)
        c_old = c0_ref[0, h]
        n_old = n0_ref[0, h:h + 1, :]
        sw = jnp.sum(qf * kf, axis=1, keepdims=True) * dw
        cq = _dot_nt(jnp.broadcast_to(q, (8, DH_C)), c_old.astype(BF16))[0:1]
        num = sw * v + pw * cq
        den = sw + pw * jnp.sum(qf * n_old, axis=1, keepdims=True)
        h_ref[0, :, sl] = num / jnp.maximum(jnp.abs(den), jnp.exp(-m_t))
        c_ref[0, h] = pw * c_old + _column(dw * v) * kf
        n_ref[0, h:h + 1, :] = pw * n_old + dw * kf
        m_ref[0, h:h + 1, :] = jnp.broadcast_to(m_t, (1, LANES))


def mlstm_step(q, k, v, gates, c0, n0, m0):
    bd = q.shape[0]
    row = lambda w: pl.BlockSpec((1, 1, w), lambda b: (b, 0, 0))
    st_c = pl.BlockSpec((1, H_C, DH_C, DH_C), lambda b: (b, 0, 0, 0))
    st_n = pl.BlockSpec((1, H_C, DH_C), lambda b: (b, 0, 0))
    st_m = pl.BlockSpec((1, H_C, LANES), lambda b: (b, 0, 0))
    h, c_fin, n_fin, m_fin = pl.pallas_call(
        _mlstm_step_body,
        grid=(bd,),
        in_specs=[row(W_C), row(W_C), row(W_C), row(LANES), st_c, st_n, st_m],
        out_specs=[row(W_C), st_c, st_n, st_m],
        out_shape=[jax.ShapeDtypeStruct((bd, 1, W_C), F32), jax.ShapeDtypeStruct(c0.shape, F32),
                   jax.ShapeDtypeStruct(n0.shape, F32), jax.ShapeDtypeStruct(m0.shape, F32)],
        compiler_params=_params("arbitrary"),
        name="mlstm_step",
    )(q.reshape(bd, 1, W_C), k.reshape(bd, 1, W_C), v.reshape(bd, 1, W_C), gates.reshape(bd, 1, LANES),
      c0, n0, m0)
    return h.reshape(bd, W_C), c_fin, n_fin, m_fin


def _odd_out_body(x_ref, h_ref, o_ref, g_ref, w_ref, y_ref):
    y = jax.nn.sigmoid(o_ref[...]) * _head_rms(h_ref[...], g_ref[...], DH_C)
    y_ref[...] = x_ref[...] + _dot(y.astype(BF16), w_ref[...])


def odd_output(x, h, o, g, w):
    m, d = x.shape
    tm = _row_tile(m)
    return pl.pallas_call(
        _odd_out_body,
        grid=(m // tm,),
        in_specs=[_rows(tm, d), _rows(tm, W_C), _rows(tm, W_C), _resident(g.shape), _resident(w.shape)],
        out_specs=_rows(tm, d),
        out_shape=jax.ShapeDtypeStruct((m, d), F32),
        compiler_params=_params("arbitrary"),
        name="odd_output",
    )(x, h, o, g, w)


def _even_weights(w_in, w_out, qk_norm, gate_w2, gate_b, a_norm, b_norm):
    main = 3 * W_A + 2 * W_BK + W_BV
    w = jnp.concatenate([w_in[:, :main], w_in[:, main + GLA_RANK:], w_in[:, main:main + GLA_RANK],
                         jnp.zeros((w_in.shape[0], LANES - GLA_RANK), w_in.dtype)], axis=1).astype(BF16)
    w2 = jnp.concatenate([gate_w2, jnp.zeros((LANES - GLA_RANK, W_BK), gate_w2.dtype)], axis=0).astype(BF16)
    idx = jnp.arange(W_A) // DK_A
    seg = (idx[:, None] == idx[None, :]).astype(BF16)
    gq = jnp.tile(qk_norm[0], W_A // DK_A).reshape(1, W_A)
    gk = jnp.tile(qk_norm[1], W_A // DK_A).reshape(1, W_A)
    return dict(w=w, w2=w2, seg=seg, gq=gq, gk=gk, gb=gate_b.reshape(1, W_BK), w_out=w_out.astype(BF16),
                ga=a_norm.reshape(1, W_A), gbn=b_norm.reshape(1, W_BV))


def _odd_weights(w_in, w_out, gate_b, norm_g):
    w = jnp.concatenate([w_in, jnp.zeros((w_in.shape[0], LANES - 2 * H_C), w_in.dtype)], axis=1).astype(BF16)
    bias = jnp.concatenate([gate_b[0], gate_b[1], jnp.zeros((LANES - 2 * H_C,), gate_b.dtype)]).reshape(1, LANES)
    return dict(w=w, bias=bias, w_out=w_out.astype(BF16), g=norm_g.reshape(1, W_C))


def kernel(x_prompt, x_sample, cache_k, cache_v, state_gla, state_mlstm_C, state_mlstm_n, state_mlstm_m,
           page_table, norm_g, ffn_w_gate, ffn_w_up, ffn_w_down, even_w_in, even_w_out, a_qk_norm, a_lambda,
           a_head_norm, b_gate_w2, b_gate_bias, b_head_norm, odd_w_in, odd_w_out, c_gate_bias, c_head_norm):
    bp, seq, d = x_prompt.shape
    bd = x_sample.shape[0]
    depth = norm_g.shape[0]
    assert x_sample.shape[1] == 1
    yp = x_prompt.reshape(bp * seq, d)
    ys = x_sample.reshape(bd, d)
    n_even, n_pool, page = cache_k.shape[:3]
    ck = jnp.transpose(cache_k, (0, 1, 3, 4, 5, 2)).reshape(n_even, n_pool, W_A, page)
    cv = cache_v.reshape(n_even, n_pool, page * H_A, DV_A)
    slopes = jnp.asarray([2.0 ** (-8.0 * (h + 1) / H_A) for h in range(H_A)], F32)
    slope8 = jnp.repeat(slopes, 2).reshape(2 * H_A, 1)
    wg, wu, wd = ffn_w_gate.astype(BF16), ffn_w_up.astype(BF16), ffn_w_down.astype(BF16)
    pos = jnp.arange(seq)[:, None]
    lane = jnp.arange(LANES)[None, :]
    posb = jnp.where(lane >= 6, 0, jnp.where(lane % 2 == 0, pos // 64 * 64, pos % 64)).astype(BF16)

    k_p, v_p, k_s, v_s, gla_p, gla_s = [], [], [], [], [], []
    cm_p, nm_p, mm_p, cm_s, nm_s, mm_s = [], [], [], [], [], []
    for li in range(depth):
        g = norm_g[li].reshape(3, 1, d)
        yp = ffn_half(yp, g[0], wg, wu, wd, li, 0)
        ys = ffn_half(ys, g[0], wg, wu, wd, li, 0)
        if li % 2 == 0:
            e = li // 2
            lam_init = 0.8 - 0.6 * math.exp(-0.3 * li)
            ew = _even_weights(even_w_in[e], even_w_out[e], a_qk_norm[e], b_gate_w2[e], b_gate_bias[e],
                               a_head_norm[e], b_head_norm[e])
            lam_vecs = a_lambda[e]
            aqt, akb, akt, av, avt, bq, bk, bv, gl, br = even_inputs(
                yp, g[1], ew["w"], ew["seg"], ew["gq"], ew["gk"], ew["w2"], ew["gb"], prompt_batch=bp)
            a_out = diff_attn_prompt(aqt, akb, avt, posb, slopes, lam_vecs, lam_init)
            b_out, s_fin = gla_prompt(bq, bk, gl, bv, jnp.zeros((bp, W_BK, DV_B), F32), bp)
            yp = even_output(yp, a_out, b_out, br, ew["ga"], ew["gbn"], ew["w_out"], lam_init)
            k_p.append(jnp.transpose(akt.reshape(bp, H_A, 2, DK_A, seq), (0, 4, 1, 2, 3)))
            v_p.append(av.reshape(bp, seq, H_A, DV_A))
            gla_p.append(s_fin.reshape(bp, H_B, DK_B, DV_B))
            aq, ak, av, bq, bk, bv, gl, br = even_inputs(
                ys, g[1], ew["w"], ew["seg"], ew["gq"], ew["gk"], ew["w2"], ew["gb"])
            a_out = diff_attn_decode(page_table, aq, ak, av, slope8, lam_vecs, lam_init, ck, cv, e)
            b_out, s_fin = gla_step(bq, bk, gl, bv, state_gla[e])
            ys = even_output(ys, a_out, b_out, br, ew["ga"], ew["gbn"], ew["w_out"], lam_init)
            k_s.append(ak.reshape(bd, 1, H_A, 2, DK_A))
            v_s.append(av.reshape(bd, 1, H_A, DV_A))
            gla_s.append(s_fin)
        else:
            o = li // 2
            ow = _odd_weights(odd_w_in[o], odd_w_out[o], c_gate_bias[o], c_head_norm[o])
            qt, k, vt, og, gates, gates_t = odd_inputs(yp, g[1], ow["w"], ow["bias"], prompt_batch=bp)
            h, c_fin, n_fin, m_fin = mlstm_prompt(
                qt, k, vt, gates, gates_t, jnp.zeros((bp, H_C, DH_C, DH_C), F32),
                jnp.zeros((bp, H_C, DH_C), F32), jnp.zeros((bp, H_C, LANES), F32))
            yp = odd_output(yp, h, og, ow["g"], ow["w_out"])
            cm_p.append(c_fin)
            nm_p.append(n_fin)
            mm_p.append(m_fin[:, :, 0])
            q, k, v, og, gates = odd_inputs(ys, g[1], ow["w"], ow["bias"])
            m0 = jnp.broadcast_to(state_mlstm_m[o][:, :, None], (bd, H_C, LANES))
            h, c_fin, n_fin, m_fin = mlstm_step(q, k, v, gates, state_mlstm_C[o], state_mlstm_n[o], m0)
            ys = odd_output(ys, h, og, ow["g"], ow["w_out"])
            cm_s.append(c_fin)
            nm_s.append(n_fin)
            mm_s.append(m_fin[:, :, 0])
        yp = ffn_half(yp, g[2], wg, wu, wd, li, 1)
        ys = ffn_half(ys, g[2], wg, wu, wd, li, 1)
    return (yp.reshape(bp, seq, d), ys.reshape(bd, 1, d),
            jnp.stack(k_p), jnp.stack(v_p), jnp.stack(k_s), jnp.stack(v_s),
            jnp.stack(gla_p), jnp.stack(gla_s),
            jnp.stack(cm_p), jnp.stack(nm_p), jnp.stack(mm_p),
            jnp.stack(cm_s), jnp.stack(nm_s), jnp.stack(mm_s))
```

```python
import functools
import math

import jax
import jax.numpy as jnp
from jax import lax
from jax.experimental import pallas as pl
from jax.experimental.pallas import tpu as pltpu

F32 = jnp.float32
BF16 = jnp.bfloat16
EPS = 1e-6
HIGHEST = lax.Precision.HIGHEST
LOG2E = math.log2(math.e)

LANES = 128
MXU_WIDTH = 256
VMEM_LIMIT_BYTES = 56 * 2**20

H_A, DK_A, DV_A = 4, 64, 128
H_B, DK_B, DV_B = 4, 64, 128
GLA_RANK = 16
GLA_NORMALIZER = 16.0
H_C, DH_C = 4, 256
W_A = H_A * 2 * DK_A
W_BK = H_B * DK_B
W_BV = H_B * DV_B
W_C = H_C * DH_C

ROW_TILE = 512
ATTN_TILE = 256
ATTN_KV_CHUNK = 1024
ATTN_HEADS_PER_STEP = 4
GLA_CHUNK = 128
MLSTM_CHUNK = 256
DECODE_PAGES = 16
GLA_STEP_SEQS = 8
MLSTM_STEP_SEQS = 2

NT_DIMS = (((1,), (1,)), ((), ()))


def _params(*semantics):
    return pltpu.CompilerParams(dimension_semantics=semantics, vmem_limit_bytes=VMEM_LIMIT_BYTES)


def _resident(shape):
    return pl.BlockSpec(shape, lambda *_: (0,) * len(shape), pipeline_mode=pl.Buffered(1))


def _rows(tm, width):
    return pl.BlockSpec((tm, width), lambda i: (i, 0))


def _row_tile(m):
    return ROW_TILE if m % ROW_TILE == 0 else m


def _rms(x, g):
    return x * lax.rsqrt(jnp.mean(x * x, axis=-1, keepdims=True) + EPS) * g


def _log_sigmoid(x):
    return jnp.minimum(x, 0.0) - jnp.log1p(jnp.exp(-jnp.abs(x)))


def _dot(a, b):
    return jnp.dot(a, b, preferred_element_type=F32)


def _dot_nt(a, b):
    return lax.dot_general(a, b, NT_DIMS, preferred_element_type=F32)


def _diff_lambda(lam_vecs, lam_init):
    p1 = jnp.sum(lam_vecs[0:1] * lam_vecs[1:2], axis=-1, keepdims=True)
    p2 = jnp.sum(lam_vecs[2:3] * lam_vecs[3:4], axis=-1, keepdims=True)
    return jnp.exp(p1) - jnp.exp(p2) + lam_init


FF_CHUNK = 256


def _swiglu_half(x, g_ref, wg_ref, wu_ref, wd_ref, h_ref):
    xn = _rms(x, g_ref[...]).astype(BF16)
    for c in range(0, wg_ref.shape[1], FF_CHUNK):
        hg = _dot(xn, wg_ref[:, c:c + FF_CHUNK])
        hu = _dot(xn, wu_ref[:, c:c + FF_CHUNK])
        h_ref[:, c:c + FF_CHUNK] = (hg * jax.nn.sigmoid(hg) * hu).astype(BF16)
    return x + 0.5 * _dot(h_ref[...], wd_ref[...])


def _ffn_body(x_ref, g_ref, wg_ref, wu_ref, wd_ref, o_ref, h_ref):
    o_ref[...] = _swiglu_half(x_ref[...], g_ref, wg_ref, wu_ref, wd_ref, h_ref)


def _ffn_call(body, name, row_inputs, small_inputs, g, wg, wu, wd, layer, half):
    m, d = row_inputs[0].shape
    dff = wg.shape[-1]
    tm = _row_tile(m)
    weight = lambda r, c: pl.BlockSpec((None, None, r, c), lambda i: (layer, half, 0, 0),
                                       pipeline_mode=pl.Buffered(1))
    return pl.pallas_call(
        body,
        grid=(m // tm,),
        in_specs=([_rows(tm, a.shape[1]) for a in row_inputs] + [_resident(a.shape) for a in small_inputs]
                  + [_resident((1, d)), weight(d, dff), weight(d, dff), weight(dff, d)]),
        out_specs=_rows(tm, d),
        out_shape=jax.ShapeDtypeStruct((m, d), F32),
        scratch_shapes=[pltpu.VMEM((tm, dff), BF16)],
        compiler_params=_params("arbitrary"),
        name=name,
    )(*row_inputs, *small_inputs, g, wg, wu, wd)


def ffn_half(x, g, wg, wu, wd, layer, half):
    return _ffn_call(_ffn_body, "ffn_half", [x], [], g, wg, wu, wd, layer, half)


E_AQ, E_AK, E_AV = 0, W_A, 2 * W_A
E_BQ = 3 * W_A
E_BK = E_BQ + W_BK
E_BV = E_BK + W_BK
E_BR = E_BV + W_BV
E_BG = E_BR + W_BV
E_COLS = E_BG + LANES


def _even_in_body(x_ref, g_ref, w_ref, seg_ref, gq_ref, gk_ref, w2_ref, gb_ref, *out_refs, attn_tile):
    xn = _rms(x_ref[...], g_ref[...]).astype(BF16)
    proj = lambda lo, n: _dot(xn, w_ref[:, lo:lo + n])

    def group_rms(t, g):
        t2 = (t * t).astype(BF16)
        ms = jnp.concatenate([_dot(t2[:, c:c + MXU_WIDTH], seg_ref[...])
                              for c in range(0, t.shape[1], MXU_WIDTH)], axis=1)
        return t * lax.rsqrt(ms * (1.0 / DK_A) + EPS) * g

    aq = group_rms(proj(E_AQ, W_A), gq_ref[...]) * DK_A ** -0.5
    ak = group_rms(proj(E_AK, W_A), gk_ref[...])
    av = proj(E_AV, W_A)
    if attn_tile is None:
        aq_ref, ak_ref, av_ref, bq_ref, bk_ref, bv_ref, gl_ref, br_ref = out_refs
        aq_ref[...] = aq.astype(BF16)
        ak_ref[...] = ak
        av_ref[...] = av
    else:
        aqt_ref, akb_ref, akt_ref, av_ref, avt_ref, bq_ref, bk_ref, bv_ref, gl_ref, br_ref = out_refs
        aqt = (aq * LOG2E).T.astype(BF16)
        avt = av.T.astype(BF16)
        for c in range(aqt_ref.shape[1]):
            aqt_ref[0, c] = aqt[:, c * attn_tile:(c + 1) * attn_tile]
            avt_ref[0, c] = avt[:, c * attn_tile:(c + 1) * attn_tile]
        akb_ref[...] = ak.astype(BF16)
        akt_ref[0] = ak.T
        for h in range(H_A):
            av_ref[pl.ds(h, av.shape[0], stride=H_A), :] = av[:, h * DV_A:(h + 1) * DV_A]
    bq_ref[...] = proj(E_BQ, W_BK) * DK_B ** -0.5
    bk_ref[...] = proj(E_BK, W_BK)
    bv_ref[...] = proj(E_BV, W_BV)
    br_ref[...] = proj(E_BR, W_BV)
    bg = proj(E_BG, LANES).astype(BF16)
    gl_ref[...] = _log_sigmoid(_dot(bg, w2_ref[...]) + gb_ref[...]) * (1.0 / GLA_NORMALIZER)


def even_inputs(x, g, w, seg, gq, gk, w2, gb, prompt_batch=None):
    m, d = x.shape
    tm = _row_tile(m)
    common = [(W_BK, F32), (W_BK, F32), (W_BV, F32), (W_BK, F32), (W_BV, F32)]
    if prompt_batch is None:
        attn_tile = None
        widths = [(W_A, BF16), (W_A, F32), (W_A, F32)] + common
        out_specs = [_rows(tm, n) for n, _ in widths]
        out_shape = [jax.ShapeDtypeStruct((m, n), dt) for n, dt in widths]
    else:
        seq = m // prompt_batch
        attn_tile = min(ATTN_TILE, seq)
        per_seq = seq // tm
        per_tile = tm // attn_tile
        chunked = pl.BlockSpec((1, per_tile, W_A, attn_tile), lambda i: (i // per_seq, i % per_seq, 0, 0))
        chunked_shape = jax.ShapeDtypeStruct((prompt_batch, seq // attn_tile, W_A, attn_tile), BF16)
        out_specs = ([chunked, _rows(tm, W_A),
                      pl.BlockSpec((1, W_A, tm), lambda i: (i // per_seq, 0, i % per_seq)),
                      _rows(tm * H_A, DV_A), chunked] + [_rows(tm, n) for n, _ in common])
        out_shape = ([chunked_shape, jax.ShapeDtypeStruct((m, W_A), BF16),
                      jax.ShapeDtypeStruct((prompt_batch, W_A, seq), F32),
                      jax.ShapeDtypeStruct((m * H_A, DV_A), F32), chunked_shape]
                     + [jax.ShapeDtypeStruct((m, n), dt) for n, dt in common])
    return pl.pallas_call(
        functools.partial(_even_in_body, attn_tile=attn_tile),
        grid=(m // tm,),
        in_specs=[_rows(tm, d), _resident((1, d)), _resident(w.shape), _resident(seg.shape),
                  _resident(gq.shape), _resident(gk.shape), _resident(w2.shape), _resident(gb.shape)],
        out_specs=out_specs,
        out_shape=out_shape,
        compiler_params=_params("arbitrary"),
        name="even_inputs",
    )(x, g, w, seg, gq, gk, w2, gb)


def _attn_body(slope_ref, lam_ref, qt_ref, k_ref, posb_ref, vt_ref, o_ref, m_ref, acc_ref, *, tile, kv, lam_init):
    qi = pl.program_id(2)
    heads = m_ref.shape[0] // 2
    row = lax.broadcasted_iota(jnp.int32, (DV_A, tile), 0)
    erow = lax.broadcasted_iota(jnp.int32, (LANES, tile), 0)
    rhs = []
    for hh in range(heads):
        slope = slope_ref[pl.program_id(1) * heads + hh]
        qt = qt_ref[0, 0, hh * DV_A:(hh + 1) * DV_A, :]
        c_all = jnp.full((LANES, tile), slope * LOG2E, F32)
        c_1 = c_all.astype(BF16).astype(F32)
        c_2 = (c_all - c_1).astype(BF16).astype(F32)
        c_3 = (c_all - c_1 - c_2).astype(BF16).astype(F32)
        ext = jnp.where(erow < 2, c_1, jnp.where(erow < 4, c_2, jnp.where(erow < 6, c_3, 0.0))).astype(BF16)
        for mm in range(2):
            own = (row >= mm * DK_A) & (row < (mm + 1) * DK_A)
            rhs.append(jnp.concatenate([jnp.where(own, qt, jnp.zeros_like(qt)), ext], axis=0))
    ones = jnp.ones((16, kv), BF16)
    m_ref[...] = jnp.full(m_ref.shape, -jnp.inf, F32)
    acc_ref[...] = jnp.zeros(acc_ref.shape, F32)
    per_kv = kv // tile

    def step(j, masked):
        rows = pl.ds(pl.multiple_of(j * kv, kv), kv)
        posb = posb_ref[rows, :]
        scores = []
        for hh in range(heads):
            kx = jnp.concatenate([k_ref[rows, hh * DV_A:(hh + 1) * DV_A], posb], axis=1)
            scores += [_dot(kx, rhs[2 * hh]), _dot(kx, rhs[2 * hh + 1])]
        for st, s in enumerate(scores):
            hs = slice(st // 2 * DV_A, (st // 2 + 1) * DV_A)
            if masked:
                kk = lax.broadcasted_iota(jnp.int32, s.shape, 0) + (j * kv - qi * tile)
                qq = lax.broadcasted_iota(jnp.int32, s.shape, 1)
                s = jnp.where(kk <= qq, s, -jnp.inf)
            m_old = m_ref[st]
            m_new = jnp.maximum(m_old, jnp.max(s, axis=0, keepdims=True))
            p = jnp.exp2(s - m_new).astype(BF16)
            vt = jnp.concatenate([vt_ref[0, j * per_kv + c, hs, :] for c in range(per_kv)], axis=1)
            vx = jnp.concatenate([vt, ones], axis=0)
            acc_ref[st] = jnp.exp2(m_old - m_new) * acc_ref[st] + _dot(vx, p)
            m_ref[st] = m_new

    def body(j, carry):
        step(j, False)
        return carry

    n_full = (qi * tile) // kv
    lax.fori_loop(0, n_full, body, 0)
    step(n_full, True)
    lam = _diff_lambda(lam_ref[...], lam_init)
    for hh in range(heads):
        acc0 = acc_ref[2 * hh]
        acc1 = acc_ref[2 * hh + 1]
        out0 = acc0[:DV_A] * (1.0 / acc0[DV_A:DV_A + 1])
        out1 = acc1[:DV_A] * (1.0 / acc1[DV_A:DV_A + 1])
        o_ref[:, hh * DV_A:(hh + 1) * DV_A] = (out0 - lam * out1).T


def diff_attn_prompt(aqt, akb, avt, posb, slopes, lam_vecs, lam_init):
    batch, nq, _, tile = aqt.shape
    seq = nq * tile
    kv = min(ATTN_KV_CHUNK, seq)
    heads = ATTN_HEADS_PER_STEP
    wide = heads * DV_A
    return pl.pallas_call(
        functools.partial(_attn_body, tile=tile, kv=kv, lam_init=lam_init),
        grid=(batch, H_A // heads, nq),
        in_specs=[pl.BlockSpec(memory_space=pltpu.SMEM),
                  pl.BlockSpec(lam_vecs.shape, lambda b, h, i: (0, 0)),
                  pl.BlockSpec((1, 1, wide, tile), lambda b, h, i: (b, i, h, 0)),
                  pl.BlockSpec((seq, wide), lambda b, h, i: (b, h)),
                  pl.BlockSpec((seq, LANES), lambda b, h, i: (0, 0)),
                  pl.BlockSpec((1, nq, wide, tile), lambda b, h, i: (b, 0, h, 0))],
        out_specs=pl.BlockSpec((tile, wide), lambda b, h, i: (b * nq + i, h)),
        out_shape=jax.ShapeDtypeStruct((batch * seq, W_A), F32),
        scratch_shapes=[pltpu.VMEM((2 * heads, 1, tile), F32), pltpu.VMEM((2 * heads, DV_A + 16, tile), F32)],
        compiler_params=_params("arbitrary", "arbitrary", "arbitrary"),
        name="diff_attn_prompt",
    )(slopes, lam_vecs, aqt, akb, posb, avt)


def _decode_body(pt_ref, q_ref, kn_ref, vn_ref, slope_ref, lam_ref, *rest, pages, past, lam_init):
    kt_refs = rest[:pages]
    v_refs = rest[pages:2 * pages]
    o_ref, m_ref, l_ref, acc_ref = rest[2 * pages:]
    j = pl.program_id(1)
    page = kt_refs[0].shape[3]
    groups = 2 * H_A

    @pl.when(j == 0)
    def _():
        m_ref[...] = jnp.full(m_ref.shape, -jnp.inf, F32)
        l_ref[...] = jnp.zeros(l_ref.shape, F32)
        acc_ref[...] = jnp.zeros(acc_ref.shape, F32)

    qrow = q_ref[0].astype(F32)
    lane = lax.broadcasted_iota(jnp.int32, (groups, W_A), 1)
    r8 = lax.broadcasted_iota(jnp.int32, (groups, W_A), 0)
    q8 = jnp.where(lane // DK_A == r8, jnp.broadcast_to(qrow, (groups, W_A)), 0.0)
    q8_bf = q8.astype(BF16)
    slope = slope_ref[...]
    col = lax.broadcasted_iota(jnp.int32, (1, page), 1)
    row_head = lax.broadcasted_iota(jnp.int32, (groups, 1), 0) // 2

    s_parts = []
    for i in range(pages):
        kpos = (col + (j * pages + i) * page).astype(F32)
        s_parts.append(_dot(q8_bf, kt_refs[i][0, 0].astype(BF16)) + slope * kpos)
    s = jnp.concatenate(s_parts, axis=1)
    m_old = m_ref[...]
    m_new = jnp.maximum(m_old, jnp.max(s, axis=1, keepdims=True))
    p = jnp.exp(s - m_new).astype(BF16)
    alpha = jnp.exp(m_old - m_new)
    pv = jnp.zeros(acc_ref.shape, F32)
    for i in range(pages):
        p_i = p[:, i * page:(i + 1) * page]
        for h in range(H_A):
            v_h = v_refs[i][0, 0, pl.ds(h, page, stride=H_A), :].astype(BF16)
            pv = pv + jnp.where(row_head == h, _dot(p_i, v_h), 0.0)
    l_ref[...] = alpha * l_ref[...] + jnp.sum(p.astype(F32), axis=1, keepdims=True)
    acc_ref[...] = alpha * acc_ref[...] + pv
    m_ref[...] = m_new

    @pl.when(j == pl.num_programs(1) - 1)
    def _():
        s_new = jnp.sum(q8 * kn_ref[0], axis=1, keepdims=True) + slope * float(past)
        v_new = jnp.zeros(acc_ref.shape, F32)
        for h in range(H_A):
            v_row = jnp.broadcast_to(vn_ref[0, :, h * DV_A:(h + 1) * DV_A], acc_ref.shape)
            v_new = jnp.where(row_head == h, v_row, v_new)
        m_old = m_ref[...]
        m_fin = jnp.maximum(m_old, s_new)
        p_new = jnp.exp(s_new - m_fin)
        alpha = jnp.exp(m_old - m_fin)
        l_fin = alpha * l_ref[...] + p_new
        o8 = (alpha * acc_ref[...] + p_new * v_new) / l_fin
        lam = _diff_lambda(lam_ref[...], lam_init)
        for h in range(H_A):
            o_ref[0, :, h * DV_A:(h + 1) * DV_A] = o8[2 * h:2 * h + 1] - lam * o8[2 * h + 1:2 * h + 2]


def diff_attn_decode(page_table, aq, ak, av, slope8, lam_vecs, lam_init, cache_kt, cache_v, layer):
    bd = aq.shape[0]
    n_pages = page_table.shape[1]
    page = cache_kt.shape[3]
    pages = math.gcd(DECODE_PAGES, n_pages)
    past = n_pages * page

    def page_spec(i, shape):
        return pl.BlockSpec((1, 1) + shape, lambda b, j, pt: (layer, pt[b * n_pages + j * pages + i], 0, 0))

    row3 = lambda: pl.BlockSpec((1, 1, W_A), lambda b, j, pt: (b, 0, 0))
    const = lambda a: pl.BlockSpec(a.shape, lambda b, j, pt: (0,) * a.ndim)
    grid_spec = pltpu.PrefetchScalarGridSpec(
        num_scalar_prefetch=1,
        grid=(bd, n_pages // pages),
        in_specs=[row3(), row3(), row3(), const(slope8), const(lam_vecs)]
        + [page_spec(i, (W_A, page)) for i in range(pages)]
        + [page_spec(i, (page * H_A, DV_A)) for i in range(pages)],
        out_specs=row3(),
        scratch_shapes=[pltpu.VMEM((2 * H_A, 1), F32), pltpu.VMEM((2 * H_A, 1), F32),
                        pltpu.VMEM((2 * H_A, DV_A), F32)],
    )
    out = pl.pallas_call(
        functools.partial(_decode_body, pages=pages, past=past, lam_init=lam_init),
        grid_spec=grid_spec,
        out_shape=jax.ShapeDtypeStruct((bd, 1, W_A), F32),
        compiler_params=_params("arbitrary", "arbitrary"),
        name="diff_attn_decode",
    )(page_table.reshape(-1), aq.reshape(bd, 1, W_A), ak.reshape(bd, 1, W_A), av.reshape(bd, 1, W_A),
      slope8, lam_vecs, *([cache_kt] * pages), *([cache_v] * pages))
    return out.reshape(bd, W_A)


def _gla_body(q_ref, k_ref, g_ref, v_ref, s0_ref, o_ref, sfin_ref, st_ref, *, chunk):
    n = pl.program_id(0)
    nb = q_ref.shape[0]
    pair = 2 * DK_B

    @pl.when(n == 0)
    def _():
        for b in range(nb):
            for p in range(H_B // 2):
                st_ref[b, p] = s0_ref[b, p * pair:(p + 1) * pair, :].T

    r = lax.broadcasted_iota(jnp.int32, (chunk, chunk), 0)
    c = lax.broadcasted_iota(jnp.int32, (chunk, chunk), 1)
    causal = r >= c
    tri = jnp.where(causal, 1.0, 0.0)
    lane_head = lax.broadcasted_iota(jnp.int32, (1, pair), 1) // DK_B
    mid = chunk // 2
    for b in range(nb):
        cum = jnp.dot(tri, g_ref[b], precision=HIGHEST, preferred_element_type=F32)
        c_mid = cum[mid - 1:mid, :]
        c_last = cum[chunk - 1:chunk, :]
        q = q_ref[b]
        k = k_ref[b]
        q_intra = (q * jnp.exp(cum - c_mid)).astype(BF16)
        k_intra = (k * jnp.exp(c_mid - cum)).astype(BF16)
        q_inter = (q * jnp.exp(cum)).astype(BF16)
        k_end = (k * jnp.exp(c_last - cum)).astype(BF16)
        decay = jnp.exp(c_last)
        zero = jnp.zeros((), BF16)
        for p in range(H_B // 2):
            sl = slice(p * pair, (p + 1) * pair)
            st = st_ref[b, p]
            st_bf = st.astype(BF16)
            st_new = st * decay[:, sl]
            for hh in range(2):
                h = 2 * p + hh
                own = lane_head == hh
                vsl = slice(h * DV_B, (h + 1) * DV_B)
                v = v_ref[b, :, vsl]
                a = _dot_nt(jnp.where(own, q_intra[:, sl], zero), k_intra[:, sl])
                a = jnp.where(causal, a, 0.0).astype(BF16)
                o = _dot(a, v.astype(BF16))
                o = o + _dot_nt(jnp.where(own, q_inter[:, sl], zero), st_bf)
                o_ref[b, :, vsl] = o
                st_new = st_new + _dot(v.T.astype(BF16), jnp.where(own, k_end[:, sl], zero))
            st_ref[b, p] = st_new

    @pl.when(n == pl.num_programs(0) - 1)
    def _():
        for b in range(nb):
            for p in range(H_B // 2):
                sfin_ref[b, p * pair:(p + 1) * pair, :] = st_ref[b, p].T


def gla_prompt(bq, bk, glog, bv, s0, batch):
    m = bq.shape[0]
    seq = m // batch
    chunk = min(GLA_CHUNK, seq)
    blk = lambda w: pl.BlockSpec((batch, chunk, w), lambda n: (0, n, 0))
    state = pl.BlockSpec((batch, W_BK, DV_B), lambda n: (0, 0, 0))
    o, s_fin = pl.pallas_call(
        functools.partial(_gla_body, chunk=chunk),
        grid=(seq // chunk,),
        in_specs=[blk(W_BK), blk(W_BK), blk(W_BK), blk(W_BV), state],
        out_specs=[blk(W_BV), state],
        out_shape=[jax.ShapeDtypeStruct((batch, seq, W_BV), F32),
                   jax.ShapeDtypeStruct((batch, W_BK, DV_B), F32)],
        scratch_shapes=[pltpu.VMEM((batch, H_B // 2, DV_B, 2 * DK_B), F32)],
        compiler_params=_params("arbitrary"),
        name="gla_prompt",
    )(bq.reshape(batch, seq, W_BK), bk.reshape(batch, seq, W_BK), glog.reshape(batch, seq, W_BK),
      bv.reshape(batch, seq, W_BV), s0)
    return o.reshape(m, W_BV), s_fin


def _column(row):
    n = row.shape[1]
    r = lax.broadcasted_iota(jnp.int32, (n, n), 0)
    c = lax.broadcasted_iota(jnp.int32, (n, n), 1)
    return jnp.sum(jnp.where(r == c, jnp.broadcast_to(row, (n, n)), 0.0), axis=1, keepdims=True)


def _gla_step_body(q_ref, k_ref, g_ref, v_ref, s0_ref, o_ref, s_ref):
    pair = 2 * DK_B
    for i in range(q_ref.shape[0]):
        for p in range(H_B // 2):
            sl = slice(p * pair, (p + 1) * pair)
            qc = _column(q_ref[i, :, sl])
            kc = _column(k_ref[i, :, sl])
            dc = jnp.exp(_column(g_ref[i, :, sl]))
            for hh in range(2):
                h = 2 * p + hh
                rs = slice(hh * DK_B, (hh + 1) * DK_B)
                v = v_ref[i, :, h * DV_B:(h + 1) * DV_B]
                s_new = dc[rs] * s0_ref[i, h] + kc[rs] * v
                s_ref[i, h] = s_new
                o_ref[i, :, h * DV_B:(h + 1) * DV_B] = jnp.sum(qc[rs] * s_new, axis=0, keepdims=True)


def gla_step(bq, bk, glog, bv, s0):
    bd = bq.shape[0]
    per_step = math.gcd(GLA_STEP_SEQS, bd)
    row = lambda w: pl.BlockSpec((per_step, 1, w), lambda b: (b, 0, 0))
    state = pl.BlockSpec((per_step, H_B, DK_B, DV_B), lambda b: (b, 0, 0, 0))
    o, s_fin = pl.pallas_call(
        _gla_step_body,
        grid=(bd // per_step,),
        in_specs=[row(W_BK), row(W_BK), row(W_BK), row(W_BV), state],
        out_specs=[row(W_BV), state],
        out_shape=[jax.ShapeDtypeStruct((bd, 1, W_BV), F32), jax.ShapeDtypeStruct(s0.shape, F32)],
        compiler_params=_params("arbitrary"),
        name="gla_step",
    )(bq.reshape(bd, 1, W_BK), bk.reshape(bd, 1, W_BK), glog.reshape(bd, 1, W_BK),
      bv.reshape(bd, 1, W_BV), s0)
    return o.reshape(bd, W_BV), s_fin


def _head_rms(t, g, width):
    parts = []
    for lo in range(0, t.shape[1], width):
        parts.append(_rms(t[:, lo:lo + width], g[:, lo:lo + width]))
    return jnp.concatenate(parts, axis=1)


def _even_out_ffn_body(x_ref, a_ref, b_ref, br_ref, ga_ref, gb_ref, w_ref, g_ref, wg_ref, wu_ref, wd_ref,
                       o_ref, h_ref, *, lam_init):
    a = _head_rms(a_ref[...], ga_ref[...], DV_A) * (1.0 - lam_init)
    br = br_ref[...]
    b = _head_rms(b_ref[...], gb_ref[...], DV_B) * (br * jax.nn.sigmoid(br))
    merged = jnp.concatenate([a, b], axis=1).astype(BF16)
    y = x_ref[...] + _dot(merged, w_ref[...])
    o_ref[...] = _swiglu_half(y, g_ref, wg_ref, wu_ref, wd_ref, h_ref)


def even_output_ffn(x, a_out, b_out, br, ga, gb, w, lam_init, g, wg, wu, wd, layer):
    return _ffn_call(functools.partial(_even_out_ffn_body, lam_init=lam_init), "even_output_ffn",
                     [x, a_out, b_out, br], [ga, gb, w], g, wg, wu, wd, layer, 1)


O_Q, O_K, O_V, O_O, O_G = 0, W_C, 2 * W_C, 3 * W_C, 4 * W_C
O_COLS = O_G + LANES


def _odd_gates(xn, w_ref, bias_ref):
    z = _dot(xn, w_ref[:, O_G:O_G + LANES]) + bias_ref[...]
    lane = lax.broadcasted_iota(jnp.int32, z.shape, 1)
    return jnp.where(lane < H_C, z, _log_sigmoid(z))


def _odd_in_step_body(x_ref, g_ref, w_ref, bias_ref, q_ref, k_ref, v_ref, o_ref, gt_ref):
    xn = _rms(x_ref[...], g_ref[...]).astype(BF16)
    proj = lambda lo, n: _dot(xn, w_ref[:, lo:lo + n])
    q_ref[...] = proj(O_Q, W_C).astype(BF16)
    k_ref[...] = (proj(O_K, W_C) * DH_C ** -0.5).astype(BF16)
    v_ref[...] = proj(O_V, W_C)
    o_ref[...] = proj(O_O, W_C)
    gt_ref[...] = _odd_gates(xn, w_ref, bias_ref)


def _odd_in_chunk_body(x_ref, g_ref, w_ref, wqt_ref, wvt_ref, bias_ref,
                       qt_ref, k_ref, vt_ref, o_ref, gt_ref, gtt_ref, *, chunk):
    xn = _rms(x_ref[...], g_ref[...]).astype(BF16)
    proj = lambda lo, n: _dot(xn, w_ref[:, lo:lo + n])
    qt = _dot_nt(wqt_ref[...], xn).astype(BF16)
    vt = _dot_nt(wvt_ref[...], xn).astype(BF16)
    for c in range(qt_ref.shape[1]):
        qt_ref[0, c] = qt[:, c * chunk:(c + 1) * chunk]
        vt_ref[0, c] = vt[:, c * chunk:(c + 1) * chunk]
    k_ref[...] = (proj(O_K, W_C) * DH_C ** -0.5).astype(BF16)
    o_ref[...] = proj(O_O, W_C)
    gates = _odd_gates(xn, w_ref, bias_ref)
    gt_ref[...] = gates
    gtt_ref[...] = gates.T[:2 * H_C, :]


def odd_inputs(x, g, w, bias, prompt_batch=None):
    m, d = x.shape
    tm = _row_tile(m)
    if prompt_batch is None:
        widths = [(W_C, BF16), (W_C, BF16), (W_C, F32), (W_C, F32), (LANES, F32)]
        return pl.pallas_call(
            _odd_in_step_body,
            grid=(m // tm,),
            in_specs=[_rows(tm, d), _resident((1, d)), _resident(w.shape), _resident(bias.shape)],
            out_specs=[_rows(tm, n) for n, _ in widths],
            out_shape=[jax.ShapeDtypeStruct((m, n), dt) for n, dt in widths],
            compiler_params=_params("arbitrary"),
            name="odd_inputs",
        )(x, g, w, bias)
    seq = m // prompt_batch
    chunk = min(MLSTM_CHUNK, seq)
    per_seq = seq // tm
    chunked = pl.BlockSpec((1, tm // chunk, W_C, chunk), lambda i: (i // per_seq, i % per_seq, 0, 0))
    chunked_shape = jax.ShapeDtypeStruct((prompt_batch, seq // chunk, W_C, chunk), BF16)
    wqt = w[:, O_Q:O_Q + W_C].T
    wvt = w[:, O_V:O_V + W_C].T
    return pl.pallas_call(
        functools.partial(_odd_in_chunk_body, chunk=chunk),
        grid=(m // tm,),
        in_specs=[_rows(tm, d), _resident((1, d)), _resident(w.shape), _resident(wqt.shape),
                  _resident(wvt.shape), _resident(bias.shape)],
        out_specs=[chunked, _rows(tm, W_C), chunked, _rows(tm, W_C), _rows(tm, LANES),
                   pl.BlockSpec((2 * H_C, tm), lambda i: (0, i))],
        out_shape=[chunked_shape, jax.ShapeDtypeStruct((m, W_C), BF16), chunked_shape,
                   jax.ShapeDtypeStruct((m, W_C), F32), jax.ShapeDtypeStruct((m, LANES), F32),
                   jax.ShapeDtypeStruct((2 * H_C, m), F32)],
        compiler_params=_params("arbitrary"),
        name="odd_inputs",
    )(x, g, w, wqt, wvt, bias)


def _mlstm_body(qt_ref, k_ref, vt_ref, gt_ref, gtt_ref, c0_ref, n0_ref, m0_ref,
                h_ref, cfin_ref, nfin_ref, mfin_ref, c_sc, n_sc, m_sc, *, chunk):
    n = pl.program_id(1)

    @pl.when(n == 0)
    def _():
        c_sc[...] = c0_ref[0]
        n_sc[...] = n0_ref[0]
        m_sc[...] = m0_ref[0]

    r = lax.broadcasted_iota(jnp.int32, (chunk, chunk), 0)
    c = lax.broadcasted_iota(jnp.int32, (chunk, chunk), 1)
    past = r <= c
    gates = gt_ref[...]
    gates_t = gtt_ref[...]
    cum_col = jnp.dot(jnp.where(r >= c, 1.0, 0.0), gates, precision=HIGHEST, preferred_element_type=F32)
    cum_row = jnp.dot(gates_t, jnp.where(past, 1.0, 0.0), precision=HIGHEST, preferred_element_type=F32)
    ones = jnp.ones((16, chunk), BF16)
    for h in range(H_C):
        sl = slice(h * DH_C, (h + 1) * DH_C)
        i_row = gates_t[h:h + 1, :]
        b_row = cum_row[H_C + h:H_C + h + 1, :]
        b_last = b_row[:, chunk - 1:chunk]
        m_prev = m_sc[h:h + 1, 0:1]
        src_col = gates[:, h:h + 1] - cum_col[:, H_C + h:H_C + h + 1]
        dlog = jnp.where(past, src_col + b_row, -jnp.inf)
        prev_log = b_row + m_prev
        m_t = jnp.maximum(prev_log, jnp.max(dlog, axis=0, keepdims=True))
        dw = jnp.exp(dlog - m_t)
        pw = jnp.exp(prev_log - m_t)
        qt = qt_ref[0, 0, sl, :]
        vt = vt_ref[0, 0, sl, :]
        k = k_ref[:, sl]
        sw = (_dot(k, qt) * dw).astype(BF16)
        c_old = c_sc[h]
        n_old = n_sc[h:h + 1, :]
        intra = _dot(jnp.concatenate([vt, ones], axis=0), sw)
        n8 = jnp.broadcast_to(n_old, (8, DH_C)).astype(BF16)
        num = intra[:DH_C] + pw * _dot(c_old.astype(BF16), qt)
        den = intra[DH_C:DH_C + 1] + pw * _dot(n8, qt)[0:1]
        h_t = num * (1.0 / jnp.maximum(jnp.abs(den), jnp.exp(-m_t)))
        h_ref[:, sl] = h_t.T
        m_end = m_t[:, chunk - 1:chunk]
        wk = jnp.exp(b_last - b_row + i_row - m_end)
        a = jnp.exp(b_last + m_prev - m_end)
        c_sc[h] = a * c_old + _dot((vt.astype(F32) * wk).astype(BF16), k)
        wk8 = jnp.broadcast_to(wk, (8, chunk)).astype(BF16)
        n_sc[h:h + 1, :] = a * n_old + _dot(wk8, k)[0:1]
        m_sc[h:h + 1, :] = jnp.broadcast_to(m_end, (1, LANES))

    @pl.when(n == pl.num_programs(1) - 1)
    def _():
        cfin_ref[0] = c_sc[...]
        nfin_ref[0] = n_sc[...]
        mfin_ref[0] = m_sc[...]


def mlstm_prompt(qt, k, vt, gates, gates_t, c0, n0, m0):
    batch, nc, _, chunk = qt.shape
    m = k.shape[0]
    rows = lambda w: pl.BlockSpec((chunk, w), lambda b, n: (b * nc + n, 0))
    chunked = pl.BlockSpec((1, 1, W_C, chunk), lambda b, n: (b, n, 0, 0))
    st_c = pl.BlockSpec((1, H_C, DH_C, DH_C), lambda b, n: (b, 0, 0, 0))
    st_n = pl.BlockSpec((1, H_C, DH_C), lambda b, n: (b, 0, 0))
    st_m = pl.BlockSpec((1, H_C, LANES), lambda b, n: (b, 0, 0))
    return pl.pallas_call(
        functools.partial(_mlstm_body, chunk=chunk),
        grid=(batch, nc),
        in_specs=[chunked, rows(W_C), chunked, rows(LANES),
                  pl.BlockSpec((2 * H_C, chunk), lambda b, n: (0, b * nc + n)), st_c, st_n, st_m],
        out_specs=[rows(W_C), st_c, st_n, st_m],
        out_shape=[jax.ShapeDtypeStruct((m, W_C), F32), jax.ShapeDtypeStruct(c0.shape, F32),
                   jax.ShapeDtypeStruct(n0.shape, F32), jax.ShapeDtypeStruct(m0.shape, F32)],
        scratch_shapes=[pltpu.VMEM((H_C, DH_C, DH_C), F32), pltpu.VMEM((H_C, DH_C), F32),
                        pltpu.VMEM((H_C, LANES), F32)],
        compiler_params=_params("arbitrary", "arbitrary"),
        name="mlstm_prompt",
    )(qt, k, vt, gates, gates_t, c0, n0, m0)


def _mlstm_step_body(q_ref, k_ref, v_ref, gt_ref, c0_ref, n0_ref, m0_ref, h_ref, c_ref, n_ref, m_ref):
    for i in range(q_ref.shape[0]):
        gates = gt_ref[i]
        for h in range(H_C):
            sl = slice(h * DH_C, (h + 1) * DH_C)
            i_pre = gates[:, h:h + 1]
            logf = gates[:, H_C + h:H_C + h + 1]
            m_prev = m0_ref[i, h:h + 1, 0:1]
            prev_log = logf + m_prev
            m_t = jnp.maximum(prev_log, i_pre)
            dw = jnp.exp(i_pre - m_t)
            pw = jnp.exp(prev_log - m_t)
            q = q_ref[i, :, sl]
            k = k_ref[i, :, sl]
            v = v_ref[i, :, sl]
            qf = q.astype(F32)
            kf = k.astype(F32)
            c_old = c0_ref[i, h]
            n_old = n0_ref[i, h:h + 1, :]
            sw = jnp.sum(qf * kf, axis=1, keepdims=True) * dw
            cq = _dot_nt(jnp.broadcast_to(q, (8, DH_C)), c_old.astype(BF16))[0:1]
            num = sw * v + pw * cq
            den = sw + pw * jnp.sum(qf * n_old, axis=1, keepdims=True)
            h_ref[i, :, sl] = num / jnp.maximum(jnp.abs(den), jnp.exp(-m_t))
            c_ref[i, h] = pw * c_old + _column(dw * v) * kf
            n_ref[i, h:h + 1, :] = pw * n_old + dw * kf
            m_ref[i, h:h + 1, :] = jnp.broadcast_to(m_t, (1, LANES))


def mlstm_step(q, k, v, gates, c0, n0, m0):
    bd = q.shape[0]
    per_step = math.gcd(MLSTM_STEP_SEQS, bd)
    row = lambda w: pl.BlockSpec((per_step, 1, w), lambda b: (b, 0, 0))
    st_c = pl.BlockSpec((per_step, H_C, DH_C, DH_C), lambda b: (b, 0, 0, 0))
    st_n = pl.BlockSpec((per_step, H_C, DH_C), lambda b: (b, 0, 0))
    st_m = pl.BlockSpec((per_step, H_C, LANES), lambda b: (b, 0, 0))
    h, c_fin, n_fin, m_fin = pl.pallas_call(
        _mlstm_step_body,
        grid=(bd // per_step,),
        in_specs=[row(W_C), row(W_C), row(W_C), row(LANES), st_c, st_n, st_m],
        out_specs=[row(W_C), st_c, st_n, st_m],
        out_shape=[jax.ShapeDtypeStruct((bd, 1, W_C), F32), jax.ShapeDtypeStruct(c0.shape, F32),
                   jax.ShapeDtypeStruct(n0.shape, F32), jax.ShapeDtypeStruct(m0.shape, F32)],
        compiler_params=_params("arbitrary"),
        name="mlstm_step",
    )(q.reshape(bd, 1, W_C), k.reshape(bd, 1, W_C), v.reshape(bd, 1, W_C), gates.reshape(bd, 1, LANES),
      c0, n0, m0)
    return h.reshape(bd, W_C), c_fin, n_fin, m_fin


def _odd_out_ffn_body(x_ref, hm_ref, og_ref, gn_ref, w_ref, g_ref, wg_ref, wu_ref, wd_ref, o_ref, h_ref):
    gated = jax.nn.sigmoid(og_ref[...]) * _head_rms(hm_ref[...], gn_ref[...], DH_C)
    y = x_ref[...] + _dot(gated.astype(BF16), w_ref[...])
    o_ref[...] = _swiglu_half(y, g_ref, wg_ref, wu_ref, wd_ref, h_ref)


def odd_output_ffn(x, hm, og, gn, w, g, wg, wu, wd, layer):
    return _ffn_call(_odd_out_ffn_body, "odd_output_ffn", [x, hm, og], [gn, w], g, wg, wu, wd, layer, 1)


def _even_weights(w_in, w_out, qk_norm, gate_w2, gate_b, a_norm, b_norm):
    main = 3 * W_A + 2 * W_BK + W_BV
    w = jnp.concatenate([w_in[:, :main], w_in[:, main + GLA_RANK:], w_in[:, main:main + GLA_RANK],
                         jnp.zeros((w_in.shape[0], LANES - GLA_RANK), w_in.dtype)], axis=1).astype(BF16)
    w2 = jnp.concatenate([gate_w2, jnp.zeros((LANES - GLA_RANK, W_BK), gate_w2.dtype)], axis=0).astype(BF16)
    idx = jnp.arange(MXU_WIDTH) // DK_A
    seg = (idx[:, None] == idx[None, :]).astype(BF16)
    gq = jnp.tile(qk_norm[0], W_A // DK_A).reshape(1, W_A)
    gk = jnp.tile(qk_norm[1], W_A // DK_A).reshape(1, W_A)
    return dict(w=w, w2=w2, seg=seg, gq=gq, gk=gk, gb=gate_b.reshape(1, W_BK), w_out=w_out.astype(BF16),
                ga=a_norm.reshape(1, W_A), gbn=b_norm.reshape(1, W_BV))


def _odd_weights(w_in, w_out, gate_b, norm_g):
    w = jnp.concatenate([w_in, jnp.zeros((w_in.shape[0], LANES - 2 * H_C), w_in.dtype)], axis=1).astype(BF16)
    bias = jnp.concatenate([gate_b[0], gate_b[1], jnp.zeros((LANES - 2 * H_C,), gate_b.dtype)]).reshape(1, LANES)
    return dict(w=w, bias=bias, w_out=w_out.astype(BF16), g=norm_g.reshape(1, W_C))


def kernel(x_prompt, x_sample, cache_k, cache_v, state_gla, state_mlstm_C, state_mlstm_n, state_mlstm_m,
           page_table, norm_g, ffn_w_gate, ffn_w_up, ffn_w_down, even_w_in, even_w_out, a_qk_norm, a_lambda,
           a_head_norm, b_gate_w2, b_gate_bias, b_head_norm, odd_w_in, odd_w_out, c_gate_bias, c_head_norm):
    bp, seq, d = x_prompt.shape
    bd = x_sample.shape[0]
    depth = norm_g.shape[0]
    assert x_sample.shape[1] == 1
    yp = x_prompt.reshape(bp * seq, d)
    ys = x_sample.reshape(bd, d)
    n_even, n_pool, page = cache_k.shape[:3]
    ck = jnp.transpose(cache_k, (0, 1, 3, 4, 5, 2)).reshape(n_even, n_pool, W_A, page)
    cv = cache_v.reshape(n_even, n_pool, page * H_A, DV_A)
    slopes = jnp.asarray([2.0 ** (-8.0 * (h + 1) / H_A) for h in range(H_A)], F32)
    slope8 = jnp.repeat(slopes, 2).reshape(2 * H_A, 1)
    wg, wu, wd = ffn_w_gate.astype(BF16), ffn_w_up.astype(BF16), ffn_w_down.astype(BF16)
    pos = jnp.arange(seq)[:, None]
    lane = jnp.arange(LANES)[None, :]
    posb = jnp.where(lane >= 6, 0, jnp.where(lane % 2 == 0, pos // 64 * 64, pos % 64)).astype(BF16)

    k_p, v_p, k_s, v_s, gla_p, gla_s = [], [], [], [], [], []
    cm_p, nm_p, mm_p, cm_s, nm_s, mm_s = [], [], [], [], [], []
    for li in range(depth):
        g = norm_g[li].reshape(3, 1, d)
        yp = ffn_half(yp, g[0], wg, wu, wd, li, 0)
        ys = ffn_half(ys, g[0], wg, wu, wd, li, 0)
        if li % 2 == 0:
            e = li // 2
            lam_init = 0.8 - 0.6 * math.exp(-0.3 * li)
            ew = _even_weights(even_w_in[e], even_w_out[e], a_qk_norm[e], b_gate_w2[e], b_gate_bias[e],
                               a_head_norm[e], b_head_norm[e])
            lam_vecs = a_lambda[e]
            aqt, akb, akt, av, avt, bq, bk, bv, gl, br = even_inputs(
                yp, g[1], ew["w"], ew["seg"], ew["gq"], ew["gk"], ew["w2"], ew["gb"], prompt_batch=bp)
            a_out = diff_attn_prompt(aqt, akb, avt, posb, slopes, lam_vecs, lam_init)
            b_out, s_fin = gla_prompt(bq, bk, gl, bv, jnp.zeros((bp, W_BK, DV_B), F32), bp)
            yp = even_output_ffn(yp, a_out, b_out, br, ew["ga"], ew["gbn"], ew["w_out"], lam_init,
                                 g[2], wg, wu, wd, li)
            k_p.append(jnp.transpose(akt.reshape(bp, H_A, 2, DK_A, seq), (0, 4, 1, 2, 3)))
            v_p.append(av.reshape(bp, seq, H_A, DV_A))
            gla_p.append(s_fin.reshape(bp, H_B, DK_B, DV_B))
            aq, ak, av, bq, bk, bv, gl, br = even_inputs(
                ys, g[1], ew["w"], ew["seg"], ew["gq"], ew["gk"], ew["w2"], ew["gb"])
            a_out = diff_attn_decode(page_table, aq, ak, av, slope8, lam_vecs, lam_init, ck, cv, e)
            b_out, s_fin = gla_step(bq, bk, gl, bv, state_gla[e])
            ys = even_output_ffn(ys, a_out, b_out, br, ew["ga"], ew["gbn"], ew["w_out"], lam_init,
                                 g[2], wg, wu, wd, li)
            k_s.append(ak.reshape(bd, 1, H_A, 2, DK_A))
            v_s.append(av.reshape(bd, 1, H_A, DV_A))
            gla_s.append(s_fin)
        else:
            o = li // 2
            ow = _odd_weights(odd_w_in[o], odd_w_out[o], c_gate_bias[o], c_head_norm[o])
            qt, k, vt, og, gates, gates_t = odd_inputs(yp, g[1], ow["w"], ow["bias"], prompt_batch=bp)
            h, c_fin, n_fin, m_fin = mlstm_prompt(
                qt, k, vt, gates, gates_t, jnp.zeros((bp, H_C, DH_C, DH_C), F32),
                jnp.zeros((bp, H_C, DH_C), F32), jnp.zeros((bp, H_C, LANES), F32))
            yp = odd_output_ffn(yp, h, og, ow["g"], ow["w_out"], g[2], wg, wu, wd, li)
            cm_p.append(c_fin)
            nm_p.append(n_fin)
            mm_p.append(m_fin[:, :, 0])
            q, k, v, og, gates = odd_inputs(ys, g[1], ow["w"], ow["bias"])
            m0 = jnp.broadcast_to(state_mlstm_m[o][:, :, None], (bd, H_C, LANES))
            h, c_fin, n_fin, m_fin = mlstm_step(q, k, v, gates, state_mlstm_C[o], state_mlstm_n[o], m0)
            ys = odd_output_ffn(ys, h, og, ow["g"], ow["w_out"], g[2], wg, wu, wd, li)
            cm_s.append(c_fin)
            nm_s.append(n_fin)
            mm_s.append(m_fin[:, :, 0])
    return (yp.reshape(bp, seq, d), ys.reshape(bd, 1, d),
            jnp.stack(k_p), jnp.stack(v_p), jnp.stack(k_s), jnp.stack(v_s),
            jnp.stack(gla_p), jnp.stack(gla_s),
            jnp.stack(cm_p), jnp.stack(nm_p), jnp.stack(mm_p),
            jnp.stack(cm_s), jnp.stack(nm_s), jnp.stack(mm_s))
```

```python
import functools
import math

import jax
import jax.numpy as jnp
from jax import lax
from jax.experimental import pallas as pl
from jax.experimental.pallas import tpu as pltpu

F32 = jnp.float32
BF16 = jnp.bfloat16
EPS = 1e-6
HIGHEST = lax.Precision.HIGHEST
LOG2E = math.log2(math.e)

LANES = 128
MXU_WIDTH = 256
VMEM_LIMIT_BYTES = 56 * 2**20

H_A, DK_A, DV_A = 4, 64, 128
H_B, DK_B, DV_B = 4, 64, 128
GLA_RANK = 16
GLA_NORMALIZER = 16.0
H_C, DH_C = 4, 256
W_A = H_A * 2 * DK_A
W_BK = H_B * DK_B
W_BV = H_B * DV_B
W_C = H_C * DH_C

ROW_TILE = 512
ATTN_TILE = 256
ATTN_KV_CHUNK = 1024
ATTN_HEADS_PER_STEP = 4
GLA_CHUNK = 128
MLSTM_CHUNK = 256
DECODE_PAGES = 16
GLA_STEP_SEQS = 8
MLSTM_STEP_SEQS = 2

NT_DIMS = (((1,), (1,)), ((), ()))


def _params(*semantics):
    return pltpu.CompilerParams(dimension_semantics=semantics, vmem_limit_bytes=VMEM_LIMIT_BYTES)


def _resident(shape):
    return pl.BlockSpec(shape, lambda *_: (0,) * len(shape), pipeline_mode=pl.Buffered(1))


def _rows(tm, width):
    return pl.BlockSpec((tm, width), lambda i: (i, 0))


def _row_tile(m):
    return ROW_TILE if m % ROW_TILE == 0 else m


def _rms(x, g):
    return x * lax.rsqrt(jnp.mean(x * x, axis=-1, keepdims=True) + EPS) * g


def _log_sigmoid(x):
    return jnp.minimum(x, 0.0) - jnp.log1p(jnp.exp(-jnp.abs(x)))


def _dot(a, b):
    return jnp.dot(a, b, preferred_element_type=F32)


def _dot_nt(a, b):
    return lax.dot_general(a, b, NT_DIMS, preferred_element_type=F32)


def _diff_lambda(lam_vecs, lam_init):
    p1 = jnp.sum(lam_vecs[0:1] * lam_vecs[1:2], axis=-1, keepdims=True)
    p2 = jnp.sum(lam_vecs[2:3] * lam_vecs[3:4], axis=-1, keepdims=True)
    return jnp.exp(p1) - jnp.exp(p2) + lam_init


FF_CHUNK = 256


def _swiglu_half(x, g_ref, wg_ref, wu_ref, wd_ref, h_ref):
    xn = _rms(x, g_ref[...]).astype(BF16)
    for c in range(0, wg_ref.shape[1], FF_CHUNK):
        hg = _dot(xn, wg_ref[:, c:c + FF_CHUNK])
        hu = _dot(xn, wu_ref[:, c:c + FF_CHUNK])
        h_ref[:, c:c + FF_CHUNK] = (hg * jax.nn.sigmoid(hg) * hu).astype(BF16)
    return x + 0.5 * _dot(h_ref[...], wd_ref[...])


def _ffn_body(x_ref, g_ref, wg_ref, wu_ref, wd_ref, o_ref, h_ref):
    o_ref[...] = _swiglu_half(x_ref[...], g_ref, wg_ref, wu_ref, wd_ref, h_ref)


def _ffn_call(body, name, row_inputs, small_inputs, g, wg, wu, wd, layer, half):
    m, d = row_inputs[0].shape
    dff = wg.shape[-1]
    tm = _row_tile(m)
    weight = lambda r, c: pl.BlockSpec((None, None, r, c), lambda i: (layer, half, 0, 0),
                                       pipeline_mode=pl.Buffered(1))
    return pl.pallas_call(
        body,
        grid=(m // tm,),
        in_specs=([_rows(tm, a.shape[1]) for a in row_inputs] + [_resident(a.shape) for a in small_inputs]
                  + [_resident((1, d)), weight(d, dff), weight(d, dff), weight(dff, d)]),
        out_specs=_rows(tm, d),
        out_shape=jax.ShapeDtypeStruct((m, d), F32),
        scratch_shapes=[pltpu.VMEM((tm, dff), BF16)],
        compiler_params=_params("arbitrary"),
        name=name,
    )(*row_inputs, *small_inputs, g, wg, wu, wd)


def ffn_half(x, g, wg, wu, wd, layer, half):
    return _ffn_call(_ffn_body, "ffn_half", [x], [], g, wg, wu, wd, layer, half)


E_AQ, E_AK, E_AV = 0, W_A, 2 * W_A
E_BQ = 3 * W_A
E_BK = E_BQ + W_BK
E_BV = E_BK + W_BK
E_BR = E_BV + W_BV
E_BG = E_BR + W_BV
E_COLS = E_BG + LANES


def _even_in_body(x_ref, g_ref, w_ref, seg_ref, gq_ref, gk_ref, w2_ref, gb_ref, *out_refs, attn_tile):
    xn = _rms(x_ref[...], g_ref[...]).astype(BF16)
    proj = lambda lo, n: _dot(xn, w_ref[:, lo:lo + n])

    def group_rms(t, g):
        t2 = (t * t).astype(BF16)
        ms = jnp.concatenate([_dot(t2[:, c:c + MXU_WIDTH], seg_ref[...])
                              for c in range(0, t.shape[1], MXU_WIDTH)], axis=1)
        return t * lax.rsqrt(ms * (1.0 / DK_A) + EPS) * g

    aq = group_rms(proj(E_AQ, W_A), gq_ref[...]) * DK_A ** -0.5
    ak = group_rms(proj(E_AK, W_A), gk_ref[...])
    av = proj(E_AV, W_A)
    if attn_tile is None:
        aq_ref, ak_ref, av_ref, bq_ref, bk_ref, bv_ref, gl_ref, br_ref = out_refs
        aq_ref[...] = aq.astype(BF16)
        ak_ref[...] = ak
        av_ref[...] = av
    else:
        aqt_ref, akb_ref, akt_ref, av_ref, avt_ref, bq_ref, bk_ref, bv_ref, gl_ref, br_ref = out_refs
        aqt = (aq * LOG2E).T.astype(BF16)
        avt = av.T.astype(BF16)
        for c in range(aqt_ref.shape[1]):
            aqt_ref[0, c] = aqt[:, c * attn_tile:(c + 1) * attn_tile]
            avt_ref[0, c] = avt[:, c * attn_tile:(c + 1) * attn_tile]
        akb_ref[...] = ak.astype(BF16)
        akt_ref[0] = ak.T
        for h in range(H_A):
            av_ref[pl.ds(h, av.shape[0], stride=H_A), :] = av[:, h * DV_A:(h + 1) * DV_A]
    bq_ref[...] = proj(E_BQ, W_BK) * DK_B ** -0.5
    bk_ref[...] = proj(E_BK, W_BK)
    bv_ref[...] = proj(E_BV, W_BV)
    br_ref[...] = proj(E_BR, W_BV)
    bg = proj(E_BG, LANES).astype(BF16)
    gl_ref[...] = _log_sigmoid(_dot(bg, w2_ref[...]) + gb_ref[...]) * (1.0 / GLA_NORMALIZER)


def even_inputs(x, g, w, seg, gq, gk, w2, gb, prompt_batch=None):
    m, d = x.shape
    tm = _row_tile(m)
    common = [(W_BK, F32), (W_BK, F32), (W_BV, F32), (W_BK, F32), (W_BV, F32)]
    if prompt_batch is None:
        attn_tile = None
        widths = [(W_A, BF16), (W_A, F32), (W_A, F32)] + common
        out_specs = [_rows(tm, n) for n, _ in widths]
        out_shape = [jax.ShapeDtypeStruct((m, n), dt) for n, dt in widths]
    else:
        seq = m // prompt_batch
        attn_tile = min(ATTN_TILE, seq)
        per_seq = seq // tm
        per_tile = tm // attn_tile
        chunked = pl.BlockSpec((1, per_tile, W_A, attn_tile), lambda i: (i // per_seq, i % per_seq, 0, 0))
        chunked_shape = jax.ShapeDtypeStruct((prompt_batch, seq // attn_tile, W_A, attn_tile), BF16)
        out_specs = ([chunked, _rows(tm, W_A),
                      pl.BlockSpec((1, W_A, tm), lambda i: (i // per_seq, 0, i % per_seq)),
                      _rows(tm * H_A, DV_A), chunked] + [_rows(tm, n) for n, _ in common])
        out_shape = ([chunked_shape, jax.ShapeDtypeStruct((m, W_A), BF16),
                      jax.ShapeDtypeStruct((prompt_batch, W_A, seq), F32),
                      jax.ShapeDtypeStruct((m * H_A, DV_A), F32), chunked_shape]
                     + [jax.ShapeDtypeStruct((m, n), dt) for n, dt in common])
    return pl.pallas_call(
        functools.partial(_even_in_body, attn_tile=attn_tile),
        grid=(m // tm,),
        in_specs=[_rows(tm, d), _resident((1, d)), _resident(w.shape), _resident(seg.shape),
                  _resident(gq.shape), _resident(gk.shape), _resident(w2.shape), _resident(gb.shape)],
        out_specs=out_specs,
        out_shape=out_shape,
        compiler_params=_params("arbitrary"),
        name="even_inputs",
    )(x, g, w, seg, gq, gk, w2, gb)


def _attn_body(slope_ref, lam_ref, qt_ref, k_ref, posb_ref, vt_ref, o_ref, m_ref, acc_ref, *, tile, kv, lam_init):
    qi = pl.program_id(2)
    heads = m_ref.shape[0] // 2
    row = lax.broadcasted_iota(jnp.int32, (DV_A, tile), 0)
    erow = lax.broadcasted_iota(jnp.int32, (LANES, tile), 0)
    rhs = []
    for hh in range(heads):
        slope = slope_ref[pl.program_id(1) * heads + hh]
        qt = qt_ref[0, 0, hh * DV_A:(hh + 1) * DV_A, :]
        c_all = jnp.full((LANES, tile), slope * LOG2E, F32)
        c_1 = c_all.astype(BF16).astype(F32)
        c_2 = (c_all - c_1).astype(BF16).astype(F32)
        c_3 = (c_all - c_1 - c_2).astype(BF16).astype(F32)
        ext = jnp.where(erow < 2, c_1, jnp.where(erow < 4, c_2, jnp.where(erow < 6, c_3, 0.0))).astype(BF16)
        for mm in range(2):
            own = (row >= mm * DK_A) & (row < (mm + 1) * DK_A)
            rhs.append(jnp.concatenate([jnp.where(own, qt, jnp.zeros_like(qt)), ext], axis=0))
    m_ref[...] = jnp.full(m_ref.shape, -jnp.inf, F32)
    acc_ref[...] = jnp.zeros(acc_ref.shape, F32)
    per_kv = kv // tile

    def step(j, masked, keys):
        rows = pl.ds(pl.multiple_of(j * kv, kv), keys)
        ones = jnp.ones((16, keys), BF16)
        posb = posb_ref[rows, :]
        scores = []
        for hh in range(heads):
            kx = jnp.concatenate([k_ref[rows, hh * DV_A:(hh + 1) * DV_A], posb], axis=1)
            scores += [_dot(kx, rhs[2 * hh]), _dot(kx, rhs[2 * hh + 1])]
        for st, s in enumerate(scores):
            hs = slice(st // 2 * DV_A, (st // 2 + 1) * DV_A)
            if masked:
                kk = lax.broadcasted_iota(jnp.int32, s.shape, 0) + (j * kv - qi * tile)
                qq = lax.broadcasted_iota(jnp.int32, s.shape, 1)
                s = jnp.where(kk <= qq, s, -jnp.inf)
            m_old = m_ref[st]
            m_new = jnp.maximum(m_old, jnp.max(s, axis=0, keepdims=True))
            p = jnp.exp2(s - m_new).astype(BF16)
            vt = jnp.concatenate([vt_ref[0, j * per_kv + c, hs, :] for c in range(keys // tile)], axis=1)
            vx = jnp.concatenate([vt, ones], axis=0)
            acc_ref[st] = jnp.exp2(m_old - m_new) * acc_ref[st] + _dot(vx, p)
            m_ref[st] = m_new

    def body(j, carry):
        step(j, False, kv)
        return carry

    n_full = (qi * tile) // kv
    lax.fori_loop(0, n_full, body, 0)
    for r in range(per_kv):
        pl.when(qi % per_kv == r)(functools.partial(step, n_full, True, (r + 1) * tile))
    lam = _diff_lambda(lam_ref[...], lam_init)
    for hh in range(heads):
        acc0 = acc_ref[2 * hh]
        acc1 = acc_ref[2 * hh + 1]
        out0 = acc0[:DV_A] * (1.0 / acc0[DV_A:DV_A + 1])
        out1 = acc1[:DV_A] * (1.0 / acc1[DV_A:DV_A + 1])
        o_ref[:, hh * DV_A:(hh + 1) * DV_A] = (out0 - lam * out1).T


def diff_attn_prompt(aqt, akb, avt, posb, slopes, lam_vecs, lam_init):
    batch, nq, _, tile = aqt.shape
    seq = nq * tile
    kv = min(ATTN_KV_CHUNK, seq)
    heads = ATTN_HEADS_PER_STEP
    wide = heads * DV_A
    return pl.pallas_call(
        functools.partial(_attn_body, tile=tile, kv=kv, lam_init=lam_init),
        grid=(batch, H_A // heads, nq),
        in_specs=[pl.BlockSpec(memory_space=pltpu.SMEM),
                  pl.BlockSpec(lam_vecs.shape, lambda b, h, i: (0, 0)),
                  pl.BlockSpec((1, 1, wide, tile), lambda b, h, i: (b, i, h, 0)),
                  pl.BlockSpec((seq, wide), lambda b, h, i: (b, h)),
                  pl.BlockSpec((seq, LANES), lambda b, h, i: (0, 0)),
                  pl.BlockSpec((1, nq, wide, tile), lambda b, h, i: (b, 0, h, 0))],
        out_specs=pl.BlockSpec((tile, wide), lambda b, h, i: (b * nq + i, h)),
        out_shape=jax.ShapeDtypeStruct((batch * seq, W_A), F32),
        scratch_shapes=[pltpu.VMEM((2 * heads, 1, tile), F32), pltpu.VMEM((2 * heads, DV_A + 16, tile), F32)],
        compiler_params=_params("arbitrary", "arbitrary", "arbitrary"),
        name="diff_attn_prompt",
    )(slopes, lam_vecs, aqt, akb, posb, avt)


def _decode_body(pt_ref, q_ref, kn_ref, vn_ref, slope_ref, lam_ref, *rest, pages, past, lam_init):
    kt_refs = rest[:pages]
    v_refs = rest[pages:2 * pages]
    o_ref, m_ref, l_ref, acc_ref = rest[2 * pages:]
    j = pl.program_id(1)
    page = kt_refs[0].shape[3]
    groups = 2 * H_A

    @pl.when(j == 0)
    def _():
        m_ref[...] = jnp.full(m_ref.shape, -jnp.inf, F32)
        l_ref[...] = jnp.zeros(l_ref.shape, F32)
        acc_ref[...] = jnp.zeros(acc_ref.shape, F32)

    qrow = q_ref[0].astype(F32)
    lane = lax.broadcasted_iota(jnp.int32, (groups, W_A), 1)
    r8 = lax.broadcasted_iota(jnp.int32, (groups, W_A), 0)
    q8 = jnp.where(lane // DK_A == r8, jnp.broadcast_to(qrow, (groups, W_A)), 0.0)
    q8_bf = q8.astype(BF16)
    slope = slope_ref[...]
    col = lax.broadcasted_iota(jnp.int32, (1, page), 1)
    row_head = lax.broadcasted_iota(jnp.int32, (groups, 1), 0) // 2

    s_parts = []
    for i in range(pages):
        kpos = (col + (j * pages + i) * page).astype(F32)
        s_parts.append(_dot(q8_bf, kt_refs[i][0, 0].astype(BF16)) + slope * kpos)
    s = jnp.concatenate(s_parts, axis=1)
    m_old = m_ref[...]
    m_new = jnp.maximum(m_old, jnp.max(s, axis=1, keepdims=True))
    p = jnp.exp(s - m_new).astype(BF16)
    alpha = jnp.exp(m_old - m_new)
    pv = jnp.zeros(acc_ref.shape, F32)
    for i in range(pages):
        p_i = p[:, i * page:(i + 1) * page]
        for h in range(H_A):
            v_h = v_refs[i][0, 0, pl.ds(h, page, stride=H_A), :].astype(BF16)
            pv = pv + jnp.where(row_head == h, _dot(p_i, v_h), 0.0)
    l_ref[...] = alpha * l_ref[...] + jnp.sum(p.astype(F32), axis=1, keepdims=True)
    acc_ref[...] = alpha * acc_ref[...] + pv
    m_ref[...] = m_new

    @pl.when(j == pl.num_programs(1) - 1)
    def _():
        s_new = jnp.sum(q8 * kn_ref[0], axis=1, keepdims=True) + slope * float(past)
        v_new = jnp.zeros(acc_ref.shape, F32)
        for h in range(H_A):
            v_row = jnp.broadcast_to(vn_ref[0, :, h * DV_A:(h + 1) * DV_A], acc_ref.shape)
            v_new = jnp.where(row_head == h, v_row, v_new)
        m_old = m_ref[...]
        m_fin = jnp.maximum(m_old, s_new)
        p_new = jnp.exp(s_new - m_fin)
        alpha = jnp.exp(m_old - m_fin)
        l_fin = alpha * l_ref[...] + p_new
        o8 = (alpha * acc_ref[...] + p_new * v_new) / l_fin
        lam = _diff_lambda(lam_ref[...], lam_init)
        for h in range(H_A):
            o_ref[0, :, h * DV_A:(h + 1) * DV_A] = o8[2 * h:2 * h + 1] - lam * o8[2 * h + 1:2 * h + 2]


def diff_attn_decode(page_table, aq, ak, av, slope8, lam_vecs, lam_init, cache_kt, cache_v, layer):
    bd = aq.shape[0]
    n_pages = page_table.shape[1]
    page = cache_kt.shape[3]
    pages = math.gcd(DECODE_PAGES, n_pages)
    past = n_pages * page

    def page_spec(i, shape):
        return pl.BlockSpec((1, 1) + shape, lambda b, j, pt: (layer, pt[b * n_pages + j * pages + i], 0, 0))

    row3 = lambda: pl.BlockSpec((1, 1, W_A), lambda b, j, pt: (b, 0, 0))
    const = lambda a: pl.BlockSpec(a.shape, lambda b, j, pt: (0,) * a.ndim)
    grid_spec = pltpu.PrefetchScalarGridSpec(
        num_scalar_prefetch=1,
        grid=(bd, n_pages // pages),
        in_specs=[row3(), row3(), row3(), const(slope8), const(lam_vecs)]
        + [page_spec(i, (W_A, page)) for i in range(pages)]
        + [page_spec(i, (page * H_A, DV_A)) for i in range(pages)],
        out_specs=row3(),
        scratch_shapes=[pltpu.VMEM((2 * H_A, 1), F32), pltpu.VMEM((2 * H_A, 1), F32),
                        pltpu.VMEM((2 * H_A, DV_A), F32)],
    )
    out = pl.pallas_call(
        functools.partial(_decode_body, pages=pages, past=past, lam_init=lam_init),
        grid_spec=grid_spec,
        out_shape=jax.ShapeDtypeStruct((bd, 1, W_A), F32),
        compiler_params=_params("arbitrary", "arbitrary"),
        name="diff_attn_decode",
    )(page_table.reshape(-1), aq.reshape(bd, 1, W_A), ak.reshape(bd, 1, W_A), av.reshape(bd, 1, W_A),
      slope8, lam_vecs, *([cache_kt] * pages), *([cache_v] * pages))
    return out.reshape(bd, W_A)


def _gla_body(q_ref, k_ref, g_ref, v_ref, s0_ref, o_ref, sfin_ref, st_ref, *, chunk):
    n = pl.program_id(0)
    nb = q_ref.shape[0]
    pair = 2 * DK_B

    @pl.when(n == 0)
    def _():
        for b in range(nb):
            for p in range(H_B // 2):
                st_ref[b, p] = s0_ref[b, p * pair:(p + 1) * pair, :].T

    r = lax.broadcasted_iota(jnp.int32, (chunk, chunk), 0)
    c = lax.broadcasted_iota(jnp.int32, (chunk, chunk), 1)
    causal = r >= c
    tri = jnp.where(causal, 1.0, 0.0)
    lane_head = lax.broadcasted_iota(jnp.int32, (1, pair), 1) // DK_B
    mid = chunk // 2
    zero = jnp.zeros((), BF16)
    heads = [(b, h) for b in range(nb) for h in range(H_B)]
    lanes_of = lambda h: slice(h // 2 * pair, (h // 2 + 1) * pair)
    vals_of = lambda h: slice(h * DV_B, (h + 1) * DV_B)
    own_of = lambda h: lane_head == h % 2
    cums = [jnp.dot(tri, g_ref[b], precision=HIGHEST, preferred_element_type=F32) for b in range(nb)]
    q_intra, k_intra, q_inter, k_end, decay = [], [], [], [], []
    for b in range(nb):
        cum = cums[b]
        c_mid = cum[mid - 1:mid, :]
        c_last = cum[chunk - 1:chunk, :]
        q = q_ref[b]
        k = k_ref[b]
        q_intra.append((q * jnp.exp(cum - c_mid)).astype(BF16))
        k_intra.append((k * jnp.exp(c_mid - cum)).astype(BF16))
        q_inter.append((q * jnp.exp(cum)).astype(BF16))
        k_end.append((k * jnp.exp(c_last - cum)).astype(BF16))
        decay.append(jnp.exp(c_last))
    scores = [_dot_nt(jnp.where(own_of(h), q_intra[b][:, lanes_of(h)], zero), k_intra[b][:, lanes_of(h)])
              for b, h in heads]
    states = [[st_ref[b, p] for p in range(H_B // 2)] for b in range(nb)]
    carried = [_dot_nt(jnp.where(own_of(h), q_inter[b][:, lanes_of(h)], zero), states[b][h // 2].astype(BF16))
               for b, h in heads]
    updates = [_dot(v_ref[b, :, vals_of(h)].T.astype(BF16), jnp.where(own_of(h), k_end[b][:, lanes_of(h)], zero))
               for b, h in heads]
    for i, (b, h) in enumerate(heads):
        a = jnp.where(causal, scores[i], 0.0).astype(BF16)
        o_ref[b, :, vals_of(h)] = _dot(a, v_ref[b, :, vals_of(h)].astype(BF16)) + carried[i]
    for b in range(nb):
        for p in range(H_B // 2):
            st_ref[b, p] = (states[b][p] * decay[b][:, p * pair:(p + 1) * pair]
                            + updates[b * H_B + 2 * p] + updates[b * H_B + 2 * p + 1])

    @pl.when(n == pl.num_programs(0) - 1)
    def _():
        for b in range(nb):
            for p in range(H_B // 2):
                sfin_ref[b, p * pair:(p + 1) * pair, :] = st_ref[b, p].T


def gla_prompt(bq, bk, glog, bv, s0, batch):
    m = bq.shape[0]
    seq = m // batch
    chunk = min(GLA_CHUNK, seq)
    blk = lambda w: pl.BlockSpec((batch, chunk, w), lambda n: (0, n, 0))
    state = pl.BlockSpec((batch, W_BK, DV_B), lambda n: (0, 0, 0))
    o, s_fin = pl.pallas_call(
        functools.partial(_gla_body, chunk=chunk),
        grid=(seq // chunk,),
        in_specs=[blk(W_BK), blk(W_BK), blk(W_BK), blk(W_BV), state],
        out_specs=[blk(W_BV), state],
        out_shape=[jax.ShapeDtypeStruct((batch, seq, W_BV), F32),
                   jax.ShapeDtypeStruct((batch, W_BK, DV_B), F32)],
        scratch_shapes=[pltpu.VMEM((batch, H_B // 2, DV_B, 2 * DK_B), F32)],
        compiler_params=_params("arbitrary"),
        name="gla_prompt",
    )(bq.reshape(batch, seq, W_BK), bk.reshape(batch, seq, W_BK), glog.reshape(batch, seq, W_BK),
      bv.reshape(batch, seq, W_BV), s0)
    return o.reshape(m, W_BV), s_fin


def _column(row):
    n = row.shape[1]
    r = lax.broadcasted_iota(jnp.int32, (n, n), 0)
    c = lax.broadcasted_iota(jnp.int32, (n, n), 1)
    return jnp.sum(jnp.where(r == c, jnp.broadcast_to(row, (n, n)), 0.0), axis=1, keepdims=True)


def _gla_step_body(q_ref, k_ref, g_ref, v_ref, s0_ref, o_ref, s_ref):
    pair = 2 * DK_B
    for i in range(q_ref.shape[0]):
        for p in range(H_B // 2):
            sl = slice(p * pair, (p + 1) * pair)
            qc = _column(q_ref[i, :, sl])
            kc = _column(k_ref[i, :, sl])
            dc = jnp.exp(_column(g_ref[i, :, sl]))
            for hh in range(2):
                h = 2 * p + hh
                rs = slice(hh * DK_B, (hh + 1) * DK_B)
                v = v_ref[i, :, h * DV_B:(h + 1) * DV_B]
                s_new = dc[rs] * s0_ref[i, h] + kc[rs] * v
                s_ref[i, h] = s_new
                o_ref[i, :, h * DV_B:(h + 1) * DV_B] = jnp.sum(qc[rs] * s_new, axis=0, keepdims=True)


def gla_step(bq, bk, glog, bv, s0):
    bd = bq.shape[0]
    per_step = math.gcd(GLA_STEP_SEQS, bd)
    row = lambda w: pl.BlockSpec((per_step, 1, w), lambda b: (b, 0, 0))
    state = pl.BlockSpec((per_step, H_B, DK_B, DV_B), lambda b: (b, 0, 0, 0))
    o, s_fin = pl.pallas_call(
        _gla_step_body,
        grid=(bd // per_step,),
        in_specs=[row(W_BK), row(W_BK), row(W_BK), row(W_BV), state],
        out_specs=[row(W_BV), state],
        out_shape=[jax.ShapeDtypeStruct((bd, 1, W_BV), F32), jax.ShapeDtypeStruct(s0.shape, F32)],
        compiler_params=_params("arbitrary"),
        name="gla_step",
    )(bq.reshape(bd, 1, W_BK), bk.reshape(bd, 1, W_BK), glog.reshape(bd, 1, W_BK),
      bv.reshape(bd, 1, W_BV), s0)
    return o.reshape(bd, W_BV), s_fin


def _head_rms(t, g, width):
    parts = []
    for lo in range(0, t.shape[1], width):
        parts.append(_rms(t[:, lo:lo + width], g[:, lo:lo + width]))
    return jnp.concatenate(parts, axis=1)


def _even_out_ffn_body(x_ref, a_ref, b_ref, br_ref, ga_ref, gb_ref, w_ref, g_ref, wg_ref, wu_ref, wd_ref,
                       o_ref, h_ref, *, lam_init):
    a = _head_rms(a_ref[...], ga_ref[...], DV_A) * (1.0 - lam_init)
    br = br_ref[...]
    b = _head_rms(b_ref[...], gb_ref[...], DV_B) * (br * jax.nn.sigmoid(br))
    merged = jnp.concatenate([a, b], axis=1).astype(BF16)
    y = x_ref[...] + _dot(merged, w_ref[...])
    o_ref[...] = _swiglu_half(y, g_ref, wg_ref, wu_ref, wd_ref, h_ref)


def even_output_ffn(x, a_out, b_out, br, ga, gb, w, lam_init, g, wg, wu, wd, layer):
    return _ffn_call(functools.partial(_even_out_ffn_body, lam_init=lam_init), "even_output_ffn",
                     [x, a_out, b_out, br], [ga, gb, w], g, wg, wu, wd, layer, 1)


O_Q, O_K, O_V, O_O, O_G = 0, W_C, 2 * W_C, 3 * W_C, 4 * W_C
O_COLS = O_G + LANES


def _odd_gates(xn, w_ref, bias_ref):
    z = _dot(xn, w_ref[:, O_G:O_G + LANES]) + bias_ref[...]
    lane = lax.broadcasted_iota(jnp.int32, z.shape, 1)
    return jnp.where(lane < H_C, z, _log_sigmoid(z))


def _odd_in_step_body(x_ref, g_ref, w_ref, bias_ref, q_ref, k_ref, v_ref, o_ref, gt_ref):
    xn = _rms(x_ref[...], g_ref[...]).astype(BF16)
    proj = lambda lo, n: _dot(xn, w_ref[:, lo:lo + n])
    q_ref[...] = proj(O_Q, W_C).astype(BF16)
    k_ref[...] = (proj(O_K, W_C) * DH_C ** -0.5).astype(BF16)
    v_ref[...] = proj(O_V, W_C)
    o_ref[...] = proj(O_O, W_C)
    gt_ref[...] = _odd_gates(xn, w_ref, bias_ref)


def _odd_in_chunk_body(x_ref, g_ref, w_ref, wqt_ref, wvt_ref, bias_ref,
                       qt_ref, k_ref, vt_ref, o_ref, gt_ref, gtt_ref, *, chunk):
    xn = _rms(x_ref[...], g_ref[...]).astype(BF16)
    proj = lambda lo, n: _dot(xn, w_ref[:, lo:lo + n])
    qt = _dot_nt(wqt_ref[...], xn).astype(BF16)
    vt = _dot_nt(wvt_ref[...], xn).astype(BF16)
    for c in range(qt_ref.shape[1]):
        qt_ref[0, c] = qt[:, c * chunk:(c + 1) * chunk]
        vt_ref[0, c] = vt[:, c * chunk:(c + 1) * chunk]
    k_ref[...] = (proj(O_K, W_C) * DH_C ** -0.5).astype(BF16)
    o_ref[...] = proj(O_O, W_C)
    gates = _odd_gates(xn, w_ref, bias_ref)
    gt_ref[...] = gates
    gtt_ref[...] = gates.T[:2 * H_C, :]


def odd_inputs(x, g, w, bias, prompt_batch=None):
    m, d = x.shape
    tm = _row_tile(m)
    if prompt_batch is None:
        widths = [(W_C, BF16), (W_C, BF16), (W_C, F32), (W_C, F32), (LANES, F32)]
        return pl.pallas_call(
            _odd_in_step_body,
            grid=(m // tm,),
            in_specs=[_rows(tm, d), _resident((1, d)), _resident(w.shape), _resident(bias.shape)],
            out_specs=[_rows(tm, n) for n, _ in widths],
            out_shape=[jax.ShapeDtypeStruct((m, n), dt) for n, dt in widths],
            compiler_params=_params("arbitrary"),
            name="odd_inputs",
        )(x, g, w, bias)
    seq = m // prompt_batch
    chunk = min(MLSTM_CHUNK, seq)
    per_seq = seq // tm
    chunked = pl.BlockSpec((1, tm // chunk, W_C, chunk), lambda i: (i // per_seq, i % per_seq, 0, 0))
    chunked_shape = jax.ShapeDtypeStruct((prompt_batch, seq // chunk, W_C, chunk), BF16)
    wqt = w[:, O_Q:O_Q + W_C].T
    wvt = w[:, O_V:O_V + W_C].T
    return pl.pallas_call(
        functools.partial(_odd_in_chunk_body, chunk=chunk),
        grid=(m // tm,),
        in_specs=[_rows(tm, d), _resident((1, d)), _resident(w.shape), _resident(wqt.shape),
                  _resident(wvt.shape), _resident(bias.shape)],
        out_specs=[chunked, _rows(tm, W_C), chunked, _rows(tm, W_C), _rows(tm, LANES),
                   pl.BlockSpec((2 * H_C, tm), lambda i: (0, i))],
        out_shape=[chunked_shape, jax.ShapeDtypeStruct((m, W_C), BF16), chunked_shape,
                   jax.ShapeDtypeStruct((m, W_C), F32), jax.ShapeDtypeStruct((m, LANES), F32),
                   jax.ShapeDtypeStruct((2 * H_C, m), F32)],
        compiler_params=_params("arbitrary"),
        name="odd_inputs",
    )(x, g, w, wqt, wvt, bias)


def _mlstm_body(qt_ref, k_ref, vt_ref, gt_ref, gtt_ref, c0_ref, n0_ref, m0_ref,
                h_ref, cfin_ref, nfin_ref, mfin_ref, c_sc, n_sc, m_sc, *, chunk):
    n = pl.program_id(1)

    @pl.when(n == 0)
    def _():
        c_sc[...] = c0_ref[0]
        n_sc[...] = n0_ref[0]
        m_sc[...] = m0_ref[0]

    r = lax.broadcasted_iota(jnp.int32, (chunk, chunk), 0)
    c = lax.broadcasted_iota(jnp.int32, (chunk, chunk), 1)
    past = r <= c
    gates = gt_ref[...]
    gates_t = gtt_ref[...]
    cum_col = jnp.dot(jnp.where(r >= c, 1.0, 0.0), gates, precision=HIGHEST, preferred_element_type=F32)
    cum_row = jnp.dot(gates_t, jnp.where(past, 1.0, 0.0), precision=HIGHEST, preferred_element_type=F32)
    ones = jnp.ones((16, chunk), BF16)
    hs = range(H_C)
    sl = [slice(h * DH_C, (h + 1) * DH_C) for h in hs]
    m_t, dw, pw, wk, a, m_end = [], [], [], [], [], []
    for h in hs:
        i_row = gates_t[h:h + 1, :]
        b_row = cum_row[H_C + h:H_C + h + 1, :]
        b_last = b_row[:, chunk - 1:chunk]
        m_prev = m_sc[h:h + 1, 0:1]
        src_col = gates[:, h:h + 1] - cum_col[:, H_C + h:H_C + h + 1]
        dlog = jnp.where(past, src_col + b_row, -jnp.inf)
        prev_log = b_row + m_prev
        m_t.append(jnp.maximum(prev_log, jnp.max(dlog, axis=0, keepdims=True)))
        dw.append(jnp.exp(dlog - m_t[h]))
        pw.append(jnp.exp(prev_log - m_t[h]))
        m_end.append(m_t[h][:, chunk - 1:chunk])
        wk.append(jnp.exp(b_last - b_row + i_row - m_end[h]))
        a.append(jnp.exp(b_last + m_prev - m_end[h]))
    qt = [qt_ref[0, 0, sl[h], :] for h in hs]
    vt = [vt_ref[0, 0, sl[h], :] for h in hs]
    k = [k_ref[:, sl[h]] for h in hs]
    c_old = [c_sc[h] for h in hs]
    n_old = [n_sc[h:h + 1, :] for h in hs]
    qk = [_dot(k[h], qt[h]) for h in hs]
    cq = [_dot(c_old[h].astype(BF16), qt[h]) for h in hs]
    nq = [_dot(jnp.broadcast_to(n_old[h], (8, DH_C)).astype(BF16), qt[h])[0:1] for h in hs]
    intra = [_dot(jnp.concatenate([vt[h], ones], axis=0), (qk[h] * dw[h]).astype(BF16)) for h in hs]
    c_add = [_dot((vt[h].astype(F32) * wk[h]).astype(BF16), k[h]) for h in hs]
    n_add = [_dot(jnp.broadcast_to(wk[h], (8, chunk)).astype(BF16), k[h])[0:1] for h in hs]
    for h in hs:
        num = intra[h][:DH_C] + pw[h] * cq[h]
        den = intra[h][DH_C:DH_C + 1] + pw[h] * nq[h]
        h_t = num * (1.0 / jnp.maximum(jnp.abs(den), jnp.exp(-m_t[h])))
        h_ref[:, sl[h]] = h_t.T
        c_sc[h] = a[h] * c_old[h] + c_add[h]
        n_sc[h:h + 1, :] = a[h] * n_old[h] + n_add[h]
        m_sc[h:h + 1, :] = jnp.broadcast_to(m_end[h], (1, LANES))

    @pl.when(n == pl.num_programs(1) - 1)
    def _():
        cfin_ref[0] = c_sc[...]
        nfin_ref[0] = n_sc[...]
        mfin_ref[0] = m_sc[...]


def mlstm_prompt(qt, k, vt, gates, gates_t, c0, n0, m0):
    batch, nc, _, chunk = qt.shape
    m = k.shape[0]
    rows = lambda w: pl.BlockSpec((chunk, w), lambda b, n: (b * nc + n, 0))
    chunked = pl.BlockSpec((1, 1, W_C, chunk), lambda b, n: (b, n, 0, 0))
    st_c = pl.BlockSpec((1, H_C, DH_C, DH_C), lambda b, n: (b, 0, 0, 0))
    st_n = pl.BlockSpec((1, H_C, DH_C), lambda b, n: (b, 0, 0))
    st_m = pl.BlockSpec((1, H_C, LANES), lambda b, n: (b, 0, 0))
    return pl.pallas_call(
        functools.partial(_mlstm_body, chunk=chunk),
        grid=(batch, nc),
        in_specs=[chunked, rows(W_C), chunked, rows(LANES),
                  pl.BlockSpec((2 * H_C, chunk), lambda b, n: (0, b * nc + n)), st_c, st_n, st_m],
        out_specs=[rows(W_C), st_c, st_n, st_m],
        out_shape=[jax.ShapeDtypeStruct((m, W_C), F32), jax.ShapeDtypeStruct(c0.shape, F32),
                   jax.ShapeDtypeStruct(n0.shape, F32), jax.ShapeDtypeStruct(m0.shape, F32)],
        scratch_shapes=[pltpu.VMEM((H_C, DH_C, DH_C), F32), pltpu.VMEM((H_C, DH_C), F32),
                        pltpu.VMEM((H_C, LANES), F32)],
        compiler_params=_params("arbitrary", "arbitrary"),
        name="mlstm_prompt",
    )(qt, k, vt, gates, gates_t, c0, n0, m0)


def _mlstm_step_body(q_ref, k_ref, v_ref, gt_ref, c0_ref, n0_ref, m0_ref, h_ref, c_ref, n_ref, m_ref):
    for i in range(q_ref.shape[0]):
        gates = gt_ref[i]
        for h in range(H_C):
            sl = slice(h * DH_C, (h + 1) * DH_C)
            i_pre = gates[:, h:h + 1]
            logf = gates[:, H_C + h:H_C + h + 1]
            m_prev = m0_ref[i, h:h + 1, 0:1]
            prev_log = logf + m_prev
            m_t = jnp.maximum(prev_log, i_pre)
            dw = jnp.exp(i_pre - m_t)
            pw = jnp.exp(prev_log - m_t)
            q = q_ref[i, :, sl]
            k = k_ref[i, :, sl]
            v = v_ref[i, :, sl]
            qf = q.astype(F32)
            kf = k.astype(F32)
            c_old = c0_ref[i, h]
            n_old = n0_ref[i, h:h + 1, :]
            sw = jnp.sum(qf * kf, axis=1, keepdims=True) * dw
            cq = _dot_nt(jnp.broadcast_to(q, (8, DH_C)), c_old.astype(BF16))[0:1]
            num = sw * v + pw * cq
            den = sw + pw * jnp.sum(qf * n_old, axis=1, keepdims=True)
            h_ref[i, :, sl] = num / jnp.maximum(jnp.abs(den), jnp.exp(-m_t))
            c_ref[i, h] = pw * c_old + _column(dw * v) * kf
            n_ref[i, h:h + 1, :] = pw * n_old + dw * kf
            m_ref[i, h:h + 1, :] = jnp.broadcast_to(m_t, (1, LANES))


def mlstm_step(q, k, v, gates, c0, n0, m0):
    bd = q.shape[0]
    per_step = math.gcd(MLSTM_STEP_SEQS, bd)
    row = lambda w: pl.BlockSpec((per_step, 1, w), lambda b: (b, 0, 0))
    st_c = pl.BlockSpec((per_step, H_C, DH_C, DH_C), lambda b: (b, 0, 0, 0))
    st_n = pl.BlockSpec((per_step, H_C, DH_C), lambda b: (b, 0, 0))
    st_m = pl.BlockSpec((per_step, H_C, LANES), lambda b: (b, 0, 0))
    h, c_fin, n_fin, m_fin = pl.pallas_call(
        _mlstm_step_body,
        grid=(bd // per_step,),
        in_specs=[row(W_C), row(W_C), row(W_C), row(LANES), st_c, st_n, st_m],
        out_specs=[row(W_C), st_c, st_n, st_m],
        out_shape=[jax.ShapeDtypeStruct((bd, 1, W_C), F32), jax.ShapeDtypeStruct(c0.shape, F32),
                   jax.ShapeDtypeStruct(n0.shape, F32), jax.ShapeDtypeStruct(m0.shape, F32)],
        compiler_params=_params("arbitrary"),
        name="mlstm_step",
    )(q.reshape(bd, 1, W_C), k.reshape(bd, 1, W_C), v.reshape(bd, 1, W_C), gates.reshape(bd, 1, LANES),
      c0, n0, m0)
    return h.reshape(bd, W_C), c_fin, n_fin, m_fin


def _odd_out_ffn_body(x_ref, hm_ref, og_ref, gn_ref, w_ref, g_ref, wg_ref, wu_ref, wd_ref, o_ref, h_ref):
    gated = jax.nn.sigmoid(og_ref[...]) * _head_rms(hm_ref[...], gn_ref[...], DH_C)
    y = x_ref[...] + _dot(gated.astype(BF16), w_ref[...])
    o_ref[...] = _swiglu_half(y, g_ref, wg_ref, wu_ref, wd_ref, h_ref)


def odd_output_ffn(x, hm, og, gn, w, g, wg, wu, wd, layer):
    return _ffn_call(_odd_out_ffn_body, "odd_output_ffn", [x, hm, og], [gn, w], g, wg, wu, wd, layer, 1)


def _even_weights(w_in, w_out, qk_norm, gate_w2, gate_b, a_norm, b_norm):
    main = 3 * W_A + 2 * W_BK + W_BV
    w = jnp.concatenate([w_in[:, :main], w_in[:, main + GLA_RANK:], w_in[:, main:main + GLA_RANK],
                         jnp.zeros((w_in.shape[0], LANES - GLA_RANK), w_in.dtype)], axis=1).astype(BF16)
    w2 = jnp.concatenate([gate_w2, jnp.zeros((LANES - GLA_RANK, W_BK), gate_w2.dtype)], axis=0).astype(BF16)
    idx = jnp.arange(MXU_WIDTH) // DK_A
    seg = (idx[:, None] == idx[None, :]).astype(BF16)
    gq = jnp.tile(qk_norm[0], W_A // DK_A).reshape(1, W_A)
    gk = jnp.tile(qk_norm[1], W_A // DK_A).reshape(1, W_A)
    return dict(w=w, w2=w2, seg=seg, gq=gq, gk=gk, gb=gate_b.reshape(1, W_BK), w_out=w_out.astype(BF16),
                ga=a_norm.reshape(1, W_A), gbn=b_norm.reshape(1, W_BV))


def _odd_weights(w_in, w_out, gate_b, norm_g):
    w = jnp.concatenate([w_in, jnp.zeros((w_in.shape[0], LANES - 2 * H_C), w_in.dtype)], axis=1).astype(BF16)
    bias = jnp.concatenate([gate_b[0], gate_b[1], jnp.zeros((LANES - 2 * H_C,), gate_b.dtype)]).reshape(1, LANES)
    return dict(w=w, bias=bias, w_out=w_out.astype(BF16), g=norm_g.reshape(1, W_C))


def kernel(x_prompt, x_sample, cache_k, cache_v, state_gla, state_mlstm_C, state_mlstm_n, state_mlstm_m,
           page_table, norm_g, ffn_w_gate, ffn_w_up, ffn_w_down, even_w_in, even_w_out, a_qk_norm, a_lambda,
           a_head_norm, b_gate_w2, b_gate_bias, b_head_norm, odd_w_in, odd_w_out, c_gate_bias, c_head_norm):
    bp, seq, d = x_prompt.shape
    bd = x_sample.shape[0]
    depth = norm_g.shape[0]
    assert x_sample.shape[1] == 1
    yp = x_prompt.reshape(bp * seq, d)
    ys = x_sample.reshape(bd, d)
    n_even, n_pool, page = cache_k.shape[:3]
    ck = jnp.transpose(cache_k, (0, 1, 3, 4, 5, 2)).reshape(n_even, n_pool, W_A, page)
    cv = cache_v.reshape(n_even, n_pool, page * H_A, DV_A)
    slopes = jnp.asarray([2.0 ** (-8.0 * (h + 1) / H_A) for h in range(H_A)], F32)
    slope8 = jnp.repeat(slopes, 2).reshape(2 * H_A, 1)
    wg, wu, wd = ffn_w_gate.astype(BF16), ffn_w_up.astype(BF16), ffn_w_down.astype(BF16)
    pos = jnp.arange(seq)[:, None]
    lane = jnp.arange(LANES)[None, :]
    posb = jnp.where(lane >= 6, 0, jnp.where(lane % 2 == 0, pos // 64 * 64, pos % 64)).astype(BF16)

    k_p, v_p, k_s, v_s, gla_p, gla_s = [], [], [], [], [], []
    cm_p, nm_p, mm_p, cm_s, nm_s, mm_s = [], [], [], [], [], []
    for li in range(depth):
        g = norm_g[li].reshape(3, 1, d)
        yp = ffn_half(yp, g[0], wg, wu, wd, li, 0)
        ys = ffn_half(ys, g[0], wg, wu, wd, li, 0)
        if li % 2 == 0:
            e = li // 2
            lam_init = 0.8 - 0.6 * math.exp(-0.3 * li)
            ew = _even_weights(even_w_in[e], even_w_out[e], a_qk_norm[e], b_gate_w2[e], b_gate_bias[e],
                               a_head_norm[e], b_head_norm[e])
            lam_vecs = a_lambda[e]
            aqt, akb, akt, av, avt, bq, bk, bv, gl, br = even_inputs(
                yp, g[1], ew["w"], ew["seg"], ew["gq"], ew["gk"], ew["w2"], ew["gb"], prompt_batch=bp)
            a_out = diff_attn_prompt(aqt, akb, avt, posb, slopes, lam_vecs, lam_init)
            b_out, s_fin = gla_prompt(bq, bk, gl, bv, jnp.zeros((bp, W_BK, DV_B), F32), bp)
            yp = even_output_ffn(yp, a_out, b_out, br, ew["ga"], ew["gbn"], ew["w_out"], lam_init,
                                 g[2], wg, wu, wd, li)
            k_p.append(jnp.transpose(akt.reshape(bp, H_A, 2, DK_A, seq), (0, 4, 1, 2, 3)))
            v_p.append(av.reshape(bp, seq, H_A, DV_A))
            gla_p.append(s_fin.reshape(bp, H_B, DK_B, DV_B))
            aq, ak, av, bq, bk, bv, gl, br = even_inputs(
                ys, g[1], ew["w"], ew["seg"], ew["gq"], ew["gk"], ew["w2"], ew["gb"])
            a_out = diff_attn_decode(page_table, aq, ak, av, slope8, lam_vecs, lam_init, ck, cv, e)
            b_out, s_fin = gla_step(bq, bk, gl, bv, state_gla[e])
            ys = even_output_ffn(ys, a_out, b_out, br, ew["ga"], ew["gbn"], ew["w_out"], lam_init,
                                 g[2], wg, wu, wd, li)
            k_s.append(ak.reshape(bd, 1, H_A, 2, DK_A))
            v_s.append(av.reshape(bd, 1, H_A, DV_A))
            gla_s.append(s_fin)
        else:
            o = li // 2
            ow = _odd_weights(odd_w_in[o], odd_w_out[o], c_gate_bias[o], c_head_norm[o])
            qt, k, vt, og, gates, gates_t = odd_inputs(yp, g[1], ow["w"], ow["bias"], prompt_batch=bp)
            h, c_fin, n_fin, m_fin = mlstm_prompt(
                qt, k, vt, gates, gates_t, jnp.zeros((bp, H_C, DH_C, DH_C), F32),
                jnp.zeros((bp, H_C, DH_C), F32), jnp.zeros((bp, H_C, LANES), F32))
            yp = odd_output_ffn(yp, h, og, ow["g"], ow["w_out"], g[2], wg, wu, wd, li)
            cm_p.append(c_fin)
            nm_p.append(n_fin)
            mm_p.append(m_fin[:, :, 0])
            q, k, v, og, gates = odd_inputs(ys, g[1], ow["w"], ow["bias"])
            m0 = jnp.broadcast_to(state_mlstm_m[o][:, :, None], (bd, H_C, LANES))
            h, c_fin, n_fin, m_fin = mlstm_step(q, k, v, gates, state_mlstm_C[o], state_mlstm_n[o], m0)
            ys = odd_output_ffn(ys, h, og, ow["g"], ow["w_out"], g[2], wg, wu, wd, li)
            cm_s.append(c_fin)
            nm_s.append(n_fin)
            mm_s.append(m_fin[:, :, 0])
    return (yp.reshape(bp, seq, d), ys.reshape(bd, 1, d),
            jnp.stack(k_p), jnp.stack(v_p), jnp.stack(k_s), jnp.stack(v_s),
            jnp.stack(gla_p), jnp.stack(gla_s),
            jnp.stack(cm_p), jnp.stack(nm_p), jnp.stack(mm_p),
            jnp.stack(cm_s), jnp.stack(nm_s), jnp.stack(mm_s))
```

```python
import functools
import math

import jax
import jax.numpy as jnp
from jax import lax
from jax.experimental import pallas as pl
from jax.experimental.pallas import tpu as pltpu

F32 = jnp.float32
BF16 = jnp.bfloat16
EPS = 1e-6
HIGHEST = lax.Precision.HIGHEST
LOG2E = math.log2(math.e)

LANES = 128
MXU_WIDTH = 256
VMEM_LIMIT_BYTES = 56 * 2**20

H_A, DK_A, DV_A = 4, 64, 128
H_B, DK_B, DV_B = 4, 64, 128
GLA_RANK = 16
GLA_NORMALIZER = 16.0
H_C, DH_C = 4, 256
W_A = H_A * 2 * DK_A
W_BK = H_B * DK_B
W_BV = H_B * DV_B
W_C = H_C * DH_C

ROW_TILE = 512
ATTN_TILE = 256
ATTN_KV_CHUNK = 1024
ATTN_HEADS_PER_STEP = 4
GLA_CHUNK = 128
MLSTM_CHUNK = 256
DECODE_PAGES = 16
GLA_STEP_SEQS = 8
MLSTM_STEP_SEQS = 2

NT_DIMS = (((1,), (1,)), ((), ()))


def _params(*semantics):
    return pltpu.CompilerParams(dimension_semantics=semantics, vmem_limit_bytes=VMEM_LIMIT_BYTES)


def _resident(shape):
    return pl.BlockSpec(shape, lambda *_: (0,) * len(shape), pipeline_mode=pl.Buffered(1))


def _rows(tm, width):
    return pl.BlockSpec((tm, width), lambda i: (i, 0))


def _row_tile(m):
    return ROW_TILE if m % ROW_TILE == 0 else m


def _rms(x, g):
    return x * lax.rsqrt(jnp.mean(x * x, axis=-1, keepdims=True) + EPS) * g


def _log_sigmoid(x):
    return jnp.minimum(x, 0.0) - jnp.log1p(jnp.exp(-jnp.abs(x)))


def _dot(a, b):
    return jnp.dot(a, b, preferred_element_type=F32)


def _dot_nt(a, b):
    return lax.dot_general(a, b, NT_DIMS, preferred_element_type=F32)


def _diff_lambda(lam_vecs, lam_init):
    p1 = jnp.sum(lam_vecs[0:1] * lam_vecs[1:2], axis=-1, keepdims=True)
    p2 = jnp.sum(lam_vecs[2:3] * lam_vecs[3:4], axis=-1, keepdims=True)
    return jnp.exp(p1) - jnp.exp(p2) + lam_init


FF_CHUNK = 256


def _swiglu_half(x, g_ref, wg_ref, wu_ref, wd_ref, h_ref):
    xn = _rms(x, g_ref[...]).astype(BF16)
    for c in range(0, wg_ref.shape[1], FF_CHUNK):
        hg = _dot(xn, wg_ref[:, c:c + FF_CHUNK])
        hu = _dot(xn, wu_ref[:, c:c + FF_CHUNK])
        h_ref[:, c:c + FF_CHUNK] = (hg * jax.nn.sigmoid(hg) * hu).astype(BF16)
    return x + 0.5 * _dot(h_ref[...], wd_ref[...])


def _ffn_kernel(*refs, prologue, n_prologue, cast_next):
    g_ref, wg_ref, wu_ref, wd_ref = refs[n_prologue:n_prologue + 4]
    rest = refs[n_prologue + 4:]
    if cast_next:
        src, o_ref, dst, h_ref = rest[:3], rest[3], rest[4:7], rest[7]
    else:
        o_ref, h_ref = rest
    o_ref[...] = _swiglu_half(prologue(*refs[:n_prologue]), g_ref, wg_ref, wu_ref, wd_ref, h_ref)
    if cast_next:
        for s_ref, d_ref in zip(src, dst):
            d_ref[...] = s_ref[...].astype(BF16)


def _ffn_call(prologue, name, row_inputs, small_inputs, g, weights, cast_next=None):
    wg, wu, wd = weights
    m, d = row_inputs[0].shape
    dff = wg.shape[-1]
    tm = _row_tile(m)
    steps = m // tm
    in_specs = ([_rows(tm, a.shape[1]) for a in row_inputs] + [_resident(a.shape) for a in small_inputs]
                + [_resident((1, d)), _resident((d, dff)), _resident((d, dff)), _resident((dff, d))])
    out_specs = [_rows(tm, d)]
    out_shape = [jax.ShapeDtypeStruct((m, d), F32)]
    operands = [*row_inputs, *small_inputs, g, wg, wu, wd]
    if cast_next is not None:
        *next32, layer, half = cast_next
        up_rows, down_rows, down_div = d // steps, dff // (steps // 2), 2
        assert up_rows * steps == d and down_rows * (steps // down_div) == dff
        assert up_rows % 16 == 0 and down_rows % 16 == 0
        in_specs += [pl.BlockSpec((None, None, up_rows, dff), lambda i: (layer, half, i, 0)),
                     pl.BlockSpec((None, None, up_rows, dff), lambda i: (layer, half, i, 0)),
                     pl.BlockSpec((None, None, down_rows, d), lambda i: (layer, half, i // down_div, 0))]
        out_specs += [pl.BlockSpec((up_rows, dff), lambda i: (i, 0)),
                      pl.BlockSpec((up_rows, dff), lambda i: (i, 0)),
                      pl.BlockSpec((down_rows, d), lambda i: (i // down_div, 0))]
        out_shape += [jax.ShapeDtypeStruct((d, dff), BF16), jax.ShapeDtypeStruct((d, dff), BF16),
                      jax.ShapeDtypeStruct((dff, d), BF16)]
        operands += next32
    out = pl.pallas_call(
        functools.partial(_ffn_kernel, prologue=prologue, n_prologue=len(row_inputs) + len(small_inputs),
                          cast_next=cast_next is not None),
        grid=(steps,),
        in_specs=in_specs,
        out_specs=out_specs,
        out_shape=out_shape,
        scratch_shapes=[pltpu.VMEM((tm, dff), BF16)],
        compiler_params=_params("arbitrary"),
        name=name,
    )(*operands)
    return out[0], tuple(out[1:])


def ffn_half(x, g, weights, cast_next=None):
    return _ffn_call(lambda x_ref: x_ref[...], "ffn_half", [x], [], g, weights, cast_next)


E_AQ, E_AK, E_AV = 0, W_A, 2 * W_A
E_BQ = 3 * W_A
E_BK = E_BQ + W_BK
E_BV = E_BK + W_BK
E_BR = E_BV + W_BV
E_BG = E_BR + W_BV
E_COLS = E_BG + LANES


def _even_in_body(x_ref, g_ref, w_ref, seg_ref, gq_ref, gk_ref, w2_ref, gb_ref, *out_refs, attn_tile):
    xn = _rms(x_ref[...], g_ref[...]).astype(BF16)
    proj = lambda lo, n: _dot(xn, w_ref[:, lo:lo + n])

    def group_rms(t, g):
        t2 = (t * t).astype(BF16)
        ms = jnp.concatenate([_dot(t2[:, c:c + MXU_WIDTH], seg_ref[...])
                              for c in range(0, t.shape[1], MXU_WIDTH)], axis=1)
        return t * lax.rsqrt(ms * (1.0 / DK_A) + EPS) * g

    aq = group_rms(proj(E_AQ, W_A), gq_ref[...]) * DK_A ** -0.5
    ak = group_rms(proj(E_AK, W_A), gk_ref[...])
    av = proj(E_AV, W_A)
    if attn_tile is None:
        aq_ref, ak_ref, av_ref, bq_ref, bk_ref, bv_ref, gl_ref, br_ref = out_refs
        aq_ref[...] = aq.astype(BF16)
        ak_ref[...] = ak
        av_ref[...] = av
    else:
        aqt_ref, akb_ref, akt_ref, av_ref, avt_ref, bq_ref, bk_ref, bv_ref, gl_ref, br_ref = out_refs
        aqt = (aq * LOG2E).T.astype(BF16)
        avt = av.T.astype(BF16)
        for c in range(aqt_ref.shape[1]):
            aqt_ref[0, c] = aqt[:, c * attn_tile:(c + 1) * attn_tile]
            avt_ref[0, c] = avt[:, c * attn_tile:(c + 1) * attn_tile]
        akb_ref[...] = ak.astype(BF16)
        akt_ref[0] = ak.T
        for h in range(H_A):
            av_ref[pl.ds(h, av.shape[0], stride=H_A), :] = av[:, h * DV_A:(h + 1) * DV_A]
    bq_ref[...] = proj(E_BQ, W_BK) * DK_B ** -0.5
    bk_ref[...] = proj(E_BK, W_BK)
    bv_ref[...] = proj(E_BV, W_BV)
    br_ref[...] = proj(E_BR, W_BV)
    bg = proj(E_BG, LANES).astype(BF16)
    gl_ref[...] = _log_sigmoid(_dot(bg, w2_ref[...]) + gb_ref[...]) * (1.0 / GLA_NORMALIZER)


def even_inputs(x, g, w, seg, gq, gk, w2, gb, prompt_batch=None):
    m, d = x.shape
    tm = _row_tile(m)
    common = [(W_BK, F32), (W_BK, F32), (W_BV, F32), (W_BK, F32), (W_BV, F32)]
    if prompt_batch is None:
        attn_tile = None
        widths = [(W_A, BF16), (W_A, F32), (W_A, F32)] + common
        out_specs = [_rows(tm, n) for n, _ in widths]
        out_shape = [jax.ShapeDtypeStruct((m, n), dt) for n, dt in widths]
    else:
        seq = m // prompt_batch
        attn_tile = min(ATTN_TILE, seq)
        per_seq = seq // tm
        per_tile = tm // attn_tile
        chunked = pl.BlockSpec((1, per_tile, W_A, attn_tile), lambda i: (i // per_seq, i % per_seq, 0, 0))
        chunked_shape = jax.ShapeDtypeStruct((prompt_batch, seq // attn_tile, W_A, attn_tile), BF16)
        out_specs = ([chunked, _rows(tm, W_A),
                      pl.BlockSpec((1, W_A, tm), lambda i: (i // per_seq, 0, i % per_seq)),
                      _rows(tm * H_A, DV_A), chunked] + [_rows(tm, n) for n, _ in common])
        out_shape = ([chunked_shape, jax.ShapeDtypeStruct((m, W_A), BF16),
                      jax.ShapeDtypeStruct((prompt_batch, W_A, seq), F32),
                      jax.ShapeDtypeStruct((m * H_A, DV_A), F32), chunked_shape]
                     + [jax.ShapeDtypeStruct((m, n), dt) for n, dt in common])
    return pl.pallas_call(
        functools.partial(_even_in_body, attn_tile=attn_tile),
        grid=(m // tm,),
        in_specs=[_rows(tm, d), _resident((1, d)), _resident(w.shape), _resident(seg.shape),
                  _resident(gq.shape), _resident(gk.shape), _resident(w2.shape), _resident(gb.shape)],
        out_specs=out_specs,
        out_shape=out_shape,
        compiler_params=_params("arbitrary"),
        name="even_inputs",
    )(x, g, w, seg, gq, gk, w2, gb)


def _attn_body(slope_ref, lam_ref, qt_ref, k_ref, posb_ref, vt_ref, o_ref, m_ref, acc_ref, *, tile, kv, lam_init):
    qi = pl.program_id(2)
    heads = m_ref.shape[0] // 2
    row = lax.broadcasted_iota(jnp.int32, (DV_A, tile), 0)
    erow = lax.broadcasted_iota(jnp.int32, (LANES, tile), 0)
    rhs = []
    for hh in range(heads):
        slope = slope_ref[pl.program_id(1) * heads + hh]
        qt = qt_ref[0, 0, hh * DV_A:(hh + 1) * DV_A, :]
        c_all = jnp.full((LANES, tile), slope * LOG2E, F32)
        c_1 = c_all.astype(BF16).astype(F32)
        c_2 = (c_all - c_1).astype(BF16).astype(F32)
        c_3 = (c_all - c_1 - c_2).astype(BF16).astype(F32)
        ext = jnp.where(erow < 2, c_1, jnp.where(erow < 4, c_2, jnp.where(erow < 6, c_3, 0.0))).astype(BF16)
        for mm in range(2):
            own = (row >= mm * DK_A) & (row < (mm + 1) * DK_A)
            rhs.append(jnp.concatenate([jnp.where(own, qt, jnp.zeros_like(qt)), ext], axis=0))
    m_ref[...] = jnp.full(m_ref.shape, -jnp.inf, F32)
    acc_ref[...] = jnp.zeros(acc_ref.shape, F32)
    per_kv = kv // tile

    def step(j, masked, keys):
        rows = pl.ds(pl.multiple_of(j * kv, kv), keys)
        ones = jnp.ones((16, keys), BF16)
        posb = posb_ref[rows, :]
        scores = []
        for hh in range(heads):
            kx = jnp.concatenate([k_ref[rows, hh * DV_A:(hh + 1) * DV_A], posb], axis=1)
            scores += [_dot(kx, rhs[2 * hh]), _dot(kx, rhs[2 * hh + 1])]
        for st, s in enumerate(scores):
            hs = slice(st // 2 * DV_A, (st // 2 + 1) * DV_A)
            if masked:
                kk = lax.broadcasted_iota(jnp.int32, s.shape, 0) + (j * kv - qi * tile)
                qq = lax.broadcasted_iota(jnp.int32, s.shape, 1)
                s = jnp.where(kk <= qq, s, -jnp.inf)
            m_old = m_ref[st]
            m_new = jnp.maximum(m_old, jnp.max(s, axis=0, keepdims=True))
            p = jnp.exp2(s - m_new).astype(BF16)
            vt = jnp.concatenate([vt_ref[0, j * per_kv + c, hs, :] for c in range(keys // tile)], axis=1)
            vx = jnp.concatenate([vt, ones], axis=0)
            acc_ref[st] = jnp.exp2(m_old - m_new) * acc_ref[st] + _dot(vx, p)
            m_ref[st] = m_new

    def body(j, carry):
        step(j, False, kv)
        return carry

    n_full = (qi * tile) // kv
    lax.fori_loop(0, n_full, body, 0)
    for r in range(per_kv):
        pl.when(qi % per_kv == r)(functools.partial(step, n_full, True, (r + 1) * tile))
    lam = _diff_lambda(lam_ref[...], lam_init)
    for hh in range(heads):
        acc0 = acc_ref[2 * hh]
        acc1 = acc_ref[2 * hh + 1]
        out0 = acc0[:DV_A] * (1.0 / acc0[DV_A:DV_A + 1])
        out1 = acc1[:DV_A] * (1.0 / acc1[DV_A:DV_A + 1])
        o_ref[:, hh * DV_A:(hh + 1) * DV_A] = (out0 - lam * out1).T


def diff_attn_prompt(aqt, akb, avt, posb, slopes, lam_vecs, lam_init):
    batch, nq, _, tile = aqt.shape
    seq = nq * tile
    kv = min(ATTN_KV_CHUNK, seq)
    heads = ATTN_HEADS_PER_STEP
    wide = heads * DV_A
    return pl.pallas_call(
        functools.partial(_attn_body, tile=tile, kv=kv, lam_init=lam_init),
        grid=(batch, H_A // heads, nq),
        in_specs=[pl.BlockSpec(memory_space=pltpu.SMEM),
                  pl.BlockSpec(lam_vecs.shape, lambda b, h, i: (0, 0)),
                  pl.BlockSpec((1, 1, wide, tile), lambda b, h, i: (b, i, h, 0)),
                  pl.BlockSpec((seq, wide), lambda b, h, i: (b, h)),
                  pl.BlockSpec((seq, LANES), lambda b, h, i: (0, 0)),
                  pl.BlockSpec((1, nq, wide, tile), lambda b, h, i: (b, 0, h, 0))],
        out_specs=pl.BlockSpec((tile, wide), lambda b, h, i: (b * nq + i, h)),
        out_shape=jax.ShapeDtypeStruct((batch * seq, W_A), F32),
        scratch_shapes=[pltpu.VMEM((2 * heads, 1, tile), F32), pltpu.VMEM((2 * heads, DV_A + 16, tile), F32)],
        compiler_params=_params("arbitrary", "arbitrary", "arbitrary"),
        name="diff_attn_prompt",
    )(slopes, lam_vecs, aqt, akb, posb, avt)


def _decode_body(pt_ref, q_ref, kn_ref, vn_ref, slope_ref, lam_ref, *rest, pages, past, lam_init):
    kt_refs = rest[:pages]
    v_refs = rest[pages:2 * pages]
    o_ref, m_ref, l_ref, acc_ref = rest[2 * pages:]
    j = pl.program_id(1)
    page = kt_refs[0].shape[3]
    groups = 2 * H_A

    @pl.when(j == 0)
    def _():
        m_ref[...] = jnp.full(m_ref.shape, -jnp.inf, F32)
        l_ref[...] = jnp.zeros(l_ref.shape, F32)
        acc_ref[...] = jnp.zeros(acc_ref.shape, F32)

    qrow = q_ref[0].astype(F32)
    lane = lax.broadcasted_iota(jnp.int32, (groups, W_A), 1)
    r8 = lax.broadcasted_iota(jnp.int32, (groups, W_A), 0)
    q8 = jnp.where(lane // DK_A == r8, jnp.broadcast_to(qrow, (groups, W_A)), 0.0)
    q8_bf = q8.astype(BF16)
    slope = slope_ref[...]
    col = lax.broadcasted_iota(jnp.int32, (1, page), 1)
    row_head = lax.broadcasted_iota(jnp.int32, (groups, 1), 0) // 2

    s_parts = []
    for i in range(pages):
        kpos = (col + (j * pages + i) * page).astype(F32)
        s_parts.append(_dot(q8_bf, kt_refs[i][0, 0].astype(BF16)) + slope * kpos)
    s = jnp.concatenate(s_parts, axis=1)
    m_old = m_ref[...]
    m_new = jnp.maximum(m_old, jnp.max(s, axis=1, keepdims=True))
    p = jnp.exp(s - m_new).astype(BF16)
    alpha = jnp.exp(m_old - m_new)
    pv = jnp.zeros(acc_ref.shape, F32)
    for i in range(pages):
        p_i = p[:, i * page:(i + 1) * page]
        for h in range(H_A):
            v_h = v_refs[i][0, 0, pl.ds(h, page, stride=H_A), :].astype(BF16)
            pv = pv + jnp.where(row_head == h, _dot(p_i, v_h), 0.0)
    l_ref[...] = alpha * l_ref[...] + jnp.sum(p.astype(F32), axis=1, keepdims=True)
    acc_ref[...] = alpha * acc_ref[...] + pv
    m_ref[...] = m_new

    @pl.when(j == pl.num_programs(1) - 1)
    def _():
        s_new = jnp.sum(q8 * kn_ref[0], axis=1, keepdims=True) + slope * float(past)
        v_new = jnp.zeros(acc_ref.shape, F32)
        for h in range(H_A):
            v_row = jnp.broadcast_to(vn_ref[0, :, h * DV_A:(h + 1) * DV_A], acc_ref.shape)
            v_new = jnp.where(row_head == h, v_row, v_new)
        m_old = m_ref[...]
        m_fin = jnp.maximum(m_old, s_new)
        p_new = jnp.exp(s_new - m_fin)
        alpha = jnp.exp(m_old - m_fin)
        l_fin = alpha * l_ref[...] + p_new
        o8 = (alpha * acc_ref[...] + p_new * v_new) / l_fin
        lam = _diff_lambda(lam_ref[...], lam_init)
        for h in range(H_A):
            o_ref[0, :, h * DV_A:(h + 1) * DV_A] = o8[2 * h:2 * h + 1] - lam * o8[2 * h + 1:2 * h + 2]


def diff_attn_decode(page_table, aq, ak, av, slope8, lam_vecs, lam_init, cache_kt, cache_v, layer):
    bd = aq.shape[0]
    n_pages = page_table.shape[1]
    page = cache_kt.shape[3]
    pages = math.gcd(DECODE_PAGES, n_pages)
    past = n_pages * page

    def page_spec(i, shape):
        return pl.BlockSpec((1, 1) + shape, lambda b, j, pt: (layer, pt[b * n_pages + j * pages + i], 0, 0))

    row3 = lambda: pl.BlockSpec((1, 1, W_A), lambda b, j, pt: (b, 0, 0))
    const = lambda a: pl.BlockSpec(a.shape, lambda b, j, pt: (0,) * a.ndim)
    grid_spec = pltpu.PrefetchScalarGridSpec(
        num_scalar_prefetch=1,
        grid=(bd, n_pages // pages),
        in_specs=[row3(), row3(), row3(), const(slope8), const(lam_vecs)]
        + [page_spec(i, (W_A, page)) for i in range(pages)]
        + [page_spec(i, (page * H_A, DV_A)) for i in range(pages)],
        out_specs=row3(),
        scratch_shapes=[pltpu.VMEM((2 * H_A, 1), F32), pltpu.VMEM((2 * H_A, 1), F32),
                        pltpu.VMEM((2 * H_A, DV_A), F32)],
    )
    out = pl.pallas_call(
        functools.partial(_decode_body, pages=pages, past=past, lam_init=lam_init),
        grid_spec=grid_spec,
        out_shape=jax.ShapeDtypeStruct((bd, 1, W_A), F32),
        compiler_params=_params("arbitrary", "arbitrary"),
        name="diff_attn_decode",
    )(page_table.reshape(-1), aq.reshape(bd, 1, W_A), ak.reshape(bd, 1, W_A), av.reshape(bd, 1, W_A),
      slope8, lam_vecs, *([cache_kt] * pages), *([cache_v] * pages))
    return out.reshape(bd, W_A)


def _gla_body(q_ref, k_ref, g_ref, v_ref, s0_ref, o_ref, sfin_ref, st_ref, *, chunk):
    n = pl.program_id(0)
    nb = q_ref.shape[0]
    pair = 2 * DK_B

    @pl.when(n == 0)
    def _():
        for b in range(nb):
            for p in range(H_B // 2):
                st_ref[b, p] = s0_ref[b, p * pair:(p + 1) * pair, :].T

    r = lax.broadcasted_iota(jnp.int32, (chunk, chunk), 0)
    c = lax.broadcasted_iota(jnp.int32, (chunk, chunk), 1)
    causal = r >= c
    tri = jnp.where(causal, 1.0, 0.0)
    lane_head = lax.broadcasted_iota(jnp.int32, (1, pair), 1) // DK_B
    mid = chunk // 2
    zero = jnp.zeros((), BF16)
    heads = [(b, h) for b in range(nb) for h in range(H_B)]
    lanes_of = lambda h: slice(h // 2 * pair, (h // 2 + 1) * pair)
    vals_of = lambda h: slice(h * DV_B, (h + 1) * DV_B)
    own_of = lambda h: lane_head == h % 2
    cums = [jnp.dot(tri, g_ref[b], precision=HIGHEST, preferred_element_type=F32) for b in range(nb)]
    q_intra, k_intra, q_inter, k_end, decay = [], [], [], [], []
    for b in range(nb):
        cum = cums[b]
        c_mid = cum[mid - 1:mid, :]
        c_last = cum[chunk - 1:chunk, :]
        q = q_ref[b]
        k = k_ref[b]
        q_intra.append((q * jnp.exp(cum - c_mid)).astype(BF16))
        k_intra.append((k * jnp.exp(c_mid - cum)).astype(BF16))
        q_inter.append((q * jnp.exp(cum)).astype(BF16))
        k_end.append((k * jnp.exp(c_last - cum)).astype(BF16))
        decay.append(jnp.exp(c_last))
    scores = [_dot_nt(jnp.where(own_of(h), q_intra[b][:, lanes_of(h)], zero), k_intra[b][:, lanes_of(h)])
              for b, h in heads]
    states = [[st_ref[b, p] for p in range(H_B // 2)] for b in range(nb)]
    carried = [_dot_nt(jnp.where(own_of(h), q_inter[b][:, lanes_of(h)], zero), states[b][h // 2].astype(BF16))
               for b, h in heads]
    updates = [_dot(v_ref[b, :, vals_of(h)].T.astype(BF16), jnp.where(own_of(h), k_end[b][:, lanes_of(h)], zero))
               for b, h in heads]
    for i, (b, h) in enumerate(heads):
        a = jnp.where(causal, scores[i], 0.0).astype(BF16)
        o_ref[b, :, vals_of(h)] = _dot(a, v_ref[b, :, vals_of(h)].astype(BF16)) + carried[i]
    for b in range(nb):
        for p in range(H_B // 2):
            st_ref[b, p] = (states[b][p] * decay[b][:, p * pair:(p + 1) * pair]
                            + updates[b * H_B + 2 * p] + updates[b * H_B + 2 * p + 1])

    @pl.when(n == pl.num_programs(0) - 1)
    def _():
        for b in range(nb):
            for p in range(H_B // 2):
                sfin_ref[b, p * pair:(p + 1) * pair, :] = st_ref[b, p].T


def gla_prompt(bq, bk, glog, bv, s0, batch):
    m = bq.shape[0]
    seq = m // batch
    chunk = min(GLA_CHUNK, seq)
    blk = lambda w: pl.BlockSpec((batch, chunk, w), lambda n: (0, n, 0))
    state = pl.BlockSpec((batch, W_BK, DV_B), lambda n: (0, 0, 0))
    o, s_fin = pl.pallas_call(
        functools.partial(_gla_body, chunk=chunk),
        grid=(seq // chunk,),
        in_specs=[blk(W_BK), blk(W_BK), blk(W_BK), blk(W_BV), state],
        out_specs=[blk(W_BV), state],
        out_shape=[jax.ShapeDtypeStruct((batch, seq, W_BV), F32),
                   jax.ShapeDtypeStruct((batch, W_BK, DV_B), F32)],
        scratch_shapes=[pltpu.VMEM((batch, H_B // 2, DV_B, 2 * DK_B), F32)],
        compiler_params=_params("arbitrary"),
        name="gla_prompt",
    )(bq.reshape(batch, seq, W_BK), bk.reshape(batch, seq, W_BK), glog.reshape(batch, seq, W_BK),
      bv.reshape(batch, seq, W_BV), s0)
    return o.reshape(m, W_BV), s_fin


def _column(row):
    n = row.shape[1]
    r = lax.broadcasted_iota(jnp.int32, (n, n), 0)
    c = lax.broadcasted_iota(jnp.int32, (n, n), 1)
    return jnp.sum(jnp.where(r == c, jnp.broadcast_to(row, (n, n)), 0.0), axis=1, keepdims=True)


def _gla_step_body(q_ref, k_ref, g_ref, v_ref, s0_ref, o_ref, s_ref):
    pair = 2 * DK_B
    for i in range(q_ref.shape[0]):
        for p in range(H_B // 2):
            sl = slice(p * pair, (p + 1) * pair)
            qc = _column(q_ref[i, :, sl])
            kc = _column(k_ref[i, :, sl])
            dc = jnp.exp(_column(g_ref[i, :, sl]))
            for hh in range(2):
                h = 2 * p + hh
                rs = slice(hh * DK_B, (hh + 1) * DK_B)
                v = v_ref[i, :, h * DV_B:(h + 1) * DV_B]
                s_new = dc[rs] * s0_ref[i, h] + kc[rs] * v
                s_ref[i, h] = s_new
                o_ref[i, :, h * DV_B:(h + 1) * DV_B] = jnp.sum(qc[rs] * s_new, axis=0, keepdims=True)


def gla_step(bq, bk, glog, bv, s0):
    bd = bq.shape[0]
    per_step = math.gcd(GLA_STEP_SEQS, bd)
    row = lambda w: pl.BlockSpec((per_step, 1, w), lambda b: (b, 0, 0))
    state = pl.BlockSpec((per_step, H_B, DK_B, DV_B), lambda b: (b, 0, 0, 0))
    o, s_fin = pl.pallas_call(
        _gla_step_body,
        grid=(bd // per_step,),
        in_specs=[row(W_BK), row(W_BK), row(W_BK), row(W_BV), state],
        out_specs=[row(W_BV), state],
        out_shape=[jax.ShapeDtypeStruct((bd, 1, W_BV), F32), jax.ShapeDtypeStruct(s0.shape, F32)],
        compiler_params=_params("arbitrary"),
        name="gla_step",
    )(bq.reshape(bd, 1, W_BK), bk.reshape(bd, 1, W_BK), glog.reshape(bd, 1, W_BK),
      bv.reshape(bd, 1, W_BV), s0)
    return o.reshape(bd, W_BV), s_fin


def _head_rms(t, g, width):
    parts = []
    for lo in range(0, t.shape[1], width):
        parts.append(_rms(t[:, lo:lo + width], g[:, lo:lo + width]))
    return jnp.concatenate(parts, axis=1)


def _even_mix(x_ref, a_ref, b_ref, br_ref, ga_ref, gb_ref, w_ref, *, lam_init):
    a = _head_rms(a_ref[...], ga_ref[...], DV_A) * (1.0 - lam_init)
    br = br_ref[...]
    b = _head_rms(b_ref[...], gb_ref[...], DV_B) * (br * jax.nn.sigmoid(br))
    merged = jnp.concatenate([a, b], axis=1).astype(BF16)
    return x_ref[...] + _dot(merged, w_ref[...])


def even_output_ffn(x, a_out, b_out, br, ga, gb, w, lam_init, g, weights, cast_next=None):
    return _ffn_call(functools.partial(_even_mix, lam_init=lam_init), "even_output_ffn",
                     [x, a_out, b_out, br], [ga, gb, w], g, weights, cast_next)


O_Q, O_K, O_V, O_O, O_G = 0, W_C, 2 * W_C, 3 * W_C, 4 * W_C
O_COLS = O_G + LANES


def _odd_gates(xn, w_ref, bias_ref):
    z = _dot(xn, w_ref[:, O_G:O_G + LANES]) + bias_ref[...]
    lane = lax.broadcasted_iota(jnp.int32, z.shape, 1)
    return jnp.where(lane < H_C, z, _log_sigmoid(z))


def _odd_in_step_body(x_ref, g_ref, w_ref, bias_ref, q_ref, k_ref, v_ref, o_ref, gt_ref):
    xn = _rms(x_ref[...], g_ref[...]).astype(BF16)
    proj = lambda lo, n: _dot(xn, w_ref[:, lo:lo + n])
    q_ref[...] = proj(O_Q, W_C).astype(BF16)
    k_ref[...] = (proj(O_K, W_C) * DH_C ** -0.5).astype(BF16)
    v_ref[...] = proj(O_V, W_C)
    o_ref[...] = proj(O_O, W_C)
    gt_ref[...] = _odd_gates(xn, w_ref, bias_ref)


def _odd_in_chunk_body(x_ref, g_ref, w_ref, wqt_ref, wvt_ref, bias_ref,
                       qt_ref, k_ref, vt_ref, o_ref, gt_ref, gtt_ref, *, chunk):
    xn = _rms(x_ref[...], g_ref[...]).astype(BF16)
    proj = lambda lo, n: _dot(xn, w_ref[:, lo:lo + n])
    qt = _dot_nt(wqt_ref[...], xn).astype(BF16)
    vt = _dot_nt(wvt_ref[...], xn).astype(BF16)
    for c in range(qt_ref.shape[1]):
        qt_ref[0, c] = qt[:, c * chunk:(c + 1) * chunk]
        vt_ref[0, c] = vt[:, c * chunk:(c + 1) * chunk]
    k_ref[...] = (proj(O_K, W_C) * DH_C ** -0.5).astype(BF16)
    o_ref[...] = proj(O_O, W_C)
    gates = _odd_gates(xn, w_ref, bias_ref)
    gt_ref[...] = gates
    gtt_ref[...] = gates.T[:2 * H_C, :]


def odd_inputs(x, g, w, bias, prompt_batch=None):
    m, d = x.shape
    tm = _row_tile(m)
    if prompt_batch is None:
        widths = [(W_C, BF16), (W_C, BF16), (W_C, F32), (W_C, F32), (LANES, F32)]
        return pl.pallas_call(
            _odd_in_step_body,
            grid=(m // tm,),
            in_specs=[_rows(tm, d), _resident((1, d)), _resident(w.shape), _resident(bias.shape)],
            out_specs=[_rows(tm, n) for n, _ in widths],
            out_shape=[jax.ShapeDtypeStruct((m, n), dt) for n, dt in widths],
            compiler_params=_params("arbitrary"),
            name="odd_inputs",
        )(x, g, w, bias)
    seq = m // prompt_batch
    chunk = min(MLSTM_CHUNK, seq)
    per_seq = seq // tm
    chunked = pl.BlockSpec((1, tm // chunk, W_C, chunk), lambda i: (i // per_seq, i % per_seq, 0, 0))
    chunked_shape = jax.ShapeDtypeStruct((prompt_batch, seq // chunk, W_C, chunk), BF16)
    wqt = w[:, O_Q:O_Q + W_C].T
    wvt = w[:, O_V:O_V + W_C].T
    return pl.pallas_call(
        functools.partial(_odd_in_chunk_body, chunk=chunk),
        grid=(m // tm,),
        in_specs=[_rows(tm, d), _resident((1, d)), _resident(w.shape), _resident(wqt.shape),
                  _resident(wvt.shape), _resident(bias.shape)],
        out_specs=[chunked, _rows(tm, W_C), chunked, _rows(tm, W_C), _rows(tm, LANES),
                   pl.BlockSpec((2 * H_C, tm), lambda i: (0, i))],
        out_shape=[chunked_shape, jax.ShapeDtypeStruct((m, W_C), BF16), chunked_shape,
                   jax.ShapeDtypeStruct((m, W_C), F32), jax.ShapeDtypeStruct((m, LANES), F32),
                   jax.ShapeDtypeStruct((2 * H_C, m), F32)],
        compiler_params=_params("arbitrary"),
        name="odd_inputs",
    )(x, g, w, wqt, wvt, bias)


def _mlstm_body(qt_ref, k_ref, vt_ref, gt_ref, gtt_ref, c0_ref, n0_ref, m0_ref,
                h_ref, cfin_ref, nfin_ref, mfin_ref, c_sc, n_sc, m_sc, *, chunk):
    n = pl.program_id(1)

    @pl.when(n == 0)
    def _():
        c_sc[...] = c0_ref[0]
        n_sc[...] = n0_ref[0]
        m_sc[...] = m0_ref[0]

    r = lax.broadcasted_iota(jnp.int32, (chunk, chunk), 0)
    c = lax.broadcasted_iota(jnp.int32, (chunk, chunk), 1)
    past = r <= c
    gates = gt_ref[...]
    gates_t = gtt_ref[...]
    cum_col = jnp.dot(jnp.where(r >= c, 1.0, 0.0), gates, precision=HIGHEST, preferred_element_type=F32)
    cum_row = jnp.dot(gates_t, jnp.where(past, 1.0, 0.0), precision=HIGHEST, preferred_element_type=F32)
    ones = jnp.ones((16, chunk), BF16)
    hs = range(H_C)
    sl = [slice(h * DH_C, (h + 1) * DH_C) for h in hs]
    m_t, dw, pw, wk, a, m_end = [], [], [], [], [], []
    for h in hs:
        i_row = gates_t[h:h + 1, :]
        b_row = cum_row[H_C + h:H_C + h + 1, :]
        b_last = b_row[:, chunk - 1:chunk]
        m_prev = m_sc[h:h + 1, 0:1]
        src_col = gates[:, h:h + 1] - cum_col[:, H_C + h:H_C + h + 1]
        dlog = jnp.where(past, src_col + b_row, -jnp.inf)
        prev_log = b_row + m_prev
        m_t.append(jnp.maximum(prev_log, jnp.max(dlog, axis=0, keepdims=True)))
        dw.append(jnp.exp(dlog - m_t[h]))
        pw.append(jnp.exp(prev_log - m_t[h]))
        m_end.append(m_t[h][:, chunk - 1:chunk])
        wk.append(jnp.exp(b_last - b_row + i_row - m_end[h]))
        a.append(jnp.exp(b_last + m_prev - m_end[h]))
    qt = [qt_ref[0, 0, sl[h], :] for h in hs]
    vt = [vt_ref[0, 0, sl[h], :] for h in hs]
    k = [k_ref[:, sl[h]] for h in hs]
    c_old = [c_sc[h] for h in hs]
    n_old = [n_sc[h:h + 1, :] for h in hs]
    qk = [_dot(k[h], qt[h]) for h in hs]
    cq = [_dot(c_old[h].astype(BF16), qt[h]) for h in hs]
    nq = [_dot(jnp.broadcast_to(n_old[h], (8, DH_C)).astype(BF16), qt[h])[0:1] for h in hs]
    intra = [_dot(jnp.concatenate([vt[h], ones], axis=0), (qk[h] * dw[h]).astype(BF16)) for h in hs]
    c_add = [_dot((vt[h].astype(F32) * wk[h]).astype(BF16), k[h]) for h in hs]
    n_add = [_dot(jnp.broadcast_to(wk[h], (8, chunk)).astype(BF16), k[h])[0:1] for h in hs]
    for h in hs:
        num = intra[h][:DH_C] + pw[h] * cq[h]
        den = intra[h][DH_C:DH_C + 1] + pw[h] * nq[h]
        h_t = num * (1.0 / jnp.maximum(jnp.abs(den), jnp.exp(-m_t[h])))
        h_ref[:, sl[h]] = h_t.T
        c_sc[h] = a[h] * c_old[h] + c_add[h]
        n_sc[h:h + 1, :] = a[h] * n_old[h] + n_add[h]
        m_sc[h:h + 1, :] = jnp.broadcast_to(m_end[h], (1, LANES))

    @pl.when(n == pl.num_programs(1) - 1)
    def _():
        cfin_ref[0] = c_sc[...]
        nfin_ref[0] = n_sc[...]
        mfin_ref[0] = m_sc[...]


def mlstm_prompt(qt, k, vt, gates, gates_t, c0, n0, m0):
    batch, nc, _, chunk = qt.shape
    m = k.shape[0]
    rows = lambda w: pl.BlockSpec((chunk, w), lambda b, n: (b * nc + n, 0))
    chunked = pl.BlockSpec((1, 1, W_C, chunk), lambda b, n: (b, n, 0, 0))
    st_c = pl.BlockSpec((1, H_C, DH_C, DH_C), lambda b, n: (b, 0, 0, 0))
    st_n = pl.BlockSpec((1, H_C, DH_C), lambda b, n: (b, 0, 0))
    st_m = pl.BlockSpec((1, H_C, LANES), lambda b, n: (b, 0, 0))
    return pl.pallas_call(
        functools.partial(_mlstm_body, chunk=chunk),
        grid=(batch, nc),
        in_specs=[chunked, rows(W_C), chunked, rows(LANES),
                  pl.BlockSpec((2 * H_C, chunk), lambda b, n: (0, b * nc + n)), st_c, st_n, st_m],
        out_specs=[rows(W_C), st_c, st_n, st_m],
        out_shape=[jax.ShapeDtypeStruct((m, W_C), F32), jax.ShapeDtypeStruct(c0.shape, F32),
                   jax.ShapeDtypeStruct(n0.shape, F32), jax.ShapeDtypeStruct(m0.shape, F32)],
        scratch_shapes=[pltpu.VMEM((H_C, DH_C, DH_C), F32), pltpu.VMEM((H_C, DH_C), F32),
                        pltpu.VMEM((H_C, LANES), F32)],
        compiler_params=_params("arbitrary", "arbitrary"),
        name="mlstm_prompt",
    )(qt, k, vt, gates, gates_t, c0, n0, m0)


def _mlstm_step_body(q_ref, k_ref, v_ref, gt_ref, c0_ref, n0_ref, m0_ref, h_ref, c_ref, n_ref, m_ref):
    for i in range(q_ref.shape[0]):
        gates = gt_ref[i]
        for h in range(H_C):
            sl = slice(h * DH_C, (h + 1) * DH_C)
            i_pre = gates[:, h:h + 1]
            logf = gates[:, H_C + h:H_C + h + 1]
            m_prev = m0_ref[i, h:h + 1, 0:1]
            prev_log = logf + m_prev
            m_t = jnp.maximum(prev_log, i_pre)
            dw = jnp.exp(i_pre - m_t)
            pw = jnp.exp(prev_log - m_t)
            q = q_ref[i, :, sl]
            k = k_ref[i, :, sl]
            v = v_ref[i, :, sl]
            qf = q.astype(F32)
            kf = k.astype(F32)
            c_old = c0_ref[i, h]
            n_old = n0_ref[i, h:h + 1, :]
            sw = jnp.sum(qf * kf, axis=1, keepdims=True) * dw
            cq = _dot_nt(jnp.broadcast_to(q, (8, DH_C)), c_old.astype(BF16))[0:1]
            num = sw * v + pw * cq
            den = sw + pw * jnp.sum(qf * n_old, axis=1, keepdims=True)
            h_ref[i, :, sl] = num / jnp.maximum(jnp.abs(den), jnp.exp(-m_t))
            c_ref[i, h] = pw * c_old + _column(dw * v) * kf
            n_ref[i, h:h + 1, :] = pw * n_old + dw * kf
            m_ref[i, h:h + 1, :] = jnp.broadcast_to(m_t, (1, LANES))


def mlstm_step(q, k, v, gates, c0, n0, m0):
    bd = q.shape[0]
    per_step = math.gcd(MLSTM_STEP_SEQS, bd)
    row = lambda w: pl.BlockSpec((per_step, 1, w), lambda b: (b, 0, 0))
    st_c = pl.BlockSpec((per_step, H_C, DH_C, DH_C), lambda b: (b, 0, 0, 0))
    st_n = pl.BlockSpec((per_step, H_C, DH_C), lambda b: (b, 0, 0))
    st_m = pl.BlockSpec((per_step, H_C, LANES), lambda b: (b, 0, 0))
    h, c_fin, n_fin, m_fin = pl.pallas_call(
        _mlstm_step_body,
        grid=(bd // per_step,),
        in_specs=[row(W_C), row(W_C), row(W_C), row(LANES), st_c, st_n, st_m],
        out_specs=[row(W_C), st_c, st_n, st_m],
        out_shape=[jax.ShapeDtypeStruct((bd, 1, W_C), F32), jax.ShapeDtypeStruct(c0.shape, F32),
                   jax.ShapeDtypeStruct(n0.shape, F32), jax.ShapeDtypeStruct(m0.shape, F32)],
        compiler_params=_params("arbitrary"),
        name="mlstm_step",
    )(q.reshape(bd, 1, W_C), k.reshape(bd, 1, W_C), v.reshape(bd, 1, W_C), gates.reshape(bd, 1, LANES),
      c0, n0, m0)
    return h.reshape(bd, W_C), c_fin, n_fin, m_fin


def _odd_mix(x_ref, hm_ref, og_ref, gn_ref, w_ref):
    gated = jax.nn.sigmoid(og_ref[...]) * _head_rms(hm_ref[...], gn_ref[...], DH_C)
    return x_ref[...] + _dot(gated.astype(BF16), w_ref[...])


def odd_output_ffn(x, hm, og, gn, w, g, weights, cast_next=None):
    return _ffn_call(_odd_mix, "odd_output_ffn", [x, hm, og], [gn, w], g, weights, cast_next)


def _even_weights(w_in, w_out, qk_norm, gate_w2, gate_b, a_norm, b_norm):
    main = 3 * W_A + 2 * W_BK + W_BV
    w = jnp.concatenate([w_in[:, :main], w_in[:, main + GLA_RANK:], w_in[:, main:main + GLA_RANK],
                         jnp.zeros((w_in.shape[0], LANES - GLA_RANK), w_in.dtype)], axis=1).astype(BF16)
    w2 = jnp.concatenate([gate_w2, jnp.zeros((LANES - GLA_RANK, W_BK), gate_w2.dtype)], axis=0).astype(BF16)
    idx = jnp.arange(MXU_WIDTH) // DK_A
    seg = (idx[:, None] == idx[None, :]).astype(BF16)
    gq = jnp.tile(qk_norm[0], W_A // DK_A).reshape(1, W_A)
    gk = jnp.tile(qk_norm[1], W_A // DK_A).reshape(1, W_A)
    return dict(w=w, w2=w2, seg=seg, gq=gq, gk=gk, gb=gate_b.reshape(1, W_BK), w_out=w_out.astype(BF16),
                ga=a_norm.reshape(1, W_A), gbn=b_norm.reshape(1, W_BV))


def _odd_weights(w_in, w_out, gate_b, norm_g):
    w = jnp.concatenate([w_in, jnp.zeros((w_in.shape[0], LANES - 2 * H_C), w_in.dtype)], axis=1).astype(BF16)
    bias = jnp.concatenate([gate_b[0], gate_b[1], jnp.zeros((LANES - 2 * H_C,), gate_b.dtype)]).reshape(1, LANES)
    return dict(w=w, bias=bias, w_out=w_out.astype(BF16), g=norm_g.reshape(1, W_C))


def kernel(x_prompt, x_sample, cache_k, cache_v, state_gla, state_mlstm_C, state_mlstm_n, state_mlstm_m,
           page_table, norm_g, ffn_w_gate, ffn_w_up, ffn_w_down, even_w_in, even_w_out, a_qk_norm, a_lambda,
           a_head_norm, b_gate_w2, b_gate_bias, b_head_norm, odd_w_in, odd_w_out, c_gate_bias, c_head_norm):
    bp, seq, d = x_prompt.shape
    bd = x_sample.shape[0]
    depth = norm_g.shape[0]
    assert x_sample.shape[1] == 1
    yp = x_prompt.reshape(bp * seq, d)
    ys = x_sample.reshape(bd, d)
    n_even, n_pool, page = cache_k.shape[:3]
    ck = jnp.transpose(cache_k, (0, 1, 3, 4, 5, 2)).reshape(n_even, n_pool, W_A, page)
    cv = cache_v.reshape(n_even, n_pool, page * H_A, DV_A)
    slopes = jnp.asarray([2.0 ** (-8.0 * (h + 1) / H_A) for h in range(H_A)], F32)
    slope8 = jnp.repeat(slopes, 2).reshape(2 * H_A, 1)
    ffn32 = (ffn_w_gate, ffn_w_up, ffn_w_down)
    w_first = tuple(w[0, 0].astype(BF16) for w in ffn32)
    pos = jnp.arange(seq)[:, None]
    lane = jnp.arange(LANES)[None, :]
    posb = jnp.where(lane >= 6, 0, jnp.where(lane % 2 == 0, pos // 64 * 64, pos % 64)).astype(BF16)

    k_p, v_p, k_s, v_s, gla_p, gla_s = [], [], [], [], [], []
    cm_p, nm_p, mm_p, cm_s, nm_s, mm_s = [], [], [], [], [], []
    for li in range(depth):
        g = norm_g[li].reshape(3, 1, d)
        yp, w_second = ffn_half(yp, g[0], w_first, cast_next=(*ffn32, li, 1))
        ys, _ = ffn_half(ys, g[0], w_first)
        cast_next = (*ffn32, li + 1, 0) if li + 1 < depth else None
        if li % 2 == 0:
            e = li // 2
            lam_init = 0.8 - 0.6 * math.exp(-0.3 * li)
            ew = _even_weights(even_w_in[e], even_w_out[e], a_qk_norm[e], b_gate_w2[e], b_gate_bias[e],
                               a_head_norm[e], b_head_norm[e])
            lam_vecs = a_lambda[e]
            aqt, akb, akt, av, avt, bq, bk, bv, gl, br = even_inputs(
                yp, g[1], ew["w"], ew["seg"], ew["gq"], ew["gk"], ew["w2"], ew["gb"], prompt_batch=bp)
            a_out = diff_attn_prompt(aqt, akb, avt, posb, slopes, lam_vecs, lam_init)
            b_out, s_fin = gla_prompt(bq, bk, gl, bv, jnp.zeros((bp, W_BK, DV_B), F32), bp)
            yp, w_first = even_output_ffn(yp, a_out, b_out, br, ew["ga"], ew["gbn"], ew["w_out"], lam_init,
                                          g[2], w_second, cast_next)
            k_p.append(jnp.transpose(akt.reshape(bp, H_A, 2, DK_A, seq), (0, 4, 1, 2, 3)))
            v_p.append(av.reshape(bp, seq, H_A, DV_A))
            gla_p.append(s_fin.reshape(bp, H_B, DK_B, DV_B))
            aq, ak, av, bq, bk, bv, gl, br = even_inputs(
                ys, g[1], ew["w"], ew["seg"], ew["gq"], ew["gk"], ew["w2"], ew["gb"])
            a_out = diff_attn_decode(page_table, aq, ak, av, slope8, lam_vecs, lam_init, ck, cv, e)
            b_out, s_fin = gla_step(bq, bk, gl, bv, state_gla[e])
            ys, _ = even_output_ffn(ys, a_out, b_out, br, ew["ga"], ew["gbn"], ew["w_out"], lam_init,
                                    g[2], w_second)
            k_s.append(ak.reshape(bd, 1, H_A, 2, DK_A))
            v_s.append(av.reshape(bd, 1, H_A, DV_A))
            gla_s.append(s_fin)
        else:
            o = li // 2
            ow = _odd_weights(odd_w_in[o], odd_w_out[o], c_gate_bias[o], c_head_norm[o])
            qt, k, vt, og, gates, gates_t = odd_inputs(yp, g[1], ow["w"], ow["bias"], prompt_batch=bp)
            h, c_fin, n_fin, m_fin = mlstm_prompt(
                qt, k, vt, gates, gates_t, jnp.zeros((bp, H_C, DH_C, DH_C), F32),
                jnp.zeros((bp, H_C, DH_C), F32), jnp.zeros((bp, H_C, LANES), F32))
            yp, w_first = odd_output_ffn(yp, h, og, ow["g"], ow["w_out"], g[2], w_second, cast_next)
            cm_p.append(c_fin)
            nm_p.append(n_fin)
            mm_p.append(m_fin[:, :, 0])
            q, k, v, og, gates = odd_inputs(ys, g[1], ow["w"], ow["bias"])
            m0 = jnp.broadcast_to(state_mlstm_m[o][:, :, None], (bd, H_C, LANES))
            h, c_fin, n_fin, m_fin = mlstm_step(q, k, v, gates, state_mlstm_C[o], state_mlstm_n[o], m0)
            ys, _ = odd_output_ffn(ys, h, og, ow["g"], ow["w_out"], g[2], w_second)
            cm_s.append(c_fin)
            nm_s.append(n_fin)
            mm_s.append(m_fin[:, :, 0])
    return (yp.reshape(bp, seq, d), ys.reshape(bd, 1, d),
            jnp.stack(k_p), jnp.stack(v_p), jnp.stack(k_s), jnp.stack(v_s),
            jnp.stack(gla_p), jnp.stack(gla_s),
            jnp.stack(cm_p), jnp.stack(nm_p), jnp.stack(mm_p),
            jnp.stack(cm_s), jnp.stack(nm_s), jnp.stack(mm_s))
```

```python
import functools
import math

import jax
import jax.numpy as jnp
from jax import lax
from jax.experimental import pallas as pl
from jax.experimental.pallas import tpu as pltpu

F32 = jnp.float32
BF16 = jnp.bfloat16
EPS = 1e-6
HIGHEST = lax.Precision.HIGHEST
LOG2E = math.log2(math.e)

LANES = 128
MXU_WIDTH = 256
VMEM_LIMIT_BYTES = 56 * 2**20

H_A, DK_A, DV_A = 4, 64, 128
H_B, DK_B, DV_B = 4, 64, 128
GLA_RANK = 16
GLA_NORMALIZER = 16.0
H_C, DH_C = 4, 256
W_A = H_A * 2 * DK_A
W_BK = H_B * DK_B
W_BV = H_B * DV_B
W_C = H_C * DH_C

ROW_TILE = 512
ATTN_TILE = 256
ATTN_KV_CHUNK = 1024
ATTN_HEADS_PER_STEP = 4
GLA_CHUNK = 128
MLSTM_CHUNK = 256
DECODE_PAGES = 32
GLA_STEP_SEQS = 8
MLSTM_STEP_SEQS = 2

NT_DIMS = (((1,), (1,)), ((), ()))


def _params(*semantics):
    return pltpu.CompilerParams(dimension_semantics=semantics, vmem_limit_bytes=VMEM_LIMIT_BYTES)


def _resident(shape):
    return pl.BlockSpec(shape, lambda *_: (0,) * len(shape), pipeline_mode=pl.Buffered(1))


def _rows(tm, width):
    return pl.BlockSpec((tm, width), lambda i: (i, 0))


def _row_tile(m):
    return ROW_TILE if m % ROW_TILE == 0 else m


def _rms(x, g):
    return x * lax.rsqrt(jnp.mean(x * x, axis=-1, keepdims=True) + EPS) * g


def _log_sigmoid(x):
    return jnp.minimum(x, 0.0) - jnp.log1p(jnp.exp(-jnp.abs(x)))


def _dot(a, b):
    return jnp.dot(a, b, preferred_element_type=F32)


def _dot_nt(a, b):
    return lax.dot_general(a, b, NT_DIMS, preferred_element_type=F32)


def _diff_lambda(lam_vecs, lam_init):
    p1 = jnp.sum(lam_vecs[0:1] * lam_vecs[1:2], axis=-1, keepdims=True)
    p2 = jnp.sum(lam_vecs[2:3] * lam_vecs[3:4], axis=-1, keepdims=True)
    return jnp.exp(p1) - jnp.exp(p2) + lam_init


FF_CHUNK = 256


def _swiglu_half(x, g_ref, wg_ref, wu_ref, wd_ref, h_ref):
    xn = _rms(x, g_ref[...]).astype(BF16)
    for c in range(0, wg_ref.shape[1], FF_CHUNK):
        hg = _dot(xn, wg_ref[:, c:c + FF_CHUNK])
        hu = _dot(xn, wu_ref[:, c:c + FF_CHUNK])
        h_ref[:, c:c + FF_CHUNK] = (hg * jax.nn.sigmoid(hg) * hu).astype(BF16)
    return x + 0.5 * _dot(h_ref[...], wd_ref[...])


def _ffn_kernel(*refs, prologue, n_rows, n_small, steps):
    rows, samples = refs[:n_rows], refs[n_rows:2 * n_rows]
    small = refs[2 * n_rows:2 * n_rows + n_small]
    g_ref, wg_ref, wu_ref, wd_ref = refs[2 * n_rows + n_small:2 * n_rows + n_small + 4]
    rest = refs[2 * n_rows + n_small + 4:]
    n_cast = (len(rest) - 3) // 2
    src, o_ref, so_ref = rest[:n_cast], rest[n_cast], rest[n_cast + 1]
    dst, h_ref = rest[n_cast + 2:2 * n_cast + 2], rest[2 * n_cast + 2]
    i = pl.program_id(0)

    @pl.when(i < steps)
    def _():
        o_ref[...] = _swiglu_half(prologue(*rows, *small), g_ref, wg_ref, wu_ref, wd_ref, h_ref)
        for s_ref, d_ref in zip(src, dst):
            d_ref[...] = s_ref[...].astype(BF16)

    @pl.when(i == steps)
    def _():
        so_ref[...] = _swiglu_half(prologue(*samples, *small), g_ref, wg_ref, wu_ref, wd_ref,
                                   h_ref.at[pl.ds(0, so_ref.shape[0])])


def _ffn_call(prologue, name, row_inputs, sample_inputs, small_inputs, g, weights, cast_next=None):
    wg, wu, wd = weights
    m, d = row_inputs[0].shape
    ms = sample_inputs[0].shape[0]
    dff = wg.shape[-1]
    tm = _row_tile(m)
    steps = m // tm
    assert ms <= tm
    tile = lambda i: jnp.minimum(i, steps - 1)
    in_specs = ([pl.BlockSpec((tm, a.shape[1]), lambda i: (tile(i), 0)) for a in row_inputs]
                + [_resident(a.shape) for a in sample_inputs] + [_resident(a.shape) for a in small_inputs]
                + [_resident((1, d)), _resident((d, dff)), _resident((d, dff)), _resident((dff, d))])
    out_specs = [pl.BlockSpec((tm, d), lambda i: (tile(i), 0)), pl.BlockSpec((ms, d), lambda i: (0, 0))]
    out_shape = [jax.ShapeDtypeStruct((m, d), F32), jax.ShapeDtypeStruct((ms, d), F32)]
    operands = [*row_inputs, *sample_inputs, *small_inputs, g, wg, wu, wd]
    if cast_next is not None:
        *next32, layer, half = cast_next
        up_rows, down_rows, down_div = d // steps, dff // (steps // 2), 2
        assert up_rows * steps == d and down_rows * (steps // down_div) == dff
        assert up_rows % 16 == 0 and down_rows % 16 == 0
        in_specs += [pl.BlockSpec((None, None, up_rows, dff), lambda i: (layer, half, tile(i), 0)),
                     pl.BlockSpec((None, None, up_rows, dff), lambda i: (layer, half, tile(i), 0)),
                     pl.BlockSpec((None, None, down_rows, d), lambda i: (layer, half, tile(i) // down_div, 0))]
        out_specs += [pl.BlockSpec((up_rows, dff), lambda i: (tile(i), 0)),
                      pl.BlockSpec((up_rows, dff), lambda i: (tile(i), 0)),
                      pl.BlockSpec((down_rows, d), lambda i: (tile(i) // down_div, 0))]
        out_shape += [jax.ShapeDtypeStruct((d, dff), BF16), jax.ShapeDtypeStruct((d, dff), BF16),
                      jax.ShapeDtypeStruct((dff, d), BF16)]
        operands += next32
    out = pl.pallas_call(
        functools.partial(_ffn_kernel, prologue=prologue, n_rows=len(row_inputs), n_small=len(small_inputs),
                          steps=steps),
        grid=(steps + 1,),
        in_specs=in_specs,
        out_specs=out_specs,
        out_shape=out_shape,
        scratch_shapes=[pltpu.VMEM((tm, dff), BF16)],
        compiler_params=_params("arbitrary"),
        name=name,
    )(*operands)
    return out[0], out[1], tuple(out[2:])


def ffn_half(x, x_sample, g, weights, cast_next=None):
    return _ffn_call(lambda x_ref: x_ref[...], "ffn_half", [x], [x_sample], [], g, weights, cast_next)


E_AQ, E_AK, E_AV = 0, W_A, 2 * W_A
E_BQ = 3 * W_A
E_BK = E_BQ + W_BK
E_BV = E_BK + W_BK
E_BR = E_BV + W_BV
E_BG = E_BR + W_BV
E_COLS = E_BG + LANES


def _even_in_body(x_ref, g_ref, w_ref, seg_ref, gq_ref, gk_ref, w2_ref, gb_ref, *out_refs, attn_tile):
    xn = _rms(x_ref[...], g_ref[...]).astype(BF16)
    proj = lambda lo, n: _dot(xn, w_ref[:, lo:lo + n])

    def group_rms(t, g):
        t2 = (t * t).astype(BF16)
        ms = jnp.concatenate([_dot(t2[:, c:c + MXU_WIDTH], seg_ref[...])
                              for c in range(0, t.shape[1], MXU_WIDTH)], axis=1)
        return t * lax.rsqrt(ms * (1.0 / DK_A) + EPS) * g

    aq = group_rms(proj(E_AQ, W_A), gq_ref[...]) * DK_A ** -0.5
    ak = group_rms(proj(E_AK, W_A), gk_ref[...])
    av = proj(E_AV, W_A)
    if attn_tile is None:
        aq_ref, ak_ref, av_ref, bq_ref, bk_ref, bv_ref, gl_ref, br_ref = out_refs
        aq_ref[...] = aq.astype(BF16)
        ak_ref[...] = ak
        av_ref[...] = av
    else:
        aqt_ref, akb_ref, akt_ref, av_ref, avt_ref, bq_ref, bk_ref, bv_ref, gl_ref, br_ref = out_refs
        aqt = (aq * LOG2E).T.astype(BF16)
        avt = av.T.astype(BF16)
        for c in range(aqt_ref.shape[1]):
            aqt_ref[0, c] = aqt[:, c * attn_tile:(c + 1) * attn_tile]
            avt_ref[0, c] = avt[:, c * attn_tile:(c + 1) * attn_tile]
        akb_ref[...] = ak.astype(BF16)
        akt_ref[0] = ak.T
        for h in range(H_A):
            av_ref[pl.ds(h, av.shape[0], stride=H_A), :] = av[:, h * DV_A:(h + 1) * DV_A]
    bq_ref[...] = proj(E_BQ, W_BK) * DK_B ** -0.5
    bk_ref[...] = proj(E_BK, W_BK)
    bv_ref[...] = proj(E_BV, W_BV)
    br_ref[...] = proj(E_BR, W_BV)
    bg = proj(E_BG, LANES).astype(BF16)
    gl_ref[...] = _log_sigmoid(_dot(bg, w2_ref[...]) + gb_ref[...]) * (1.0 / GLA_NORMALIZER)


def even_inputs(x, g, w, seg, gq, gk, w2, gb, prompt_batch=None):
    m, d = x.shape
    tm = _row_tile(m)
    common = [(W_BK, F32), (W_BK, F32), (W_BV, F32), (W_BK, F32), (W_BV, F32)]
    if prompt_batch is None:
        attn_tile = None
        widths = [(W_A, BF16), (W_A, F32), (W_A, F32)] + common
        out_specs = [_rows(tm, n) for n, _ in widths]
        out_shape = [jax.ShapeDtypeStruct((m, n), dt) for n, dt in widths]
    else:
        seq = m // prompt_batch
        attn_tile = min(ATTN_TILE, seq)
        per_seq = seq // tm
        per_tile = tm // attn_tile
        chunked = pl.BlockSpec((1, per_tile, W_A, attn_tile), lambda i: (i // per_seq, i % per_seq, 0, 0))
        chunked_shape = jax.ShapeDtypeStruct((prompt_batch, seq // attn_tile, W_A, attn_tile), BF16)
        out_specs = ([chunked, _rows(tm, W_A),
                      pl.BlockSpec((1, W_A, tm), lambda i: (i // per_seq, 0, i % per_seq)),
                      _rows(tm * H_A, DV_A), chunked] + [_rows(tm, n) for n, _ in common])
        out_shape = ([chunked_shape, jax.ShapeDtypeStruct((m, W_A), BF16),
                      jax.ShapeDtypeStruct((prompt_batch, W_A, seq), F32),
                      jax.ShapeDtypeStruct((m * H_A, DV_A), F32), chunked_shape]
                     + [jax.ShapeDtypeStruct((m, n), dt) for n, dt in common])
    return pl.pallas_call(
        functools.partial(_even_in_body, attn_tile=attn_tile),
        grid=(m // tm,),
        in_specs=[_rows(tm, d), _resident((1, d)), _resident(w.shape), _resident(seg.shape),
                  _resident(gq.shape), _resident(gk.shape), _resident(w2.shape), _resident(gb.shape)],
        out_specs=out_specs,
        out_shape=out_shape,
        compiler_params=_params("arbitrary"),
        name="even_inputs",
    )(x, g, w, seg, gq, gk, w2, gb)


def _attn_body(slope_ref, lam_ref, qt_ref, k_ref, posb_ref, vt_ref, o_ref, m_ref, acc_ref, *, tile, kv, lam_init):
    qi = pl.program_id(2)
    heads = m_ref.shape[0] // 2
    row = lax.broadcasted_iota(jnp.int32, (DV_A, tile), 0)
    erow = lax.broadcasted_iota(jnp.int32, (LANES, tile), 0)
    rhs = []
    for hh in range(heads):
        slope = slope_ref[pl.program_id(1) * heads + hh]
        qt = qt_ref[0, 0, hh * DV_A:(hh + 1) * DV_A, :]
        c_all = jnp.full((LANES, tile), slope * LOG2E, F32)
        c_1 = c_all.astype(BF16).astype(F32)
        c_2 = (c_all - c_1).astype(BF16).astype(F32)
        c_3 = (c_all - c_1 - c_2).astype(BF16).astype(F32)
        ext = jnp.where(erow < 2, c_1, jnp.where(erow < 4, c_2, jnp.where(erow < 6, c_3, 0.0))).astype(BF16)
        for mm in range(2):
            own = (row >= mm * DK_A) & (row < (mm + 1) * DK_A)
            rhs.append(jnp.concatenate([jnp.where(own, qt, jnp.zeros_like(qt)), ext], axis=0))
    m_ref[...] = jnp.full(m_ref.shape, -jnp.inf, F32)
    acc_ref[...] = jnp.zeros(acc_ref.shape, F32)
    per_kv = kv // tile

    def step(j, masked, keys):
        rows = pl.ds(pl.multiple_of(j * kv, kv), keys)
        ones = jnp.ones((16, keys), BF16)
        posb = posb_ref[rows, :]
        scores = []
        for hh in range(heads):
            kx = jnp.concatenate([k_ref[rows, hh * DV_A:(hh + 1) * DV_A], posb], axis=1)
            scores += [_dot(kx, rhs[2 * hh]), _dot(kx, rhs[2 * hh + 1])]
        for st, s in enumerate(scores):
            hs = slice(st // 2 * DV_A, (st // 2 + 1) * DV_A)
            if masked:
                kk = lax.broadcasted_iota(jnp.int32, s.shape, 0) + (j * kv - qi * tile)
                qq = lax.broadcasted_iota(jnp.int32, s.shape, 1)
                s = jnp.where(kk <= qq, s, -jnp.inf)
            m_old = m_ref[st]
            m_new = jnp.maximum(m_old, jnp.max(s, axis=0, keepdims=True))
            p = jnp.exp2(s - m_new).astype(BF16)
            vt = jnp.concatenate([vt_ref[0, j * per_kv + c, hs, :] for c in range(keys // tile)], axis=1)
            vx = jnp.concatenate([vt, ones], axis=0)
            acc_ref[st] = jnp.exp2(m_old - m_new) * acc_ref[st] + _dot(vx, p)
            m_ref[st] = m_new

    def body(j, carry):
        step(j, False, kv)
        return carry

    n_full = (qi * tile) // kv
    lax.fori_loop(0, n_full, body, 0)
    for r in range(per_kv):
        pl.when(qi % per_kv == r)(functools.partial(step, n_full, True, (r + 1) * tile))
    lam = _diff_lambda(lam_ref[...], lam_init)
    for hh in range(heads):
        acc0 = acc_ref[2 * hh]
        acc1 = acc_ref[2 * hh + 1]
        out0 = acc0[:DV_A] * (1.0 / acc0[DV_A:DV_A + 1])
        out1 = acc1[:DV_A] * (1.0 / acc1[DV_A:DV_A + 1])
        o_ref[:, hh * DV_A:(hh + 1) * DV_A] = (out0 - lam * out1).T


def diff_attn_prompt(aqt, akb, avt, posb, slopes, lam_vecs, lam_init):
    batch, nq, _, tile = aqt.shape
    seq = nq * tile
    kv = min(ATTN_KV_CHUNK, seq)
    heads = ATTN_HEADS_PER_STEP
    wide = heads * DV_A
    return pl.pallas_call(
        functools.partial(_attn_body, tile=tile, kv=kv, lam_init=lam_init),
        grid=(batch, H_A // heads, nq),
        in_specs=[pl.BlockSpec(memory_space=pltpu.SMEM),
                  pl.BlockSpec(lam_vecs.shape, lambda b, h, i: (0, 0)),
                  pl.BlockSpec((1, 1, wide, tile), lambda b, h, i: (b, i, h, 0)),
                  pl.BlockSpec((seq, wide), lambda b, h, i: (b, h)),
                  pl.BlockSpec((seq, LANES), lambda b, h, i: (0, 0)),
                  pl.BlockSpec((1, nq, wide, tile), lambda b, h, i: (b, 0, h, 0))],
        out_specs=pl.BlockSpec((tile, wide), lambda b, h, i: (b * nq + i, h)),
        out_shape=jax.ShapeDtypeStruct((batch * seq, W_A), F32),
        scratch_shapes=[pltpu.VMEM((2 * heads, 1, tile), F32), pltpu.VMEM((2 * heads, DV_A + 16, tile), F32)],
        compiler_params=_params("arbitrary", "arbitrary", "arbitrary"),
        name="diff_attn_prompt",
    )(slopes, lam_vecs, aqt, akb, posb, avt)


def _decode_body(pt_ref, q_ref, kn_ref, vn_ref, slope_ref, lam_ref, *rest, pages, past, lam_init):
    kt_refs = rest[:pages]
    v_refs = rest[pages:2 * pages]
    o_ref, m_ref, l_ref, acc_ref = rest[2 * pages:]
    j = pl.program_id(1)
    page = kt_refs[0].shape[3]
    groups = 2 * H_A

    @pl.when(j == 0)
    def _():
        m_ref[...] = jnp.full(m_ref.shape, -jnp.inf, F32)
        l_ref[...] = jnp.zeros(l_ref.shape, F32)
        acc_ref[...] = jnp.zeros(acc_ref.shape, F32)

    qrow = q_ref[0].astype(F32)
    lane = lax.broadcasted_iota(jnp.int32, (groups, W_A), 1)
    r8 = lax.broadcasted_iota(jnp.int32, (groups, W_A), 0)
    q8 = jnp.where(lane // DK_A == r8, jnp.broadcast_to(qrow, (groups, W_A)), 0.0)
    q8_bf = q8.astype(BF16)
    slope = slope_ref[...]
    col = lax.broadcasted_iota(jnp.int32, (1, page), 1)
    row_head = lax.broadcasted_iota(jnp.int32, (groups, 1), 0) // 2

    s_parts = []
    for i in range(pages):
        kpos = (col + (j * pages + i) * page).astype(F32)
        s_parts.append(_dot(q8_bf, kt_refs[i][0, 0].astype(BF16)) + slope * kpos)
    s = jnp.concatenate(s_parts, axis=1)
    m_old = m_ref[...]
    m_new = jnp.maximum(m_old, jnp.max(s, axis=1, keepdims=True))
    p = jnp.exp(s - m_new).astype(BF16)
    alpha = jnp.exp(m_old - m_new)
    pv = jnp.zeros(acc_ref.shape, F32)
    for i in range(pages):
        p_i = p[:, i * page:(i + 1) * page]
        for h in range(H_A):
            v_h = v_refs[i][0, 0, pl.ds(h, page, stride=H_A), :].astype(BF16)
            pv = pv + jnp.where(row_head == h, _dot(p_i, v_h), 0.0)
    l_ref[...] = alpha * l_ref[...] + jnp.sum(p.astype(F32), axis=1, keepdims=True)
    acc_ref[...] = alpha * acc_ref[...] + pv
    m_ref[...] = m_new

    @pl.when(j == pl.num_programs(1) - 1)
    def _():
        s_new = jnp.sum(q8 * kn_ref[0], axis=1, keepdims=True) + slope * float(past)
        v_new = jnp.zeros(acc_ref.shape, F32)
        for h in range(H_A):
            v_row = jnp.broadcast_to(vn_ref[0, :, h * DV_A:(h + 1) * DV_A], acc_ref.shape)
            v_new = jnp.where(row_head == h, v_row, v_new)
        m_old = m_ref[...]
        m_fin = jnp.maximum(m_old, s_new)
        p_new = jnp.exp(s_new - m_fin)
        alpha = jnp.exp(m_old - m_fin)
        l_fin = alpha * l_ref[...] + p_new
        o8 = (alpha * acc_ref[...] + p_new * v_new) / l_fin
        lam = _diff_lambda(lam_ref[...], lam_init)
        for h in range(H_A):
            o_ref[0, :, h * DV_A:(h + 1) * DV_A] = o8[2 * h:2 * h + 1] - lam * o8[2 * h + 1:2 * h + 2]


def diff_attn_decode(page_table, aq, ak, av, slope8, lam_vecs, lam_init, cache_kt, cache_v, layer):
    bd = aq.shape[0]
    n_pages = page_table.shape[1]
    page = cache_kt.shape[3]
    pages = math.gcd(DECODE_PAGES, n_pages)
    past = n_pages * page

    def page_spec(i, shape):
        return pl.BlockSpec((1, 1) + shape, lambda b, j, pt: (layer, pt[b * n_pages + j * pages + i], 0, 0))

    row3 = lambda: pl.BlockSpec((1, 1, W_A), lambda b, j, pt: (b, 0, 0))
    const = lambda a: pl.BlockSpec(a.shape, lambda b, j, pt: (0,) * a.ndim)
    grid_spec = pltpu.PrefetchScalarGridSpec(
        num_scalar_prefetch=1,
        grid=(bd, n_pages // pages),
        in_specs=[row3(), row3(), row3(), const(slope8), const(lam_vecs)]
        + [page_spec(i, (W_A, page)) for i in range(pages)]
        + [page_spec(i, (page * H_A, DV_A)) for i in range(pages)],
        out_specs=row3(),
        scratch_shapes=[pltpu.VMEM((2 * H_A, 1), F32), pltpu.VMEM((2 * H_A, 1), F32),
                        pltpu.VMEM((2 * H_A, DV_A), F32)],
    )
    out = pl.pallas_call(
        functools.partial(_decode_body, pages=pages, past=past, lam_init=lam_init),
        grid_spec=grid_spec,
        out_shape=jax.ShapeDtypeStruct((bd, 1, W_A), F32),
        compiler_params=_params("arbitrary", "arbitrary"),
        name="diff_attn_decode",
    )(page_table.reshape(-1), aq.reshape(bd, 1, W_A), ak.reshape(bd, 1, W_A), av.reshape(bd, 1, W_A),
      slope8, lam_vecs, *([cache_kt] * pages), *([cache_v] * pages))
    return out.reshape(bd, W_A)


def _gla_body(q_ref, k_ref, g_ref, v_ref, s0_ref, o_ref, sfin_ref, st_ref, *, chunk):
    n = pl.program_id(0)
    nb = q_ref.shape[0]
    pair = 2 * DK_B

    @pl.when(n == 0)
    def _():
        for b in range(nb):
            for p in range(H_B // 2):
                st_ref[b, p] = s0_ref[b, p * pair:(p + 1) * pair, :].T

    r = lax.broadcasted_iota(jnp.int32, (chunk, chunk), 0)
    c = lax.broadcasted_iota(jnp.int32, (chunk, chunk), 1)
    causal = r >= c
    tri = jnp.where(causal, 1.0, 0.0)
    lane_head = lax.broadcasted_iota(jnp.int32, (1, pair), 1) // DK_B
    mid = chunk // 2
    zero = jnp.zeros((), BF16)
    heads = [(b, h) for b in range(nb) for h in range(H_B)]
    lanes_of = lambda h: slice(h // 2 * pair, (h // 2 + 1) * pair)
    vals_of = lambda h: slice(h * DV_B, (h + 1) * DV_B)
    own_of = lambda h: lane_head == h % 2
    cums = [jnp.dot(tri, g_ref[b], precision=HIGHEST, preferred_element_type=F32) for b in range(nb)]
    q_intra, k_intra, q_inter, k_end, decay = [], [], [], [], []
    for b in range(nb):
        cum = cums[b]
        c_mid = cum[mid - 1:mid, :]
        c_last = cum[chunk - 1:chunk, :]
        q = q_ref[b]
        k = k_ref[b]
        q_intra.append((q * jnp.exp(cum - c_mid)).astype(BF16))
        k_intra.append((k * jnp.exp(c_mid - cum)).astype(BF16))
        q_inter.append((q * jnp.exp(cum)).astype(BF16))
        k_end.append((k * jnp.exp(c_last - cum)).astype(BF16))
        decay.append(jnp.exp(c_last))
    scores = [_dot_nt(jnp.where(own_of(h), q_intra[b][:, lanes_of(h)], zero), k_intra[b][:, lanes_of(h)])
              for b, h in heads]
    states = [[st_ref[b, p] for p in range(H_B // 2)] for b in range(nb)]
    carried = [_dot_nt(jnp.where(own_of(h), q_inter[b][:, lanes_of(h)], zero), states[b][h // 2].astype(BF16))
               for b, h in heads]
    updates = [_dot(v_ref[b, :, vals_of(h)].T.astype(BF16), jnp.where(own_of(h), k_end[b][:, lanes_of(h)], zero))
               for b, h in heads]
    for i, (b, h) in enumerate(heads):
        a = jnp.where(causal, scores[i], 0.0).astype(BF16)
        o_ref[b, :, vals_of(h)] = _dot(a, v_ref[b, :, vals_of(h)].astype(BF16)) + carried[i]
    for b in range(nb):
        for p in range(H_B // 2):
            st_ref[b, p] = (states[b][p] * decay[b][:, p * pair:(p + 1) * pair]
                            + updates[b * H_B + 2 * p] + updates[b * H_B + 2 * p + 1])

    @pl.when(n == pl.num_programs(0) - 1)
    def _():
        for b in range(nb):
            for p in range(H_B // 2):
                sfin_ref[b, p * pair:(p + 1) * pair, :] = st_ref[b, p].T


def gla_prompt(bq, bk, glog, bv, s0, batch):
    m = bq.shape[0]
    seq = m // batch
    chunk = min(GLA_CHUNK, seq)
    blk = lambda w: pl.BlockSpec((batch, chunk, w), lambda n: (0, n, 0))
    state = pl.BlockSpec((batch, W_BK, DV_B), lambda n: (0, 0, 0))
    o, s_fin = pl.pallas_call(
        functools.partial(_gla_body, chunk=chunk),
        grid=(seq // chunk,),
        in_specs=[blk(W_BK), blk(W_BK), blk(W_BK), blk(W_BV), state],
        out_specs=[blk(W_BV), state],
        out_shape=[jax.ShapeDtypeStruct((batch, seq, W_BV), F32),
                   jax.ShapeDtypeStruct((batch, W_BK, DV_B), F32)],
        scratch_shapes=[pltpu.VMEM((batch, H_B // 2, DV_B, 2 * DK_B), F32)],
        compiler_params=_params("arbitrary"),
        name="gla_prompt",
    )(bq.reshape(batch, seq, W_BK), bk.reshape(batch, seq, W_BK), glog.reshape(batch, seq, W_BK),
      bv.reshape(batch, seq, W_BV), s0)
    return o.reshape(m, W_BV), s_fin


def _column(row):
    n = row.shape[1]
    r = lax.broadcasted_iota(jnp.int32, (n, n), 0)
    c = lax.broadcasted_iota(jnp.int32, (n, n), 1)
    return jnp.sum(jnp.where(r == c, jnp.broadcast_to(row, (n, n)), 0.0), axis=1, keepdims=True)


def _gla_step_body(q_ref, k_ref, g_ref, v_ref, s0_ref, o_ref, s_ref):
    pair = 2 * DK_B
    for i in range(q_ref.shape[0]):
        for p in range(H_B // 2):
            sl = slice(p * pair, (p + 1) * pair)
            qc = _column(q_ref[i, :, sl])
            kc = _column(k_ref[i, :, sl])
            dc = jnp.exp(_column(g_ref[i, :, sl]))
            for hh in range(2):
                h = 2 * p + hh
                rs = slice(hh * DK_B, (hh + 1) * DK_B)
                v = v_ref[i, :, h * DV_B:(h + 1) * DV_B]
                s_new = dc[rs] * s0_ref[i, h] + kc[rs] * v
                s_ref[i, h] = s_new
                o_ref[i, :, h * DV_B:(h + 1) * DV_B] = jnp.sum(qc[rs] * s_new, axis=0, keepdims=True)


def gla_step(bq, bk, glog, bv, s0):
    bd = bq.shape[0]
    per_step = math.gcd(GLA_STEP_SEQS, bd)
    row = lambda w: pl.BlockSpec((per_step, 1, w), lambda b: (b, 0, 0))
    state = pl.BlockSpec((per_step, H_B, DK_B, DV_B), lambda b: (b, 0, 0, 0))
    o, s_fin = pl.pallas_call(
        _gla_step_body,
        grid=(bd // per_step,),
        in_specs=[row(W_BK), row(W_BK), row(W_BK), row(W_BV), state],
        out_specs=[row(W_BV), state],
        out_shape=[jax.ShapeDtypeStruct((bd, 1, W_BV), F32), jax.ShapeDtypeStruct(s0.shape, F32)],
        compiler_params=_params("arbitrary"),
        name="gla_step",
    )(bq.reshape(bd, 1, W_BK), bk.reshape(bd, 1, W_BK), glog.reshape(bd, 1, W_BK),
      bv.reshape(bd, 1, W_BV), s0)
    return o.reshape(bd, W_BV), s_fin


def _head_rms(t, g, width):
    parts = []
    for lo in range(0, t.shape[1], width):
        parts.append(_rms(t[:, lo:lo + width], g[:, lo:lo + width]))
    return jnp.concatenate(parts, axis=1)


def _even_mix(x_ref, a_ref, b_ref, br_ref, ga_ref, gb_ref, w_ref, *, lam_init):
    a = _head_rms(a_ref[...], ga_ref[...], DV_A) * (1.0 - lam_init)
    br = br_ref[...]
    b = _head_rms(b_ref[...], gb_ref[...], DV_B) * (br * jax.nn.sigmoid(br))
    merged = jnp.concatenate([a, b], axis=1).astype(BF16)
    return x_ref[...] + _dot(merged, w_ref[...])


def even_output_ffn(prompt_rows, sample_rows, ga, gb, w, lam_init, g, weights, cast_next=None):
    return _ffn_call(functools.partial(_even_mix, lam_init=lam_init), "even_output_ffn",
                     prompt_rows, sample_rows, [ga, gb, w], g, weights, cast_next)


O_Q, O_K, O_V, O_O, O_G = 0, W_C, 2 * W_C, 3 * W_C, 4 * W_C
O_COLS = O_G + LANES


def _odd_gates(xn, w_ref, bias_ref):
    z = _dot(xn, w_ref[:, O_G:O_G + LANES]) + bias_ref[...]
    lane = lax.broadcasted_iota(jnp.int32, z.shape, 1)
    return jnp.where(lane < H_C, z, _log_sigmoid(z))


def _odd_in_step_body(x_ref, g_ref, w_ref, bias_ref, q_ref, k_ref, v_ref, o_ref, gt_ref):
    xn = _rms(x_ref[...], g_ref[...]).astype(BF16)
    proj = lambda lo, n: _dot(xn, w_ref[:, lo:lo + n])
    q_ref[...] = proj(O_Q, W_C).astype(BF16)
    k_ref[...] = (proj(O_K, W_C) * DH_C ** -0.5).astype(BF16)
    v_ref[...] = proj(O_V, W_C)
    o_ref[...] = proj(O_O, W_C)
    gt_ref[...] = _odd_gates(xn, w_ref, bias_ref)


def _odd_in_chunk_body(x_ref, g_ref, w_ref, wqt_ref, wvt_ref, bias_ref,
                       qt_ref, k_ref, vt_ref, o_ref, gt_ref, gtt_ref, *, chunk):
    xn = _rms(x_ref[...], g_ref[...]).astype(BF16)
    proj = lambda lo, n: _dot(xn, w_ref[:, lo:lo + n])
    qt = _dot_nt(wqt_ref[...], xn).astype(BF16)
    vt = _dot_nt(wvt_ref[...], xn).astype(BF16)
    for c in range(qt_ref.shape[1]):
        qt_ref[0, c] = qt[:, c * chunk:(c + 1) * chunk]
        vt_ref[0, c] = vt[:, c * chunk:(c + 1) * chunk]
    k_ref[...] = (proj(O_K, W_C) * DH_C ** -0.5).astype(BF16)
    o_ref[...] = proj(O_O, W_C)
    gates = _odd_gates(xn, w_ref, bias_ref)
    gt_ref[...] = gates
    gtt_ref[...] = gates.T[:2 * H_C, :]


def odd_inputs(x, g, w, bias, prompt_batch=None):
    m, d = x.shape
    tm = _row_tile(m)
    if prompt_batch is None:
        widths = [(W_C, BF16), (W_C, BF16), (W_C, F32), (W_C, F32), (LANES, F32)]
        return pl.pallas_call(
            _odd_in_step_body,
            grid=(m // tm,),
            in_specs=[_rows(tm, d), _resident((1, d)), _resident(w.shape), _resident(bias.shape)],
            out_specs=[_rows(tm, n) for n, _ in widths],
            out_shape=[jax.ShapeDtypeStruct((m, n), dt) for n, dt in widths],
            compiler_params=_params("arbitrary"),
            name="odd_inputs",
        )(x, g, w, bias)
    seq = m // prompt_batch
    chunk = min(MLSTM_CHUNK, seq)
    per_seq = seq // tm
    chunked = pl.BlockSpec((1, tm // chunk, W_C, chunk), lambda i: (i // per_seq, i % per_seq, 0, 0))
    chunked_shape = jax.ShapeDtypeStruct((prompt_batch, seq // chunk, W_C, chunk), BF16)
    wqt = w[:, O_Q:O_Q + W_C].T
    wvt = w[:, O_V:O_V + W_C].T
    return pl.pallas_call(
        functools.partial(_odd_in_chunk_body, chunk=chunk),
        grid=(m // tm,),
        in_specs=[_rows(tm, d), _resident((1, d)), _resident(w.shape), _resident(wqt.shape),
                  _resident(wvt.shape), _resident(bias.shape)],
        out_specs=[chunked, _rows(tm, W_C), chunked, _rows(tm, W_C), _rows(tm, LANES),
                   pl.BlockSpec((2 * H_C, tm), lambda i: (0, i))],
        out_shape=[chunked_shape, jax.ShapeDtypeStruct((m, W_C), BF16), chunked_shape,
                   jax.ShapeDtypeStruct((m, W_C), F32), jax.ShapeDtypeStruct((m, LANES), F32),
                   jax.ShapeDtypeStruct((2 * H_C, m), F32)],
        compiler_params=_params("arbitrary"),
        name="odd_inputs",
    )(x, g, w, wqt, wvt, bias)


def _mlstm_body(qt_ref, k_ref, vt_ref, gt_ref, gtt_ref, c0_ref, n0_ref, m0_ref,
                h_ref, cfin_ref, nfin_ref, mfin_ref, c_sc, n_sc, m_sc, *, chunk):
    n = pl.program_id(1)

    @pl.when(n == 0)
    def _():
        c_sc[...] = c0_ref[0]
        n_sc[...] = n0_ref[0]
        m_sc[...] = m0_ref[0]

    r = lax.broadcasted_iota(jnp.int32, (chunk, chunk), 0)
    c = lax.broadcasted_iota(jnp.int32, (chunk, chunk), 1)
    past = r <= c
    gates = gt_ref[...]
    gates_t = gtt_ref[...]
    cum_col = jnp.dot(jnp.where(r >= c, 1.0, 0.0), gates, precision=HIGHEST, preferred_element_type=F32)
    cum_row = jnp.dot(gates_t, jnp.where(past, 1.0, 0.0), precision=HIGHEST, preferred_element_type=F32)
    ones = jnp.ones((16, chunk), BF16)
    hs = range(H_C)
    sl = [slice(h * DH_C, (h + 1) * DH_C) for h in hs]
    m_t, dw, pw, wk, a, m_end = [], [], [], [], [], []
    for h in hs:
        i_row = gates_t[h:h + 1, :]
        b_row = cum_row[H_C + h:H_C + h + 1, :]
        b_last = b_row[:, chunk - 1:chunk]
        m_prev = m_sc[h:h + 1, 0:1]
        src_col = gates[:, h:h + 1] - cum_col[:, H_C + h:H_C + h + 1]
        dlog = jnp.where(past, src_col + b_row, -jnp.inf)
        prev_log = b_row + m_prev
        m_t.append(jnp.maximum(prev_log, jnp.max(dlog, axis=0, keepdims=True)))
        dw.append(jnp.exp(dlog - m_t[h]))
        pw.append(jnp.exp(prev_log - m_t[h]))
        m_end.append(m_t[h][:, chunk - 1:chunk])
        wk.append(jnp.exp(b_last - b_row + i_row - m_end[h]))
        a.append(jnp.exp(b_last + m_prev - m_end[h]))
    qt = [qt_ref[0, 0, sl[h], :] for h in hs]
    vt = [vt_ref[0, 0, sl[h], :] for h in hs]
    k = [k_ref[:, sl[h]] for h in hs]
    c_old = [c_sc[h] for h in hs]
    n_old = [n_sc[h:h + 1, :] for h in hs]
    qk = [_dot(k[h], qt[h]) for h in hs]
    cq = [_dot(c_old[h].astype(BF16), qt[h]) for h in hs]
    nq = [_dot(jnp.broadcast_to(n_old[h], (8, DH_C)).astype(BF16), qt[h])[0:1] for h in hs]
    intra = [_dot(jnp.concatenate([vt[h], ones], axis=0), (qk[h] * dw[h]).astype(BF16)) for h in hs]
    c_add = [_dot((vt[h].astype(F32) * wk[h]).astype(BF16), k[h]) for h in hs]
    n_add = [_dot(jnp.broadcast_to(wk[h], (8, chunk)).astype(BF16), k[h])[0:1] for h in hs]
    for h in hs:
        num = intra[h][:DH_C] + pw[h] * cq[h]
        den = intra[h][DH_C:DH_C + 1] + pw[h] * nq[h]
        h_t = num * (1.0 / jnp.maximum(jnp.abs(den), jnp.exp(-m_t[h])))
        h_ref[:, sl[h]] = h_t.T
        c_sc[h] = a[h] * c_old[h] + c_add[h]
        n_sc[h:h + 1, :] = a[h] * n_old[h] + n_add[h]
        m_sc[h:h + 1, :] = jnp.broadcast_to(m_end[h], (1, LANES))

    @pl.when(n == pl.num_programs(1) - 1)
    def _():
        cfin_ref[0] = c_sc[...]
        nfin_ref[0] = n_sc[...]
        mfin_ref[0] = m_sc[...]


def mlstm_prompt(qt, k, vt, gates, gates_t, c0, n0, m0):
    batch, nc, _, chunk = qt.shape
    m = k.shape[0]
    rows = lambda w: pl.BlockSpec((chunk, w), lambda b, n: (b * nc + n, 0))
    chunked = pl.BlockSpec((1, 1, W_C, chunk), lambda b, n: (b, n, 0, 0))
    st_c = pl.BlockSpec((1, H_C, DH_C, DH_C), lambda b, n: (b, 0, 0, 0))
    st_n = pl.BlockSpec((1, H_C, DH_C), lambda b, n: (b, 0, 0))
    st_m = pl.BlockSpec((1, H_C, LANES), lambda b, n: (b, 0, 0))
    return pl.pallas_call(
        functools.partial(_mlstm_body, chunk=chunk),
        grid=(batch, nc),
        in_specs=[chunked, rows(W_C), chunked, rows(LANES),
                  pl.BlockSpec((2 * H_C, chunk), lambda b, n: (0, b * nc + n)), st_c, st_n, st_m],
        out_specs=[rows(W_C), st_c, st_n, st_m],
        out_shape=[jax.ShapeDtypeStruct((m, W_C), F32), jax.ShapeDtypeStruct(c0.shape, F32),
                   jax.ShapeDtypeStruct(n0.shape, F32), jax.ShapeDtypeStruct(m0.shape, F32)],
        scratch_shapes=[pltpu.VMEM((H_C, DH_C, DH_C), F32), pltpu.VMEM((H_C, DH_C), F32),
                        pltpu.VMEM((H_C, LANES), F32)],
        compiler_params=_params("arbitrary", "arbitrary"),
        name="mlstm_prompt",
    )(qt, k, vt, gates, gates_t, c0, n0, m0)


def _mlstm_step_body(q_ref, k_ref, v_ref, gt_ref, c0_ref, n0_ref, m0_ref, h_ref, c_ref, n_ref, m_ref):
    for i in range(q_ref.shape[0]):
        gates = gt_ref[i]
        for h in range(H_C):
            sl = slice(h * DH_C, (h + 1) * DH_C)
            i_pre = gates[:, h:h + 1]
            logf = gates[:, H_C + h:H_C + h + 1]
            m_prev = m0_ref[i, h:h + 1, 0:1]
            prev_log = logf + m_prev
            m_t = jnp.maximum(prev_log, i_pre)
            dw = jnp.exp(i_pre - m_t)
            pw = jnp.exp(prev_log - m_t)
            q = q_ref[i, :, sl]
            k = k_ref[i, :, sl]
            v = v_ref[i, :, sl]
            qf = q.astype(F32)
            kf = k.astype(F32)
            c_old = c0_ref[i, h]
            n_old = n0_ref[i, h:h + 1, :]
            sw = jnp.sum(qf * kf, axis=1, keepdims=True) * dw
            cq = _dot_nt(jnp.broadcast_to(q, (8, DH_C)), c_old.astype(BF16))[0:1]
            num = sw * v + pw * cq
            den = sw + pw * jnp.sum(qf * n_old, axis=1, keepdims=True)
            h_ref[i, :, sl] = num / jnp.maximum(jnp.abs(den), jnp.exp(-m_t))
            c_ref[i, h] = pw * c_old + _column(dw * v) * kf
            n_ref[i, h:h + 1, :] = pw * n_old + dw * kf
            m_ref[i, h:h + 1, :] = jnp.broadcast_to(m_t, (1, LANES))


def mlstm_step(q, k, v, gates, c0, n0, m0):
    bd = q.shape[0]
    per_step = math.gcd(MLSTM_STEP_SEQS, bd)
    row = lambda w: pl.BlockSpec((per_step, 1, w), lambda b: (b, 0, 0))
    st_c = pl.BlockSpec((per_step, H_C, DH_C, DH_C), lambda b: (b, 0, 0, 0))
    st_n = pl.BlockSpec((per_step, H_C, DH_C), lambda b: (b, 0, 0))
    st_m = pl.BlockSpec((per_step, H_C, LANES), lambda b: (b, 0, 0))
    h, c_fin, n_fin, m_fin = pl.pallas_call(
        _mlstm_step_body,
        grid=(bd // per_step,),
        in_specs=[row(W_C), row(W_C), row(W_C), row(LANES), st_c, st_n, st_m],
        out_specs=[row(W_C), st_c, st_n, st_m],
        out_shape=[jax.ShapeDtypeStruct((bd, 1, W_C), F32), jax.ShapeDtypeStruct(c0.shape, F32),
                   jax.ShapeDtypeStruct(n0.shape, F32), jax.ShapeDtypeStruct(m0.shape, F32)],
        compiler_params=_params("arbitrary"),
        name="mlstm_step",
    )(q.reshape(bd, 1, W_C), k.reshape(bd, 1, W_C), v.reshape(bd, 1, W_C), gates.reshape(bd, 1, LANES),
      c0, n0, m0)
    return h.reshape(bd, W_C), c_fin, n_fin, m_fin


def _odd_mix(x_ref, hm_ref, og_ref, gn_ref, w_ref):
    gated = jax.nn.sigmoid(og_ref[...]) * _head_rms(hm_ref[...], gn_ref[...], DH_C)
    return x_ref[...] + _dot(gated.astype(BF16), w_ref[...])


def odd_output_ffn(prompt_rows, sample_rows, gn, w, g, weights, cast_next=None):
    return _ffn_call(_odd_mix, "odd_output_ffn", prompt_rows, sample_rows, [gn, w], g, weights, cast_next)


def _even_weights(w_in, w_out, qk_norm, gate_w2, gate_b, a_norm, b_norm):
    main = 3 * W_A + 2 * W_BK + W_BV
    w = jnp.concatenate([w_in[:, :main], w_in[:, main + GLA_RANK:], w_in[:, main:main + GLA_RANK],
                         jnp.zeros((w_in.shape[0], LANES - GLA_RANK), w_in.dtype)], axis=1).astype(BF16)
    w2 = jnp.concatenate([gate_w2, jnp.zeros((LANES - GLA_RANK, W_BK), gate_w2.dtype)], axis=0).astype(BF16)
    idx = jnp.arange(MXU_WIDTH) // DK_A
    seg = (idx[:, None] == idx[None, :]).astype(BF16)
    gq = jnp.tile(qk_norm[0], W_A // DK_A).reshape(1, W_A)
    gk = jnp.tile(qk_norm[1], W_A // DK_A).reshape(1, W_A)
    return dict(w=w, w2=w2, seg=seg, gq=gq, gk=gk, gb=gate_b.reshape(1, W_BK), w_out=w_out.astype(BF16),
                ga=a_norm.reshape(1, W_A), gbn=b_norm.reshape(1, W_BV))


def _odd_weights(w_in, w_out, gate_b, norm_g):
    w = jnp.concatenate([w_in, jnp.zeros((w_in.shape[0], LANES - 2 * H_C), w_in.dtype)], axis=1).astype(BF16)
    bias = jnp.concatenate([gate_b[0], gate_b[1], jnp.zeros((LANES - 2 * H_C,), gate_b.dtype)]).reshape(1, LANES)
    return dict(w=w, bias=bias, w_out=w_out.astype(BF16), g=norm_g.reshape(1, W_C))


def kernel(x_prompt, x_sample, cache_k, cache_v, state_gla, state_mlstm_C, state_mlstm_n, state_mlstm_m,
           page_table, norm_g, ffn_w_gate, ffn_w_up, ffn_w_down, even_w_in, even_w_out, a_qk_norm, a_lambda,
           a_head_norm, b_gate_w2, b_gate_bias, b_head_norm, odd_w_in, odd_w_out, c_gate_bias, c_head_norm):
    bp, seq, d = x_prompt.shape
    bd = x_sample.shape[0]
    depth = norm_g.shape[0]
    assert x_sample.shape[1] == 1
    yp = x_prompt.reshape(bp * seq, d)
    ys = x_sample.reshape(bd, d)
    n_even, n_pool, page = cache_k.shape[:3]
    ck = jnp.transpose(cache_k, (0, 1, 3, 4, 5, 2)).reshape(n_even, n_pool, W_A, page)
    cv = cache_v.reshape(n_even, n_pool, page * H_A, DV_A)
    slopes = jnp.asarray([2.0 ** (-8.0 * (h + 1) / H_A) for h in range(H_A)], F32)
    slope8 = jnp.repeat(slopes, 2).reshape(2 * H_A, 1)
    ffn32 = (ffn_w_gate, ffn_w_up, ffn_w_down)
    w_first = tuple(w[0, 0].astype(BF16) for w in ffn32)
    pos = jnp.arange(seq)[:, None]
    lane = jnp.arange(LANES)[None, :]
    posb = jnp.where(lane >= 6, 0, jnp.where(lane % 2 == 0, pos // 64 * 64, pos % 64)).astype(BF16)

    k_p, v_p, k_s, v_s, gla_p, gla_s = [], [], [], [], [], []
    cm_p, nm_p, mm_p, cm_s, nm_s, mm_s = [], [], [], [], [], []
    for li in range(depth):
        g = norm_g[li].reshape(3, 1, d)
        yp, ys, w_second = ffn_half(yp, ys, g[0], w_first, cast_next=(*ffn32, li, 1))
        cast_next = (*ffn32, li + 1, 0) if li + 1 < depth else None
        if li % 2 == 0:
            e = li // 2
            lam_init = 0.8 - 0.6 * math.exp(-0.3 * li)
            ew = _even_weights(even_w_in[e], even_w_out[e], a_qk_norm[e], b_gate_w2[e], b_gate_bias[e],
                               a_head_norm[e], b_head_norm[e])
            lam_vecs = a_lambda[e]
            aqt, akb, akt, av, avt, bq, bk, bv, gl, br = even_inputs(
                yp, g[1], ew["w"], ew["seg"], ew["gq"], ew["gk"], ew["w2"], ew["gb"], prompt_batch=bp)
            a_out = diff_attn_prompt(aqt, akb, avt, posb, slopes, lam_vecs, lam_init)
            b_out, s_fin = gla_prompt(bq, bk, gl, bv, jnp.zeros((bp, W_BK, DV_B), F32), bp)
            prompt_rows = [yp, a_out, b_out, br]
            k_p.append(jnp.transpose(akt.reshape(bp, H_A, 2, DK_A, seq), (0, 4, 1, 2, 3)))
            v_p.append(av.reshape(bp, seq, H_A, DV_A))
            gla_p.append(s_fin.reshape(bp, H_B, DK_B, DV_B))
            aq, ak, av, bq, bk, bv, gl, br = even_inputs(
                ys, g[1], ew["w"], ew["seg"], ew["gq"], ew["gk"], ew["w2"], ew["gb"])
            a_out = diff_attn_decode(page_table, aq, ak, av, slope8, lam_vecs, lam_init, ck, cv, e)
            b_out, s_fin = gla_step(bq, bk, gl, bv, state_gla[e])
            k_s.append(ak.reshape(bd, 1, H_A, 2, DK_A))
            v_s.append(av.reshape(bd, 1, H_A, DV_A))
            gla_s.append(s_fin)
            yp, ys, w_first = even_output_ffn(prompt_rows, [ys, a_out, b_out, br], ew["ga"], ew["gbn"],
                                              ew["w_out"], lam_init, g[2], w_second, cast_next)
        else:
            o = li // 2
            ow = _odd_weights(odd_w_in[o], odd_w_out[o], c_gate_bias[o], c_head_norm[o])
            qt, k, vt, og, gates, gates_t = odd_inputs(yp, g[1], ow["w"], ow["bias"], prompt_batch=bp)
            h, c_fin, n_fin, m_fin = mlstm_prompt(
                qt, k, vt, gates, gates_t, jnp.zeros((bp, H_C, DH_C, DH_C), F32),
                jnp.zeros((bp, H_C, DH_C), F32), jnp.zeros((bp, H_C, LANES), F32))
            prompt_rows = [yp, h, og]
            cm_p.append(c_fin)
            nm_p.append(n_fin)
            mm_p.append(m_fin[:, :, 0])
            q, k, v, og, gates = odd_inputs(ys, g[1], ow["w"], ow["bias"])
            m0 = jnp.broadcast_to(state_mlstm_m[o][:, :, None], (bd, H_C, LANES))
            h, c_fin, n_fin, m_fin = mlstm_step(q, k, v, gates, state_mlstm_C[o], state_mlstm_n[o], m0)
            cm_s.append(c_fin)
            nm_s.append(n_fin)
            mm_s.append(m_fin[:, :, 0])
            yp, ys, w_first = odd_output_ffn(prompt_rows, [ys, h, og], ow["g"], ow["w_out"], g[2], w_second,
                                             cast_next)
    return (yp.reshape(bp, seq, d), ys.reshape(bd, 1, d),
            jnp.stack(k_p), jnp.stack(v_p), jnp.stack(k_s), jnp.stack(v_s),
            jnp.stack(gla_p), jnp.stack(gla_s),
            jnp.stack(cm_p), jnp.stack(nm_p), jnp.stack(mm_p),
            jnp.stack(cm_s), jnp.stack(nm_s), jnp.stack(mm_s))
```

```python
import functools
import math

import jax
import jax.numpy as jnp
from jax import lax
from jax.experimental import pallas as pl
from jax.experimental.pallas import tpu as pltpu

F32 = jnp.float32
BF16 = jnp.bfloat16
EPS = 1e-6
HIGHEST = lax.Precision.HIGHEST
LOG2E = math.log2(math.e)

LANES = 128
MXU_WIDTH = 256
ONES_ROWS = 16
POS_SPLIT = 64
VMEM_LIMIT_BYTES = 56 * 2**20

H_A, DK_A, DV_A = 4, 64, 128
H_B, DK_B, DV_B = 4, 64, 128
GLA_RANK = 16
GLA_NORMALIZER = 16.0
H_C, DH_C = 4, 256
W_A = H_A * 2 * DK_A
W_BK = H_B * DK_B
W_BV = H_B * DV_B
W_C = H_C * DH_C

ROW_TILE = 512
ATTN_TILE = 256
ATTN_KV_CHUNK = 1024
ATTN_HEADS_PER_STEP = 4
GLA_CHUNK = 128
MLSTM_CHUNK = 256
DECODE_PAGES = 32
GLA_STEP_SEQS = 8
MLSTM_STEP_SEQS = 2

NT_DIMS = (((1,), (1,)), ((), ()))


def _params(*semantics):
    return pltpu.CompilerParams(dimension_semantics=semantics, vmem_limit_bytes=VMEM_LIMIT_BYTES)


def _resident(shape):
    return pl.BlockSpec(shape, lambda *_: (0,) * len(shape), pipeline_mode=pl.Buffered(1))


def _rows(tm, width):
    return pl.BlockSpec((tm, width), lambda i: (i, 0))


def _row_tile(m):
    return ROW_TILE if m % ROW_TILE == 0 else m


def _rms(x, g):
    return x * lax.rsqrt(jnp.mean(x * x, axis=-1, keepdims=True) + EPS) * g


def _log_sigmoid(x):
    return jnp.minimum(x, 0.0) - jnp.log1p(jnp.exp(-jnp.abs(x)))


def _dot(a, b):
    return jnp.dot(a, b, preferred_element_type=F32)


def _dot_nt(a, b):
    return lax.dot_general(a, b, NT_DIMS, preferred_element_type=F32)


def _diff_lambda(lam_vecs, lam_init):
    p1 = jnp.sum(lam_vecs[0:1] * lam_vecs[1:2], axis=-1, keepdims=True)
    p2 = jnp.sum(lam_vecs[2:3] * lam_vecs[3:4], axis=-1, keepdims=True)
    return jnp.exp(p1) - jnp.exp(p2) + lam_init


FF_CHUNK = 256


def _swiglu_half(x, g_ref, wg_ref, wu_ref, wd_ref, h_ref):
    xn = _rms(x, g_ref[...]).astype(BF16)
    for c in range(0, wg_ref.shape[1], FF_CHUNK):
        hg = _dot(xn, wg_ref[:, c:c + FF_CHUNK])
        hu = _dot(xn, wu_ref[:, c:c + FF_CHUNK])
        h_ref[:, c:c + FF_CHUNK] = (hg * jax.nn.sigmoid(hg) * hu).astype(BF16)
    return x + 0.5 * _dot(h_ref[...], wd_ref[...])


def _ffn_kernel(*refs, prologue, n_rows, n_small, steps):
    rows, samples = refs[:n_rows], refs[n_rows:2 * n_rows]
    small = refs[2 * n_rows:2 * n_rows + n_small]
    g_ref, wg_ref, wu_ref, wd_ref = refs[2 * n_rows + n_small:2 * n_rows + n_small + 4]
    rest = refs[2 * n_rows + n_small + 4:]
    n_cast = (len(rest) - 3) // 2
    src, o_ref, so_ref = rest[:n_cast], rest[n_cast], rest[n_cast + 1]
    dst, h_ref = rest[n_cast + 2:2 * n_cast + 2], rest[2 * n_cast + 2]
    i = pl.program_id(0)

    @pl.when(i < steps)
    def _():
        o_ref[...] = _swiglu_half(prologue(*rows, *small), g_ref, wg_ref, wu_ref, wd_ref, h_ref)
        for s_ref, d_ref in zip(src, dst):
            d_ref[...] = s_ref[...].astype(BF16)

    @pl.when(i == steps)
    def _():
        so_ref[...] = _swiglu_half(prologue(*samples, *small), g_ref, wg_ref, wu_ref, wd_ref,
                                   h_ref.at[pl.ds(0, so_ref.shape[0])])


def _ffn_call(prologue, name, row_inputs, sample_inputs, small_inputs, g, weights, cast_next=None):
    wg, wu, wd = weights
    m, d = row_inputs[0].shape
    ms = sample_inputs[0].shape[0]
    dff = wg.shape[-1]
    tm = _row_tile(m)
    steps = m // tm
    assert ms <= tm
    tile = lambda i: jnp.minimum(i, steps - 1)
    in_specs = ([pl.BlockSpec((tm, a.shape[1]), lambda i: (tile(i), 0)) for a in row_inputs]
                + [_resident(a.shape) for a in sample_inputs] + [_resident(a.shape) for a in small_inputs]
                + [_resident((1, d)), _resident((d, dff)), _resident((d, dff)), _resident((dff, d))])
    out_specs = [pl.BlockSpec((tm, d), lambda i: (tile(i), 0)), pl.BlockSpec((ms, d), lambda i: (0, 0))]
    out_shape = [jax.ShapeDtypeStruct((m, d), F32), jax.ShapeDtypeStruct((ms, d), F32)]
    operands = [*row_inputs, *sample_inputs, *small_inputs, g, wg, wu, wd]
    if cast_next is not None:
        *next32, layer, half = cast_next
        up_rows, down_rows, down_div = d // steps, dff // (steps // 2), 2
        assert up_rows * steps == d and down_rows * (steps // down_div) == dff
        assert up_rows % 16 == 0 and down_rows % 16 == 0
        in_specs += [pl.BlockSpec((None, None, up_rows, dff), lambda i: (layer, half, tile(i), 0)),
                     pl.BlockSpec((None, None, up_rows, dff), lambda i: (layer, half, tile(i), 0)),
                     pl.BlockSpec((None, None, down_rows, d), lambda i: (layer, half, tile(i) // down_div, 0))]
        out_specs += [pl.BlockSpec((up_rows, dff), lambda i: (tile(i), 0)),
                      pl.BlockSpec((up_rows, dff), lambda i: (tile(i), 0)),
                      pl.BlockSpec((down_rows, d), lambda i: (tile(i) // down_div, 0))]
        out_shape += [jax.ShapeDtypeStruct((d, dff), BF16), jax.ShapeDtypeStruct((d, dff), BF16),
                      jax.ShapeDtypeStruct((dff, d), BF16)]
        operands += next32
    out = pl.pallas_call(
        functools.partial(_ffn_kernel, prologue=prologue, n_rows=len(row_inputs), n_small=len(small_inputs),
                          steps=steps),
        grid=(steps + 1,),
        in_specs=in_specs,
        out_specs=out_specs,
        out_shape=out_shape,
        scratch_shapes=[pltpu.VMEM((tm, dff), BF16)],
        compiler_params=_params("arbitrary"),
        name=name,
    )(*operands)
    return out[0], out[1], tuple(out[2:])


def ffn_half(x, x_sample, g, weights, cast_next=None):
    return _ffn_call(lambda x_ref: x_ref[...], "ffn_half", [x], [x_sample], [], g, weights, cast_next)


E_AQ, E_AK, E_AV = 0, W_A, 2 * W_A
E_BQ = 3 * W_A
E_BK = E_BQ + W_BK
E_BV = E_BK + W_BK
E_MAIN = E_BV + W_BV


def _even_in_body(x_ref, g_ref, w_ref, wbr_ref, wbg_ref, seg_ref, gq_ref, gk_ref, w2_ref, gb_ref, *out_refs,
                  attn_tile):
    xn = _rms(x_ref[...], g_ref[...]).astype(BF16)
    proj = lambda lo, n: _dot(xn, w_ref[:, lo:lo + n])

    def group_rms(t, g):
        t2 = (t * t).astype(BF16)
        ms = jnp.concatenate([_dot(t2[:, c:c + MXU_WIDTH], seg_ref[...])
                              for c in range(0, t.shape[1], MXU_WIDTH)], axis=1)
        return t * lax.rsqrt(ms * (1.0 / DK_A) + EPS) * g

    aq = group_rms(proj(E_AQ, W_A), gq_ref[...]) * DK_A ** -0.5
    ak = group_rms(proj(E_AK, W_A), gk_ref[...])
    av = proj(E_AV, W_A)
    if attn_tile is None:
        aq_ref, ak_ref, av_ref, bq_ref, bk_ref, bv_ref, gl_ref, br_ref = out_refs
        aq_ref[...] = aq.astype(BF16)
        ak_ref[...] = ak
        av_ref[...] = av
    else:
        aqt_ref, akb_ref, akt_ref, av_ref, avt_ref, bq_ref, bk_ref, bv_ref, gl_ref, br_ref = out_refs
        aqt = (aq * LOG2E).T.astype(BF16)
        avt = av.T.astype(BF16)
        for c in range(aqt_ref.shape[1]):
            aqt_ref[0, c] = aqt[:, c * attn_tile:(c + 1) * attn_tile]
            avt_ref[0, c] = avt[:, c * attn_tile:(c + 1) * attn_tile]
        akb_ref[...] = ak.astype(BF16)
        akt_ref[0] = ak.T
        for h in range(H_A):
            av_ref[pl.ds(h, av.shape[0], stride=H_A), :] = av[:, h * DV_A:(h + 1) * DV_A]
    bq_ref[...] = proj(E_BQ, W_BK) * DK_B ** -0.5
    bk_ref[...] = proj(E_BK, W_BK)
    bv_ref[...] = proj(E_BV, W_BV)
    br_ref[...] = _dot(xn, wbr_ref[...])
    bg = _dot(xn, wbg_ref[...]).astype(BF16)
    gl_ref[...] = _log_sigmoid(_dot(bg, w2_ref[...]) + gb_ref[...]) * (1.0 / GLA_NORMALIZER)


def even_inputs(x, g, w, wbr, wbg, seg, gq, gk, w2, gb, prompt_batch=None):
    m, d = x.shape
    tm = _row_tile(m)
    common = [(W_BK, F32), (W_BK, F32), (W_BV, F32), (W_BK, F32), (W_BV, F32)]
    if prompt_batch is None:
        attn_tile = None
        widths = [(W_A, BF16), (W_A, F32), (W_A, F32)] + common
        out_specs = [_rows(tm, n) for n, _ in widths]
        out_shape = [jax.ShapeDtypeStruct((m, n), dt) for n, dt in widths]
    else:
        seq = m // prompt_batch
        attn_tile = min(ATTN_TILE, seq)
        per_seq = seq // tm
        per_tile = tm // attn_tile
        chunked = pl.BlockSpec((1, per_tile, W_A, attn_tile), lambda i: (i // per_seq, i % per_seq, 0, 0))
        chunked_shape = jax.ShapeDtypeStruct((prompt_batch, seq // attn_tile, W_A, attn_tile), BF16)
        out_specs = ([chunked, _rows(tm, W_A),
                      pl.BlockSpec((1, W_A, tm), lambda i: (i // per_seq, 0, i % per_seq)),
                      _rows(tm * H_A, DV_A), chunked] + [_rows(tm, n) for n, _ in common])
        out_shape = ([chunked_shape, jax.ShapeDtypeStruct((m, W_A), BF16),
                      jax.ShapeDtypeStruct((prompt_batch, W_A, seq), F32),
                      jax.ShapeDtypeStruct((m * H_A, DV_A), F32), chunked_shape]
                     + [jax.ShapeDtypeStruct((m, n), dt) for n, dt in common])
    return pl.pallas_call(
        functools.partial(_even_in_body, attn_tile=attn_tile),
        grid=(m // tm,),
        in_specs=[_rows(tm, d), _resident((1, d)), _resident(w.shape), _resident(wbr.shape), _resident(wbg.shape),
                  _resident(seg.shape), _resident(gq.shape), _resident(gk.shape), _resident(w2.shape),
                  _resident(gb.shape)],
        out_specs=out_specs,
        out_shape=out_shape,
        compiler_params=_params("arbitrary"),
        name="even_inputs",
    )(x, g, w, wbr, wbg, seg, gq, gk, w2, gb)


def _attn_body(slope_ref, lam_ref, qt_ref, k_ref, posb_ref, vt_ref, o_ref, m_ref, acc_ref, *, tile, kv, lam_init):
    qi = pl.program_id(2)
    heads = m_ref.shape[0] // 2
    row = lax.broadcasted_iota(jnp.int32, (DV_A, tile), 0)
    erow = lax.broadcasted_iota(jnp.int32, (LANES, tile), 0)
    rhs = []
    for hh in range(heads):
        slope = slope_ref[pl.program_id(1) * heads + hh]
        qt = qt_ref[0, 0, hh * DV_A:(hh + 1) * DV_A, :]
        c_all = jnp.full((LANES, tile), slope * LOG2E, F32)
        c_1 = c_all.astype(BF16).astype(F32)
        c_2 = (c_all - c_1).astype(BF16).astype(F32)
        c_3 = (c_all - c_1 - c_2).astype(BF16).astype(F32)
        ext = jnp.where(erow < 2, c_1, jnp.where(erow < 4, c_2, jnp.where(erow < 6, c_3, 0.0))).astype(BF16)
        for mm in range(2):
            own = (row >= mm * DK_A) & (row < (mm + 1) * DK_A)
            rhs.append(jnp.concatenate([jnp.where(own, qt, jnp.zeros_like(qt)), ext], axis=0))
    m_ref[...] = jnp.full(m_ref.shape, -jnp.inf, F32)
    acc_ref[...] = jnp.zeros(acc_ref.shape, F32)
    per_kv = kv // tile

    def step(j, masked, keys):
        rows = pl.ds(pl.multiple_of(j * kv, kv), keys)
        ones = jnp.ones((ONES_ROWS, keys), BF16)
        posb = posb_ref[rows, :]
        scores = []
        for hh in range(heads):
            kx = jnp.concatenate([k_ref[rows, hh * DV_A:(hh + 1) * DV_A], posb], axis=1)
            scores += [_dot(kx, rhs[2 * hh]), _dot(kx, rhs[2 * hh + 1])]
        for st, s in enumerate(scores):
            hs = slice(st // 2 * DV_A, (st // 2 + 1) * DV_A)
            if masked:
                kk = lax.broadcasted_iota(jnp.int32, s.shape, 0) + (j * kv - qi * tile)
                qq = lax.broadcasted_iota(jnp.int32, s.shape, 1)
                s = jnp.where(kk <= qq, s, -jnp.inf)
            m_old = m_ref[st]
            m_new = jnp.maximum(m_old, jnp.max(s, axis=0, keepdims=True))
            p = jnp.exp2(s - m_new).astype(BF16)
            vt = jnp.concatenate([vt_ref[0, j * per_kv + c, hs, :] for c in range(keys // tile)], axis=1)
            vx = jnp.concatenate([vt, ones], axis=0)
            acc_ref[st] = jnp.exp2(m_old - m_new) * acc_ref[st] + _dot(vx, p)
            m_ref[st] = m_new

    def body(j, carry):
        step(j, False, kv)
        return carry

    n_full = (qi * tile) // kv
    lax.fori_loop(0, n_full, body, 0)
    for r in range(per_kv):
        pl.when(qi % per_kv == r)(functools.partial(step, n_full, True, (r + 1) * tile))
    lam = _diff_lambda(lam_ref[...], lam_init)
    for hh in range(heads):
        acc0 = acc_ref[2 * hh]
        acc1 = acc_ref[2 * hh + 1]
        out0 = acc0[:DV_A] * (1.0 / acc0[DV_A:DV_A + 1])
        out1 = acc1[:DV_A] * (1.0 / acc1[DV_A:DV_A + 1])
        o_ref[:, hh * DV_A:(hh + 1) * DV_A] = (out0 - lam * out1).T


def diff_attn_prompt(aqt, akb, avt, posb, slopes, lam_vecs, lam_init):
    batch, nq, _, tile = aqt.shape
    seq = nq * tile
    kv = min(ATTN_KV_CHUNK, seq)
    heads = ATTN_HEADS_PER_STEP
    wide = heads * DV_A
    return pl.pallas_call(
        functools.partial(_attn_body, tile=tile, kv=kv, lam_init=lam_init),
        grid=(batch, H_A // heads, nq),
        in_specs=[pl.BlockSpec(memory_space=pltpu.SMEM),
                  pl.BlockSpec(lam_vecs.shape, lambda b, h, i: (0, 0)),
                  pl.BlockSpec((1, 1, wide, tile), lambda b, h, i: (b, i, h, 0)),
                  pl.BlockSpec((seq, wide), lambda b, h, i: (b, h)),
                  pl.BlockSpec((seq, LANES), lambda b, h, i: (0, 0)),
                  pl.BlockSpec((1, nq, wide, tile), lambda b, h, i: (b, 0, h, 0))],
        out_specs=pl.BlockSpec((tile, wide), lambda b, h, i: (b * nq + i, h)),
        out_shape=jax.ShapeDtypeStruct((batch * seq, W_A), F32),
        scratch_shapes=[pltpu.VMEM((2 * heads, 1, tile), F32), pltpu.VMEM((2 * heads, DV_A + ONES_ROWS, tile), F32)],
        compiler_params=_params("arbitrary", "arbitrary", "arbitrary"),
        name="diff_attn_prompt",
    )(slopes, lam_vecs, aqt, akb, posb, avt)


def _decode_body(pt_ref, q_ref, kn_ref, vn_ref, slope_ref, lam_ref, *rest, pages, past, lam_init):
    kt_refs = rest[:pages]
    v_refs = rest[pages:2 * pages]
    o_ref, m_ref, l_ref, acc_ref = rest[2 * pages:]
    j = pl.program_id(1)
    page = kt_refs[0].shape[3]
    groups = 2 * H_A

    @pl.when(j == 0)
    def _():
        m_ref[...] = jnp.full(m_ref.shape, -jnp.inf, F32)
        l_ref[...] = jnp.zeros(l_ref.shape, F32)
        acc_ref[...] = jnp.zeros(acc_ref.shape, F32)

    qrow = q_ref[0].astype(F32)
    lane = lax.broadcasted_iota(jnp.int32, (groups, W_A), 1)
    r8 = lax.broadcasted_iota(jnp.int32, (groups, W_A), 0)
    q8 = jnp.where(lane // DK_A == r8, jnp.broadcast_to(qrow, (groups, W_A)), 0.0)
    q8_bf = q8.astype(BF16)
    slope = slope_ref[...]
    col = lax.broadcasted_iota(jnp.int32, (1, page), 1)
    row_head = lax.broadcasted_iota(jnp.int32, (groups, 1), 0) // 2

    s_parts = []
    for i in range(pages):
        kpos = (col + (j * pages + i) * page).astype(F32)
        s_parts.append(_dot(q8_bf, kt_refs[i][0, 0].astype(BF16)) + slope * kpos)
    s = jnp.concatenate(s_parts, axis=1)
    m_old = m_ref[...]
    m_new = jnp.maximum(m_old, jnp.max(s, axis=1, keepdims=True))
    p = jnp.exp(s - m_new).astype(BF16)
    alpha = jnp.exp(m_old - m_new)
    pv = jnp.zeros(acc_ref.shape, F32)
    for i in range(pages):
        p_i = p[:, i * page:(i + 1) * page]
        for h in range(H_A):
            v_h = v_refs[i][0, 0, pl.ds(h, page, stride=H_A), :].astype(BF16)
            pv = pv + jnp.where(row_head == h, _dot(p_i, v_h), 0.0)
    l_ref[...] = alpha * l_ref[...] + jnp.sum(p.astype(F32), axis=1, keepdims=True)
    acc_ref[...] = alpha * acc_ref[...] + pv
    m_ref[...] = m_new

    @pl.when(j == pl.num_programs(1) - 1)
    def _():
        s_new = jnp.sum(q8 * kn_ref[0], axis=1, keepdims=True) + slope * float(past)
        v_new = jnp.zeros(acc_ref.shape, F32)
        for h in range(H_A):
            v_row = jnp.broadcast_to(vn_ref[0, :, h * DV_A:(h + 1) * DV_A], acc_ref.shape)
            v_new = jnp.where(row_head == h, v_row, v_new)
        m_old = m_ref[...]
        m_fin = jnp.maximum(m_old, s_new)
        p_new = jnp.exp(s_new - m_fin)
        alpha = jnp.exp(m_old - m_fin)
        l_fin = alpha * l_ref[...] + p_new
        o8 = (alpha * acc_ref[...] + p_new * v_new) / l_fin
        lam = _diff_lambda(lam_ref[...], lam_init)
        for h in range(H_A):
            o_ref[0, :, h * DV_A:(h + 1) * DV_A] = o8[2 * h:2 * h + 1] - lam * o8[2 * h + 1:2 * h + 2]


def diff_attn_decode(page_table, aq, ak, av, slope8, lam_vecs, lam_init, cache_kt, cache_v, layer):
    bd = aq.shape[0]
    n_pages = page_table.shape[1]
    page = cache_kt.shape[3]
    pages = math.gcd(DECODE_PAGES, n_pages)
    past = n_pages * page

    def page_spec(i, shape):
        return pl.BlockSpec((1, 1) + shape, lambda b, j, pt: (layer, pt[b * n_pages + j * pages + i], 0, 0))

    row3 = lambda: pl.BlockSpec((1, 1, W_A), lambda b, j, pt: (b, 0, 0))
    const = lambda a: pl.BlockSpec(a.shape, lambda b, j, pt: (0,) * a.ndim)
    grid_spec = pltpu.PrefetchScalarGridSpec(
        num_scalar_prefetch=1,
        grid=(bd, n_pages // pages),
        in_specs=[row3(), row3(), row3(), const(slope8), const(lam_vecs)]
        + [page_spec(i, (W_A, page)) for i in range(pages)]
        + [page_spec(i, (page * H_A, DV_A)) for i in range(pages)],
        out_specs=row3(),
        scratch_shapes=[pltpu.VMEM((2 * H_A, 1), F32), pltpu.VMEM((2 * H_A, 1), F32),
                        pltpu.VMEM((2 * H_A, DV_A), F32)],
    )
    out = pl.pallas_call(
        functools.partial(_decode_body, pages=pages, past=past, lam_init=lam_init),
        grid_spec=grid_spec,
        out_shape=jax.ShapeDtypeStruct((bd, 1, W_A), F32),
        compiler_params=_params("arbitrary", "arbitrary"),
        name="diff_attn_decode",
    )(page_table.reshape(-1), aq.reshape(bd, 1, W_A), ak.reshape(bd, 1, W_A), av.reshape(bd, 1, W_A),
      slope8, lam_vecs, *([cache_kt] * pages), *([cache_v] * pages))
    return out.reshape(bd, W_A)


def _gla_body(q_ref, k_ref, g_ref, v_ref, s0_ref, o_ref, sfin_ref, st_ref, *, chunk):
    n = pl.program_id(0)
    nb = q_ref.shape[0]
    pair = 2 * DK_B

    @pl.when(n == 0)
    def _():
        for b in range(nb):
            for p in range(H_B // 2):
                st_ref[b, p] = s0_ref[b, p * pair:(p + 1) * pair, :].T

    r = lax.broadcasted_iota(jnp.int32, (chunk, chunk), 0)
    c = lax.broadcasted_iota(jnp.int32, (chunk, chunk), 1)
    causal = r >= c
    tri = jnp.where(causal, 1.0, 0.0)
    lane_head = lax.broadcasted_iota(jnp.int32, (1, pair), 1) // DK_B
    mid = chunk // 2
    zero = jnp.zeros((), BF16)
    heads = [(b, h) for b in range(nb) for h in range(H_B)]
    lanes_of = lambda h: slice(h // 2 * pair, (h // 2 + 1) * pair)
    vals_of = lambda h: slice(h * DV_B, (h + 1) * DV_B)
    own_of = lambda h: lane_head == h % 2
    cums = [jnp.dot(tri, g_ref[b], precision=HIGHEST, preferred_element_type=F32) for b in range(nb)]
    q_intra, k_intra, q_inter, k_end, decay = [], [], [], [], []
    for b in range(nb):
        cum = cums[b]
        c_mid = cum[mid - 1:mid, :]
        c_last = cum[chunk - 1:chunk, :]
        q = q_ref[b]
        k = k_ref[b]
        q_intra.append((q * jnp.exp(cum - c_mid)).astype(BF16))
        k_intra.append((k * jnp.exp(c_mid - cum)).astype(BF16))
        q_inter.append((q * jnp.exp(cum)).astype(BF16))
        k_end.append((k * jnp.exp(c_last - cum)).astype(BF16))
        decay.append(jnp.exp(c_last))
    scores = [_dot_nt(jnp.where(own_of(h), q_intra[b][:, lanes_of(h)], zero), k_intra[b][:, lanes_of(h)])
              for b, h in heads]
    states = [[st_ref[b, p] for p in range(H_B // 2)] for b in range(nb)]
    carried = [_dot_nt(jnp.where(own_of(h), q_inter[b][:, lanes_of(h)], zero), states[b][h // 2].astype(BF16))
               for b, h in heads]
    updates = [_dot(v_ref[b, :, vals_of(h)].T.astype(BF16), jnp.where(own_of(h), k_end[b][:, lanes_of(h)], zero))
               for b, h in heads]
    for i, (b, h) in enumerate(heads):
        a = jnp.where(causal, scores[i], 0.0).astype(BF16)
        o_ref[b, :, vals_of(h)] = _dot(a, v_ref[b, :, vals_of(h)].astype(BF16)) + carried[i]
    for b in range(nb):
        for p in range(H_B // 2):
            st_ref[b, p] = (states[b][p] * decay[b][:, p * pair:(p + 1) * pair]
                            + updates[b * H_B + 2 * p] + updates[b * H_B + 2 * p + 1])

    @pl.when(n == pl.num_programs(0) - 1)
    def _():
        for b in range(nb):
            for p in range(H_B // 2):
                sfin_ref[b, p * pair:(p + 1) * pair, :] = st_ref[b, p].T


def gla_prompt(bq, bk, glog, bv, s0, batch):
    m = bq.shape[0]
    seq = m // batch
    chunk = min(GLA_CHUNK, seq)
    blk = lambda w: pl.BlockSpec((batch, chunk, w), lambda n: (0, n, 0))
    state = pl.BlockSpec((batch, W_BK, DV_B), lambda n: (0, 0, 0))
    o, s_fin = pl.pallas_call(
        functools.partial(_gla_body, chunk=chunk),
        grid=(seq // chunk,),
        in_specs=[blk(W_BK), blk(W_BK), blk(W_BK), blk(W_BV), state],
        out_specs=[blk(W_BV), state],
        out_shape=[jax.ShapeDtypeStruct((batch, seq, W_BV), F32),
                   jax.ShapeDtypeStruct((batch, W_BK, DV_B), F32)],
        scratch_shapes=[pltpu.VMEM((batch, H_B // 2, DV_B, 2 * DK_B), F32)],
        compiler_params=_params("arbitrary"),
        name="gla_prompt",
    )(bq.reshape(batch, seq, W_BK), bk.reshape(batch, seq, W_BK), glog.reshape(batch, seq, W_BK),
      bv.reshape(batch, seq, W_BV), s0)
    return o.reshape(m, W_BV), s_fin


def _column(row):
    n = row.shape[1]
    r = lax.broadcasted_iota(jnp.int32, (n, n), 0)
    c = lax.broadcasted_iota(jnp.int32, (n, n), 1)
    return jnp.sum(jnp.where(r == c, jnp.broadcast_to(row, (n, n)), 0.0), axis=1, keepdims=True)


def _gla_step_body(q_ref, k_ref, g_ref, v_ref, s0_ref, o_ref, s_ref):
    pair = 2 * DK_B
    for i in range(q_ref.shape[0]):
        for p in range(H_B // 2):
            sl = slice(p * pair, (p + 1) * pair)
            qc = _column(q_ref[i, :, sl])
            kc = _column(k_ref[i, :, sl])
            dc = jnp.exp(_column(g_ref[i, :, sl]))
            for hh in range(2):
                h = 2 * p + hh
                rs = slice(hh * DK_B, (hh + 1) * DK_B)
                v = v_ref[i, :, h * DV_B:(h + 1) * DV_B]
                s_new = dc[rs] * s0_ref[i, h] + kc[rs] * v
                s_ref[i, h] = s_new
                o_ref[i, :, h * DV_B:(h + 1) * DV_B] = jnp.sum(qc[rs] * s_new, axis=0, keepdims=True)


def gla_step(bq, bk, glog, bv, s0):
    bd = bq.shape[0]
    per_step = math.gcd(GLA_STEP_SEQS, bd)
    row = lambda w: pl.BlockSpec((per_step, 1, w), lambda b: (b, 0, 0))
    state = pl.BlockSpec((per_step, H_B, DK_B, DV_B), lambda b: (b, 0, 0, 0))
    o, s_fin = pl.pallas_call(
        _gla_step_body,
        grid=(bd // per_step,),
        in_specs=[row(W_BK), row(W_BK), row(W_BK), row(W_BV), state],
        out_specs=[row(W_BV), state],
        out_shape=[jax.ShapeDtypeStruct((bd, 1, W_BV), F32), jax.ShapeDtypeStruct(s0.shape, F32)],
        compiler_params=_params("arbitrary"),
        name="gla_step",
    )(bq.reshape(bd, 1, W_BK), bk.reshape(bd, 1, W_BK), glog.reshape(bd, 1, W_BK),
      bv.reshape(bd, 1, W_BV), s0)
    return o.reshape(bd, W_BV), s_fin


def _head_rms(t, g, width):
    parts = []
    for lo in range(0, t.shape[1], width):
        parts.append(_rms(t[:, lo:lo + width], g[:, lo:lo + width]))
    return jnp.concatenate(parts, axis=1)


def _even_mix(x_ref, a_ref, b_ref, br_ref, ga_ref, gb_ref, w_ref, *, lam_init):
    a = _head_rms(a_ref[...], ga_ref[...], DV_A) * (1.0 - lam_init)
    br = br_ref[...]
    b = _head_rms(b_ref[...], gb_ref[...], DV_B) * (br * jax.nn.sigmoid(br))
    merged = jnp.concatenate([a, b], axis=1).astype(BF16)
    return x_ref[...] + _dot(merged, w_ref[...])


def even_output_ffn(prompt_rows, sample_rows, ga, gb, w, lam_init, g, weights, cast_next=None):
    return _ffn_call(functools.partial(_even_mix, lam_init=lam_init), "even_output_ffn",
                     prompt_rows, sample_rows, [ga, gb, w], g, weights, cast_next)


O_Q, O_K, O_V, O_O, O_G = 0, W_C, 2 * W_C, 3 * W_C, 4 * W_C


def _odd_gates(xn, wgate_ref, bias_ref):
    z = _dot(xn, wgate_ref[...]) + bias_ref[...]
    lane = lax.broadcasted_iota(jnp.int32, z.shape, 1)
    return jnp.where(lane < H_C, z, _log_sigmoid(z))


def _odd_in_step_body(x_ref, g_ref, w_ref, wgate_ref, bias_ref, q_ref, k_ref, v_ref, o_ref, gt_ref):
    xn = _rms(x_ref[...], g_ref[...]).astype(BF16)
    proj = lambda lo, n: _dot(xn, w_ref[:, lo:lo + n])
    q_ref[...] = proj(O_Q, W_C).astype(BF16)
    k_ref[...] = (proj(O_K, W_C) * DH_C ** -0.5).astype(BF16)
    v_ref[...] = proj(O_V, W_C)
    o_ref[...] = proj(O_O, W_C)
    gt_ref[...] = _odd_gates(xn, wgate_ref, bias_ref)


def _odd_in_chunk_body(x_ref, g_ref, w_ref, wgate_ref, wqt_ref, wvt_ref, bias_ref,
                       qt_ref, k_ref, vt_ref, o_ref, gt_ref, gtt_ref, *, chunk):
    xn = _rms(x_ref[...], g_ref[...]).astype(BF16)
    proj = lambda lo, n: _dot(xn, w_ref[:, lo:lo + n])
    qt = _dot_nt(wqt_ref[...], xn).astype(BF16)
    vt = _dot_nt(wvt_ref[...], xn).astype(BF16)
    for c in range(qt_ref.shape[1]):
        qt_ref[0, c] = qt[:, c * chunk:(c + 1) * chunk]
        vt_ref[0, c] = vt[:, c * chunk:(c + 1) * chunk]
    k_ref[...] = (proj(O_K, W_C) * DH_C ** -0.5).astype(BF16)
    o_ref[...] = proj(O_O, W_C)
    gates = _odd_gates(xn, wgate_ref, bias_ref)
    gt_ref[...] = gates
    gtt_ref[...] = gates.T[:2 * H_C, :]


def odd_inputs(x, g, w, wgate, bias, prompt_batch=None):
    m, d = x.shape
    tm = _row_tile(m)
    if prompt_batch is None:
        widths = [(W_C, BF16), (W_C, BF16), (W_C, F32), (W_C, F32), (LANES, F32)]
        return pl.pallas_call(
            _odd_in_step_body,
            grid=(m // tm,),
            in_specs=[_rows(tm, d), _resident((1, d)), _resident(w.shape), _resident(wgate.shape),
                      _resident(bias.shape)],
            out_specs=[_rows(tm, n) for n, _ in widths],
            out_shape=[jax.ShapeDtypeStruct((m, n), dt) for n, dt in widths],
            compiler_params=_params("arbitrary"),
            name="odd_inputs",
        )(x, g, w, wgate, bias)
    seq = m // prompt_batch
    chunk = min(MLSTM_CHUNK, seq)
    per_seq = seq // tm
    chunked = pl.BlockSpec((1, tm // chunk, W_C, chunk), lambda i: (i // per_seq, i % per_seq, 0, 0))
    chunked_shape = jax.ShapeDtypeStruct((prompt_batch, seq // chunk, W_C, chunk), BF16)
    wqt = w[:, O_Q:O_Q + W_C].T
    wvt = w[:, O_V:O_V + W_C].T
    return pl.pallas_call(
        functools.partial(_odd_in_chunk_body, chunk=chunk),
        grid=(m // tm,),
        in_specs=[_rows(tm, d), _resident((1, d)), _resident(w.shape), _resident(wgate.shape),
                  _resident(wqt.shape), _resident(wvt.shape), _resident(bias.shape)],
        out_specs=[chunked, _rows(tm, W_C), chunked, _rows(tm, W_C), _rows(tm, LANES),
                   pl.BlockSpec((2 * H_C, tm), lambda i: (0, i))],
        out_shape=[chunked_shape, jax.ShapeDtypeStruct((m, W_C), BF16), chunked_shape,
                   jax.ShapeDtypeStruct((m, W_C), F32), jax.ShapeDtypeStruct((m, LANES), F32),
                   jax.ShapeDtypeStruct((2 * H_C, m), F32)],
        compiler_params=_params("arbitrary"),
        name="odd_inputs",
    )(x, g, w, wgate, wqt, wvt, bias)


def _mlstm_body(qt_ref, k_ref, vt_ref, gt_ref, gtt_ref, c0_ref, n0_ref, m0_ref,
                h_ref, cfin_ref, nfin_ref, mfin_ref, c_sc, n_sc, m_sc, *, chunk):
    n = pl.program_id(1)

    @pl.when(n == 0)
    def _():
        c_sc[...] = c0_ref[0]
        n_sc[...] = n0_ref[0]
        m_sc[...] = m0_ref[0]

    r = lax.broadcasted_iota(jnp.int32, (chunk, chunk), 0)
    c = lax.broadcasted_iota(jnp.int32, (chunk, chunk), 1)
    past = r <= c
    gates = gt_ref[...]
    gates_t = gtt_ref[...]
    cum_col = jnp.dot(jnp.where(r >= c, 1.0, 0.0), gates, precision=HIGHEST, preferred_element_type=F32)
    cum_row = jnp.dot(gates_t, jnp.where(past, 1.0, 0.0), precision=HIGHEST, preferred_element_type=F32)
    ones = jnp.ones((ONES_ROWS, chunk), BF16)
    hs = range(H_C)
    sl = [slice(h * DH_C, (h + 1) * DH_C) for h in hs]
    m_t, dw, pw, wk, a, m_end = [], [], [], [], [], []
    for h in hs:
        i_row = gates_t[h:h + 1, :]
        b_row = cum_row[H_C + h:H_C + h + 1, :]
        b_last = b_row[:, chunk - 1:chunk]
        m_prev = m_sc[h:h + 1, 0:1]
        src_col = gates[:, h:h + 1] - cum_col[:, H_C + h:H_C + h + 1]
        dlog = jnp.where(past, src_col + b_row, -jnp.inf)
        prev_log = b_row + m_prev
        m_t.append(jnp.maximum(prev_log, jnp.max(dlog, axis=0, keepdims=True)))
        dw.append(jnp.exp(dlog - m_t[h]))
        pw.append(jnp.exp(prev_log - m_t[h]))
        m_end.append(m_t[h][:, chunk - 1:chunk])
        wk.append(jnp.exp(b_last - b_row + i_row - m_end[h]))
        a.append(jnp.exp(b_last + m_prev - m_end[h]))
    qt = [qt_ref[0, 0, sl[h], :] for h in hs]
    vt = [vt_ref[0, 0, sl[h], :] for h in hs]
    k = [k_ref[:, sl[h]] for h in hs]
    c_old = [c_sc[h] for h in hs]
    n_old = [n_sc[h:h + 1, :] for h in hs]
    qk = [_dot(k[h], qt[h]) for h in hs]
    cq = [_dot(c_old[h].astype(BF16), qt[h]) for h in hs]
    nq = [_dot(jnp.broadcast_to(n_old[h], (8, DH_C)).astype(BF16), qt[h])[0:1] for h in hs]
    intra = [_dot(jnp.concatenate([vt[h], ones], axis=0), (qk[h] * dw[h]).astype(BF16)) for h in hs]
    c_add = [_dot((vt[h].astype(F32) * wk[h]).astype(BF16), k[h]) for h in hs]
    n_add = [_dot(jnp.broadcast_to(wk[h], (8, chunk)).astype(BF16), k[h])[0:1] for h in hs]
    for h in hs:
        num = intra[h][:DH_C] + pw[h] * cq[h]
        den = intra[h][DH_C:DH_C + 1] + pw[h] * nq[h]
        h_t = num * (1.0 / jnp.maximum(jnp.abs(den), jnp.exp(-m_t[h])))
        h_ref[:, sl[h]] = h_t.T
        c_sc[h] = a[h] * c_old[h] + c_add[h]
        n_sc[h:h + 1, :] = a[h] * n_old[h] + n_add[h]
        m_sc[h:h + 1, :] = jnp.broadcast_to(m_end[h], (1, LANES))

    @pl.when(n == pl.num_programs(1) - 1)
    def _():
        cfin_ref[0] = c_sc[...]
        nfin_ref[0] = n_sc[...]
        mfin_ref[0] = m_sc[...]


def mlstm_prompt(qt, k, vt, gates, gates_t, c0, n0, m0):
    batch, nc, _, chunk = qt.shape
    m = k.shape[0]
    rows = lambda w: pl.BlockSpec((chunk, w), lambda b, n: (b * nc + n, 0))
    chunked = pl.BlockSpec((1, 1, W_C, chunk), lambda b, n: (b, n, 0, 0))
    st_c = pl.BlockSpec((1, H_C, DH_C, DH_C), lambda b, n: (b, 0, 0, 0))
    st_n = pl.BlockSpec((1, H_C, DH_C), lambda b, n: (b, 0, 0))
    st_m = pl.BlockSpec((1, H_C, LANES), lambda b, n: (b, 0, 0))
    return pl.pallas_call(
        functools.partial(_mlstm_body, chunk=chunk),
        grid=(batch, nc),
        in_specs=[chunked, rows(W_C), chunked, rows(LANES),
                  pl.BlockSpec((2 * H_C, chunk), lambda b, n: (0, b * nc + n)), st_c, st_n, st_m],
        out_specs=[rows(W_C), st_c, st_n, st_m],
        out_shape=[jax.ShapeDtypeStruct((m, W_C), F32), jax.ShapeDtypeStruct(c0.shape, F32),
                   jax.ShapeDtypeStruct(n0.shape, F32), jax.ShapeDtypeStruct(m0.shape, F32)],
        scratch_shapes=[pltpu.VMEM((H_C, DH_C, DH_C), F32), pltpu.VMEM((H_C, DH_C), F32),
                        pltpu.VMEM((H_C, LANES), F32)],
        compiler_params=_params("arbitrary", "arbitrary"),
        name="mlstm_prompt",
    )(qt, k, vt, gates, gates_t, c0, n0, m0)


def _mlstm_step_body(q_ref, k_ref, v_ref, gt_ref, c0_ref, n0_ref, m0_ref, h_ref, c_ref, n_ref, m_ref):
    for i in range(q_ref.shape[0]):
        gates = gt_ref[i]
        for h in range(H_C):
            sl = slice(h * DH_C, (h + 1) * DH_C)
            i_pre = gates[:, h:h + 1]
            logf = gates[:, H_C + h:H_C + h + 1]
            m_prev = m0_ref[i, h:h + 1, 0:1]
            prev_log = logf + m_prev
            m_t = jnp.maximum(prev_log, i_pre)
            dw = jnp.exp(i_pre - m_t)
            pw = jnp.exp(prev_log - m_t)
            q = q_ref[i, :, sl]
            k = k_ref[i, :, sl]
            v = v_ref[i, :, sl]
            qf = q.astype(F32)
            kf = k.astype(F32)
            c_old = c0_ref[i, h]
            n_old = n0_ref[i, h:h + 1, :]
            sw = jnp.sum(qf * kf, axis=1, keepdims=True) * dw
            cq = _dot_nt(jnp.broadcast_to(q, (8, DH_C)), c_old.astype(BF16))[0:1]
            num = sw * v + pw * cq
            den = sw + pw * jnp.sum(qf * n_old, axis=1, keepdims=True)
            h_ref[i, :, sl] = num / jnp.maximum(jnp.abs(den), jnp.exp(-m_t))
            c_ref[i, h] = pw * c_old + _column(dw * v) * kf
            n_ref[i, h:h + 1, :] = pw * n_old + dw * kf
            m_ref[i, h:h + 1, :] = jnp.broadcast_to(m_t, (1, LANES))


def mlstm_step(q, k, v, gates, c0, n0, m0):
    bd = q.shape[0]
    per_step = math.gcd(MLSTM_STEP_SEQS, bd)
    row = lambda w: pl.BlockSpec((per_step, 1, w), lambda b: (b, 0, 0))
    st_c = pl.BlockSpec((per_step, H_C, DH_C, DH_C), lambda b: (b, 0, 0, 0))
    st_n = pl.BlockSpec((per_step, H_C, DH_C), lambda b: (b, 0, 0))
    st_m = pl.BlockSpec((per_step, H_C, LANES), lambda b: (b, 0, 0))
    h, c_fin, n_fin, m_fin = pl.pallas_call(
        _mlstm_step_body,
        grid=(bd // per_step,),
        in_specs=[row(W_C), row(W_C), row(W_C), row(LANES), st_c, st_n, st_m],
        out_specs=[row(W_C), st_c, st_n, st_m],
        out_shape=[jax.ShapeDtypeStruct((bd, 1, W_C), F32), jax.ShapeDtypeStruct(c0.shape, F32),
                   jax.ShapeDtypeStruct(n0.shape, F32), jax.ShapeDtypeStruct(m0.shape, F32)],
        compiler_params=_params("arbitrary"),
        name="mlstm_step",
    )(q.reshape(bd, 1, W_C), k.reshape(bd, 1, W_C), v.reshape(bd, 1, W_C), gates.reshape(bd, 1, LANES),
      c0, n0, m0)
    return h.reshape(bd, W_C), c_fin, n_fin, m_fin


def _odd_mix(x_ref, hm_ref, og_ref, gn_ref, w_ref):
    gated = jax.nn.sigmoid(og_ref[...]) * _head_rms(hm_ref[...], gn_ref[...], DH_C)
    return x_ref[...] + _dot(gated.astype(BF16), w_ref[...])


def odd_output_ffn(prompt_rows, sample_rows, gn, w, g, weights, cast_next=None):
    return _ffn_call(_odd_mix, "odd_output_ffn", prompt_rows, sample_rows, [gn, w], g, weights, cast_next)


def _even_weights(w_in, w_out, qk_norm, gate_w2, gate_b, a_norm, b_norm):
    w = w_in.astype(BF16)
    wbr = w_in[:, E_MAIN + GLA_RANK:].astype(BF16)
    wbg = jnp.pad(w_in[:, E_MAIN:E_MAIN + GLA_RANK], ((0, 0), (0, LANES - GLA_RANK))).astype(BF16)
    w2 = jnp.concatenate([gate_w2, jnp.zeros((LANES - GLA_RANK, W_BK), gate_w2.dtype)], axis=0).astype(BF16)
    idx = jnp.arange(MXU_WIDTH) // DK_A
    seg = (idx[:, None] == idx[None, :]).astype(BF16)
    gq = jnp.tile(qk_norm[0], W_A // DK_A).reshape(1, W_A)
    gk = jnp.tile(qk_norm[1], W_A // DK_A).reshape(1, W_A)
    return dict(w=(w, wbr, wbg), w2=w2, seg=seg, gq=gq, gk=gk, gb=gate_b.reshape(1, W_BK),
                w_out=w_out.astype(BF16), ga=a_norm.reshape(1, W_A), gbn=b_norm.reshape(1, W_BV))


def _odd_weights(w_in, w_out, gate_b, norm_g):
    w = w_in.astype(BF16)
    wgate =jnp.pad(w_in[:, O_G:], ((0, 0), (0, LANES - 2 * H_C))).astype(BF16)
    bias =jnp.concatenate([gate_b[0], gate_b[1], jnp.zeros((LANES - 2 * H_C,), gate_b.dtype)]).reshape(1, LANES)
    return dict(w=w, wgate=wgate, bias=bias, w_out=w_out.astype(BF16), g=norm_g.reshape(1, W_C))


def kernel(x_prompt, x_sample, cache_k, cache_v, state_gla, state_mlstm_C, state_mlstm_n, state_mlstm_m,
           page_table, norm_g, ffn_w_gate, ffn_w_up, ffn_w_down, even_w_in, even_w_out, a_qk_norm, a_lambda,
           a_head_norm, b_gate_w2, b_gate_bias, b_head_norm, odd_w_in, odd_w_out, c_gate_bias, c_head_norm):
    bp, seq, d = x_prompt.shape
    bd = x_sample.shape[0]
    depth = norm_g.shape[0]
    assert x_sample.shape[1] == 1
    yp = x_prompt.reshape(bp * seq, d)
    ys = x_sample.reshape(bd, d)
    n_even, n_pool, page = cache_k.shape[:3]
    ck = jnp.transpose(cache_k, (0, 1, 3, 4, 5, 2)).reshape(n_even, n_pool, W_A, page)
    cv = cache_v.reshape(n_even, n_pool, page * H_A, DV_A)
    slopes = jnp.asarray([2.0 ** (-8.0 * (h + 1) / H_A) for h in range(H_A)], F32)
    slope8 = jnp.repeat(slopes, 2).reshape(2 * H_A, 1)
    ffn32 = (ffn_w_gate, ffn_w_up, ffn_w_down)
    w_first = tuple(w[0, 0].astype(BF16) for w in ffn32)
    pos = jnp.arange(seq)[:, None]
    lane = jnp.arange(LANES)[None, :]
    posb = jnp.where(lane >= 6, 0, jnp.where(lane % 2 == 0, pos // POS_SPLIT * POS_SPLIT,
                                             pos % POS_SPLIT)).astype(BF16)

    k_p, v_p, k_s, v_s, gla_p, gla_s = [], [], [], [], [], []
    cm_p, nm_p, mm_p, cm_s, nm_s, mm_s = [], [], [], [], [], []
    for li in range(depth):
        g = norm_g[li].reshape(3, 1, d)
        yp, ys, w_second = ffn_half(yp, ys, g[0], w_first, cast_next=(*ffn32, li, 1))
        cast_next = (*ffn32, li + 1, 0) if li + 1 < depth else None
        if li % 2 == 0:
            e = li // 2
            lam_init = 0.8 - 0.6 * math.exp(-0.3 * li)
            ew = _even_weights(even_w_in[e], even_w_out[e], a_qk_norm[e], b_gate_w2[e], b_gate_bias[e],
                               a_head_norm[e], b_head_norm[e])
            lam_vecs = a_lambda[e]
            aqt, akb, akt, av, avt, bq, bk, bv, gl, br = even_inputs(
                yp, g[1], *ew["w"], ew["seg"], ew["gq"], ew["gk"], ew["w2"], ew["gb"], prompt_batch=bp)
            a_out = diff_attn_prompt(aqt, akb, avt, posb, slopes, lam_vecs, lam_init)
            b_out, s_fin = gla_prompt(bq, bk, gl, bv, jnp.zeros((bp, W_BK, DV_B), F32), bp)
            prompt_rows = [yp, a_out, b_out, br]
            k_p.append(jnp.transpose(akt.reshape(bp, H_A, 2, DK_A, seq), (0, 4, 1, 2, 3)))
            v_p.append(av.reshape(bp, seq, H_A, DV_A))
            gla_p.append(s_fin.reshape(bp, H_B, DK_B, DV_B))
            aq, ak, av, bq, bk, bv, gl, br = even_inputs(
                ys, g[1], *ew["w"], ew["seg"], ew["gq"], ew["gk"], ew["w2"], ew["gb"])
            a_out = diff_attn_decode(page_table, aq, ak, av, slope8, lam_vecs, lam_init, ck, cv, e)
            b_out, s_fin = gla_step(bq, bk, gl, bv, state_gla[e])
            k_s.append(ak.reshape(bd, 1, H_A, 2, DK_A))
            v_s.append(av.reshape(bd, 1, H_A, DV_A))
            gla_s.append(s_fin)
            yp, ys, w_first = even_output_ffn(prompt_rows, [ys, a_out, b_out, br], ew["ga"], ew["gbn"],
                                              ew["w_out"], lam_init, g[2], w_second, cast_next)
        else:
            o = li // 2
            ow = _odd_weights(odd_w_in[o], odd_w_out[o], c_gate_bias[o], c_head_norm[o])
            qt, k, vt, og, gates, gates_t = odd_inputs(yp, g[1], ow["w"], ow["wgate"], ow["bias"],
                                                       prompt_batch=bp)
            h, c_fin, n_fin, m_fin = mlstm_prompt(
                qt, k, vt, gates, gates_t, jnp.zeros((bp, H_C, DH_C, DH_C), F32),
                jnp.zeros((bp, H_C, DH_C), F32), jnp.zeros((bp, H_C, LANES), F32))
            prompt_rows = [yp, h, og]
            cm_p.append(c_fin)
            nm_p.append(n_fin)
            mm_p.append(m_fin[:, :, 0])
            q, k, v, og, gates = odd_inputs(ys, g[1], ow["w"], ow["wgate"], ow["bias"])
            m0 = jnp.broadcast_to(state_mlstm_m[o][:, :, None], (bd, H_C, LANES))
            h, c_fin, n_fin, m_fin = mlstm_step(q, k, v, gates, state_mlstm_C[o], state_mlstm_n[o], m0)
            cm_s.append(c_fin)
            nm_s.append(n_fin)
            mm_s.append(m_fin[:, :, 0])
            yp, ys, w_first = odd_output_ffn(prompt_rows, [ys, h, og], ow["g"], ow["w_out"], g[2], w_second,
                                             cast_next)
    return (yp.reshape(bp, seq, d), ys.reshape(bd, 1, d),
            jnp.stack(k_p), jnp.stack(v_p), jnp.stack(k_s), jnp.stack(v_s),
            jnp.stack(gla_p), jnp.stack(gla_s),
            jnp.stack(cm_p), jnp.stack(nm_p), jnp.stack(mm_p),
            jnp.stack(cm_s), jnp.stack(nm_s), jnp.stack(mm_s))
```

```python
import functools
import math

import jax
import jax.numpy as jnp
from jax import lax
from jax.experimental import pallas as pl
from jax.experimental.pallas import tpu as pltpu

F32 = jnp.float32
BF16 = jnp.bfloat16
EPS = 1e-6
HIGHEST = lax.Precision.HIGHEST
LOG2E = math.log2(math.e)

LANES = 128
MXU_WIDTH = 256
ONES_ROWS = 16
POS_SPLIT = 64
VMEM_LIMIT_BYTES = 56 * 2**20

H_A, DK_A, DV_A = 4, 64, 128
H_B, DK_B, DV_B = 4, 64, 128
GLA_RANK = 16
GLA_NORMALIZER = 16.0
H_C, DH_C = 4, 256
W_A = H_A * 2 * DK_A
W_BK = H_B * DK_B
W_BV = H_B * DV_B
W_C = H_C * DH_C

ROW_TILE = 512
ATTN_TILE = 256
ATTN_KV_CHUNK = 1024
ATTN_HEADS_PER_STEP = 4
GLA_CHUNK = 128
MLSTM_CHUNK = 256
DECODE_PAGES = 32
GLA_STEP_SEQS = 8
MLSTM_STEP_SEQS = 2

NT_DIMS = (((1,), (1,)), ((), ()))


def _params(*semantics):
    return pltpu.CompilerParams(dimension_semantics=semantics, vmem_limit_bytes=VMEM_LIMIT_BYTES)


def _resident(shape):
    return pl.BlockSpec(shape, lambda *_: (0,) * len(shape), pipeline_mode=pl.Buffered(1))


def _rows(tm, width):
    return pl.BlockSpec((tm, width), lambda i: (i, 0))


def _row_tile(m):
    return ROW_TILE if m % ROW_TILE == 0 else m


def _rms(x, g):
    return x * lax.rsqrt(jnp.mean(x * x, axis=-1, keepdims=True) + EPS) * g


def _log_sigmoid(x):
    return jnp.minimum(x, 0.0) - jnp.log1p(jnp.exp(-jnp.abs(x)))


def _dot(a, b):
    return jnp.dot(a, b, preferred_element_type=F32)


def _dot_nt(a, b):
    return lax.dot_general(a, b, NT_DIMS, preferred_element_type=F32)


def _diff_lambda(lam_vecs, lam_init):
    p1 = jnp.sum(lam_vecs[0:1] * lam_vecs[1:2], axis=-1, keepdims=True)
    p2 = jnp.sum(lam_vecs[2:3] * lam_vecs[3:4], axis=-1, keepdims=True)
    return jnp.exp(p1) - jnp.exp(p2) + lam_init


FF_CHUNK = 256


def _swiglu_half(x, g_ref, wg_ref, wu_ref, wd_ref, h_ref):
    xn = _rms(x, g_ref[...]).astype(BF16)
    for c in range(0, wg_ref.shape[1], FF_CHUNK):
        hg = _dot(xn, wg_ref[:, c:c + FF_CHUNK])
        hu = _dot(xn, wu_ref[:, c:c + FF_CHUNK])
        h_ref[:, c:c + FF_CHUNK] = (hg * jax.nn.sigmoid(hg) * hu).astype(BF16)
    return x + 0.5 * _dot(h_ref[...], wd_ref[...])


def _ffn_kernel(*refs, prologue, n_rows, n_small, steps):
    rows, samples = refs[:n_rows], refs[n_rows:2 * n_rows]
    small = refs[2 * n_rows:2 * n_rows + n_small]
    g_ref, wg_ref, wu_ref, wd_ref = refs[2 * n_rows + n_small:2 * n_rows + n_small + 4]
    rest = refs[2 * n_rows + n_small + 4:]
    n_cast = (len(rest) - 3) // 2
    src, o_ref, so_ref = rest[:n_cast], rest[n_cast], rest[n_cast + 1]
    dst, h_ref = rest[n_cast + 2:2 * n_cast + 2], rest[2 * n_cast + 2]
    i = pl.program_id(0)

    @pl.when(i < steps)
    def _():
        o_ref[...] = _swiglu_half(prologue(*rows, *small), g_ref, wg_ref, wu_ref, wd_ref, h_ref)
        for s_ref, d_ref in zip(src, dst):
            d_ref[...] = s_ref[...].astype(BF16)

    @pl.when(i == steps)
    def _():
        so_ref[...] = _swiglu_half(prologue(*samples, *small), g_ref, wg_ref, wu_ref, wd_ref,
                                   h_ref.at[pl.ds(0, so_ref.shape[0])])


def _ffn_call(prologue, name, row_inputs, sample_inputs, small_inputs, g, weights, cast_next=None):
    wg, wu, wd = weights
    m, d = row_inputs[0].shape
    ms = sample_inputs[0].shape[0]
    dff = wg.shape[-1]
    tm = _row_tile(m)
    steps = m // tm
    assert ms <= tm
    tile = lambda i: jnp.minimum(i, steps - 1)
    in_specs = ([pl.BlockSpec((tm, a.shape[1]), lambda i: (tile(i), 0)) for a in row_inputs]
                + [_resident(a.shape) for a in sample_inputs] + [_resident(a.shape) for a in small_inputs]
                + [_resident((1, d)), _resident((d, dff)), _resident((d, dff)), _resident((dff, d))])
    out_specs = [pl.BlockSpec((tm, d), lambda i: (tile(i), 0)), pl.BlockSpec((ms, d), lambda i: (0, 0))]
    out_shape = [jax.ShapeDtypeStruct((m, d), F32), jax.ShapeDtypeStruct((ms, d), F32)]
    operands = [*row_inputs, *sample_inputs, *small_inputs, g, wg, wu, wd]
    if cast_next is not None:
        *next32, layer, half = cast_next
        up_rows, down_rows, down_div = d // steps, dff // (steps // 2), 2
        assert up_rows * steps == d and down_rows * (steps // down_div) == dff
        assert up_rows % 16 == 0 and down_rows % 16 == 0
        in_specs += [pl.BlockSpec((None, None, up_rows, dff), lambda i: (layer, half, tile(i), 0)),
                     pl.BlockSpec((None, None, up_rows, dff), lambda i: (layer, half, tile(i), 0)),
                     pl.BlockSpec((None, None, down_rows, d), lambda i: (layer, half, tile(i) // down_div, 0))]
        out_specs += [pl.BlockSpec((up_rows, dff), lambda i: (tile(i), 0)),
                      pl.BlockSpec((up_rows, dff), lambda i: (tile(i), 0)),
                      pl.BlockSpec((down_rows, d), lambda i: (tile(i) // down_div, 0))]
        out_shape += [jax.ShapeDtypeStruct((d, dff), BF16), jax.ShapeDtypeStruct((d, dff), BF16),
                      jax.ShapeDtypeStruct((dff, d), BF16)]
        operands += next32
    out = pl.pallas_call(
        functools.partial(_ffn_kernel, prologue=prologue, n_rows=len(row_inputs), n_small=len(small_inputs),
                          steps=steps),
        grid=(steps + 1,),
        in_specs=in_specs,
        out_specs=out_specs,
        out_shape=out_shape,
        scratch_shapes=[pltpu.VMEM((tm, dff), BF16)],
        compiler_params=_params("arbitrary"),
        name=name,
    )(*operands)
    return out[0], out[1], tuple(out[2:])


def ffn_half(x, x_sample, g, weights, cast_next=None):
    return _ffn_call(lambda x_ref: x_ref[...], "ffn_half", [x], [x_sample], [], g, weights, cast_next)


E_AQ, E_AK, E_AV = 0, W_A, 2 * W_A
E_BQ = 3 * W_A
E_BK = E_BQ + W_BK
E_BV = E_BK + W_BK
E_MAIN = E_BV + W_BV


def _even_in_body(x_ref, g_ref, w_ref, wbr_ref, wbg_ref, seg_ref, gq_ref, gk_ref, w2_ref, gb_ref, *out_refs,
                  attn_tile):
    xn = _rms(x_ref[...], g_ref[...]).astype(BF16)
    proj = lambda lo, n: _dot(xn, w_ref[:, lo:lo + n])

    def group_rms(t, g):
        t2 = (t * t).astype(BF16)
        ms = jnp.concatenate([_dot(t2[:, c:c + MXU_WIDTH], seg_ref[...])
                              for c in range(0, t.shape[1], MXU_WIDTH)], axis=1)
        return t * lax.rsqrt(ms * (1.0 / DK_A) + EPS) * g

    aq = group_rms(proj(E_AQ, W_A), gq_ref[...]) * DK_A ** -0.5
    ak = group_rms(proj(E_AK, W_A), gk_ref[...])
    av = proj(E_AV, W_A)
    if attn_tile is None:
        aq_ref, ak_ref, av_ref, bq_ref, bk_ref, bv_ref, gl_ref, br_ref = out_refs
        aq_ref[...] = aq.astype(BF16)
        ak_ref[...] = ak
        av_ref[...] = av
    else:
        aqt_ref, akb_ref, akt_ref, av_ref, avt_ref, bq_ref, bk_ref, bv_ref, gl_ref, br_ref = out_refs
        aqt = (aq * LOG2E).T.astype(BF16)
        avt = av.T.astype(BF16)
        for c in range(aqt_ref.shape[1]):
            aqt_ref[0, c] = aqt[:, c * attn_tile:(c + 1) * attn_tile]
            avt_ref[0, c] = avt[:, c * attn_tile:(c + 1) * attn_tile]
        akb_ref[...] = ak.astype(BF16)
        akt_ref[0] = ak.T
        for h in range(H_A):
            av_ref[pl.ds(h, av.shape[0], stride=H_A), :] = av[:, h * DV_A:(h + 1) * DV_A]
    bq_ref[...] = proj(E_BQ, W_BK) * DK_B ** -0.5
    bk_ref[...] = proj(E_BK, W_BK)
    bv_ref[...] = proj(E_BV, W_BV)
    br_ref[...] = _dot(xn, wbr_ref[...])
    bg = _dot(xn, wbg_ref[...]).astype(BF16)
    gl_ref[...] = _log_sigmoid(_dot(bg, w2_ref[...]) + gb_ref[...]) * (1.0 / GLA_NORMALIZER)


def even_inputs(x, g, w, wbr, wbg, seg, gq, gk, w2, gb, prompt_batch=None):
    m, d = x.shape
    tm = _row_tile(m)
    common = [(W_BK, F32), (W_BK, F32), (W_BV, F32), (W_BK, F32), (W_BV, F32)]
    if prompt_batch is None:
        attn_tile = None
        widths = [(W_A, BF16), (W_A, F32), (W_A, F32)] + common
        out_specs = [_rows(tm, n) for n, _ in widths]
        out_shape = [jax.ShapeDtypeStruct((m, n), dt) for n, dt in widths]
    else:
        seq = m // prompt_batch
        attn_tile = min(ATTN_TILE, seq)
        per_seq = seq // tm
        per_tile = tm // attn_tile
        chunked = pl.BlockSpec((1, per_tile, W_A, attn_tile), lambda i: (i // per_seq, i % per_seq, 0, 0))
        chunked_shape = jax.ShapeDtypeStruct((prompt_batch, seq // attn_tile, W_A, attn_tile), BF16)
        out_specs = ([chunked, _rows(tm, W_A),
                      pl.BlockSpec((1, W_A, tm), lambda i: (i // per_seq, 0, i % per_seq)),
                      _rows(tm * H_A, DV_A), chunked] + [_rows(tm, n) for n, _ in common])
        out_shape = ([chunked_shape, jax.ShapeDtypeStruct((m, W_A), BF16),
                      jax.ShapeDtypeStruct((prompt_batch, W_A, seq), F32),
                      jax.ShapeDtypeStruct((m * H_A, DV_A), F32), chunked_shape]
                     + [jax.ShapeDtypeStruct((m, n), dt) for n, dt in common])
    return pl.pallas_call(
        functools.partial(_even_in_body, attn_tile=attn_tile),
        grid=(m // tm,),
        in_specs=[_rows(tm, d), _resident((1, d)), _resident(w.shape), _resident(wbr.shape), _resident(wbg.shape),
                  _resident(seg.shape), _resident(gq.shape), _resident(gk.shape), _resident(w2.shape),
                  _resident(gb.shape)],
        out_specs=out_specs,
        out_shape=out_shape,
        compiler_params=_params("arbitrary"),
        name="even_inputs",
    )(x, g, w, wbr, wbg, seg, gq, gk, w2, gb)


def _attn_body(slope_ref, lam_ref, qt_ref, k_ref, posb_ref, vt_ref, o_ref, m_ref, acc_ref, *, tile, kv, lam_init):
    qi = pl.program_id(2)
    heads = m_ref.shape[0] // 2
    row = lax.broadcasted_iota(jnp.int32, (DV_A, tile), 0)
    erow = lax.broadcasted_iota(jnp.int32, (LANES, tile), 0)
    rhs = []
    for hh in range(heads):
        slope = slope_ref[pl.program_id(1) * heads + hh]
        qt = qt_ref[0, 0, hh * DV_A:(hh + 1) * DV_A, :]
        c_all = jnp.full((LANES, tile), slope * LOG2E, F32)
        c_1 = c_all.astype(BF16).astype(F32)
        c_2 = (c_all - c_1).astype(BF16).astype(F32)
        c_3 = (c_all - c_1 - c_2).astype(BF16).astype(F32)
        ext = jnp.where(erow < 2, c_1, jnp.where(erow < 4, c_2, jnp.where(erow < 6, c_3, 0.0))).astype(BF16)
        for mm in range(2):
            own = (row >= mm * DK_A) & (row < (mm + 1) * DK_A)
            rhs.append(jnp.concatenate([jnp.where(own, qt, jnp.zeros_like(qt)), ext], axis=0))
    m_ref[...] = jnp.full(m_ref.shape, -jnp.inf, F32)
    acc_ref[...] = jnp.zeros(acc_ref.shape, F32)
    per_kv = kv // tile

    def step(j, masked, keys):
        rows = pl.ds(pl.multiple_of(j * kv, kv), keys)
        ones = jnp.ones((ONES_ROWS, keys), BF16)
        posb = posb_ref[rows, :]
        scores = []
        for hh in range(heads):
            kx = jnp.concatenate([k_ref[rows, hh * DV_A:(hh + 1) * DV_A], posb], axis=1)
            scores += [_dot(kx, rhs[2 * hh]), _dot(kx, rhs[2 * hh + 1])]
        for st, s in enumerate(scores):
            hs = slice(st // 2 * DV_A, (st // 2 + 1) * DV_A)
            if masked:
                kk = lax.broadcasted_iota(jnp.int32, s.shape, 0) + (j * kv - qi * tile)
                qq = lax.broadcasted_iota(jnp.int32, s.shape, 1)
                s = jnp.where(kk <= qq, s, -jnp.inf)
            m_old = m_ref[st]
            m_new = jnp.maximum(m_old, jnp.max(s, axis=0, keepdims=True))
            p = jnp.exp2(s - m_new).astype(BF16)
            vt = jnp.concatenate([vt_ref[0, j * per_kv + c, hs, :] for c in range(keys // tile)], axis=1)
            vx = jnp.concatenate([vt, ones], axis=0)
            acc_ref[st] = jnp.exp2(m_old - m_new) * acc_ref[st] + _dot(vx, p)
            m_ref[st] = m_new

    def body(j, carry):
        step(j, False, kv)
        return carry

    n_full = (qi * tile) // kv
    lax.fori_loop(0, n_full, body, 0)
    for r in range(per_kv):
        pl.when(qi % per_kv == r)(functools.partial(step, n_full, True, (r + 1) * tile))
    lam = _diff_lambda(lam_ref[...], lam_init)
    for hh in range(heads):
        acc0 = acc_ref[2 * hh]
        acc1 = acc_ref[2 * hh + 1]
        out0 = acc0[:DV_A] * (1.0 / acc0[DV_A:DV_A + 1])
        out1 = acc1[:DV_A] * (1.0 / acc1[DV_A:DV_A + 1])
        o_ref[:, hh * DV_A:(hh + 1) * DV_A] = (out0 - lam * out1).T


def diff_attn_prompt(aqt, akb, avt, posb, slopes, lam_vecs, lam_init):
    batch, nq, _, tile = aqt.shape
    seq = nq * tile
    kv = min(ATTN_KV_CHUNK, seq)
    heads = ATTN_HEADS_PER_STEP
    wide = heads * DV_A
    return pl.pallas_call(
        functools.partial(_attn_body, tile=tile, kv=kv, lam_init=lam_init),
        grid=(batch, H_A // heads, nq),
        in_specs=[pl.BlockSpec(memory_space=pltpu.SMEM),
                  pl.BlockSpec(lam_vecs.shape, lambda b, h, i: (0, 0)),
                  pl.BlockSpec((1, 1, wide, tile), lambda b, h, i: (b, i, h, 0)),
                  pl.BlockSpec((seq, wide), lambda b, h, i: (b, h)),
                  pl.BlockSpec((seq, LANES), lambda b, h, i: (0, 0)),
                  pl.BlockSpec((1, nq, wide, tile), lambda b, h, i: (b, 0, h, 0))],
        out_specs=pl.BlockSpec((tile, wide), lambda b, h, i: (b * nq + i, h)),
        out_shape=jax.ShapeDtypeStruct((batch * seq, W_A), F32),
        scratch_shapes=[pltpu.VMEM((2 * heads, 1, tile), F32), pltpu.VMEM((2 * heads, DV_A + ONES_ROWS, tile), F32)],
        compiler_params=_params("arbitrary", "arbitrary", "arbitrary"),
        name="diff_attn_prompt",
    )(slopes, lam_vecs, aqt, akb, posb, avt)


def _decode_body(pt_ref, q_ref, kn_ref, vn_ref, slope_ref, lam_ref, *rest, pages, past, lam_init):
    kt_refs = rest[:pages]
    v_refs = rest[pages:2 * pages]
    o_ref, m_ref, l_ref, acc_ref = rest[2 * pages:]
    j = pl.program_id(1)
    page = kt_refs[0].shape[3]
    groups = 2 * H_A

    @pl.when(j == 0)
    def _():
        m_ref[...] = jnp.full(m_ref.shape, -jnp.inf, F32)
        l_ref[...] = jnp.zeros(l_ref.shape, F32)
        acc_ref[...] = jnp.zeros(acc_ref.shape, F32)

    qrow = q_ref[0].astype(F32)
    lane = lax.broadcasted_iota(jnp.int32, (groups, W_A), 1)
    r8 = lax.broadcasted_iota(jnp.int32, (groups, W_A), 0)
    q8 = jnp.where(lane // DK_A == r8, jnp.broadcast_to(qrow, (groups, W_A)), 0.0)
    q8_bf = q8.astype(BF16)
    slope = slope_ref[...]
    col = lax.broadcasted_iota(jnp.int32, (1, page), 1)
    row_head = lax.broadcasted_iota(jnp.int32, (groups, 1), 0) // 2

    s_parts = []
    for i in range(pages):
        kpos = (col + (j * pages + i) * page).astype(F32)
        s_parts.append(_dot(q8_bf, kt_refs[i][0, 0].astype(BF16)) + slope * kpos)
    s = jnp.concatenate(s_parts, axis=1)
    m_old = m_ref[...]
    m_new = jnp.maximum(m_old, jnp.max(s, axis=1, keepdims=True))
    p = jnp.exp(s - m_new).astype(BF16)
    alpha = jnp.exp(m_old - m_new)
    pv = jnp.zeros(acc_ref.shape, F32)
    for i in range(pages):
        p_i = p[:, i * page:(i + 1) * page]
        for h in range(H_A):
            v_h = v_refs[i][0, 0, pl.ds(h, page, stride=H_A), :].astype(BF16)
            pv = pv + jnp.where(row_head == h, _dot(p_i, v_h), 0.0)
    l_ref[...] = alpha * l_ref[...] + jnp.sum(p.astype(F32), axis=1, keepdims=True)
    acc_ref[...] = alpha * acc_ref[...] + pv
    m_ref[...] = m_new

    @pl.when(j == pl.num_programs(1) - 1)
    def _():
        s_new = jnp.sum(q8 * kn_ref[0], axis=1, keepdims=True) + slope * float(past)
        v_new = jnp.zeros(acc_ref.shape, F32)
        for h in range(H_A):
            v_row = jnp.broadcast_to(vn_ref[0, :, h * DV_A:(h + 1) * DV_A], acc_ref.shape)
            v_new = jnp.where(row_head == h, v_row, v_new)
        m_old = m_ref[...]
        m_fin = jnp.maximum(m_old, s_new)
        p_new = jnp.exp(s_new - m_fin)
        alpha = jnp.exp(m_old - m_fin)
        l_fin = alpha * l_ref[...] + p_new
        o8 = (alpha * acc_ref[...] + p_new * v_new) / l_fin
        lam = _diff_lambda(lam_ref[...], lam_init)
        for h in range(H_A):
            o_ref[0, :, h * DV_A:(h + 1) * DV_A] = o8[2 * h:2 * h + 1] - lam * o8[2 * h + 1:2 * h + 2]


def diff_attn_decode(page_table, aq, ak, av, slope8, lam_vecs, lam_init, cache_kt, cache_v, layer):
    bd = aq.shape[0]
    n_pages = page_table.shape[1]
    page = cache_kt.shape[3]
    pages = math.gcd(DECODE_PAGES, n_pages)
    past = n_pages * page

    def page_spec(i, shape):
        return pl.BlockSpec((1, 1) + shape, lambda b, j, pt: (layer, pt[b * n_pages + j * pages + i], 0, 0))

    row3 = lambda: pl.BlockSpec((1, 1, W_A), lambda b, j, pt: (b, 0, 0))
    const = lambda a: pl.BlockSpec(a.shape, lambda b, j, pt: (0,) * a.ndim)
    grid_spec = pltpu.PrefetchScalarGridSpec(
        num_scalar_prefetch=1,
        grid=(bd, n_pages // pages),
        in_specs=[row3(), row3(), row3(), const(slope8), const(lam_vecs)]
        + [page_spec(i, (W_A, page)) for i in range(pages)]
        + [page_spec(i, (page * H_A, DV_A)) for i in range(pages)],
        out_specs=row3(),
        scratch_shapes=[pltpu.VMEM((2 * H_A, 1), F32), pltpu.VMEM((2 * H_A, 1), F32),
                        pltpu.VMEM((2 * H_A, DV_A), F32)],
    )
    out = pl.pallas_call(
        functools.partial(_decode_body, pages=pages, past=past, lam_init=lam_init),
        grid_spec=grid_spec,
        out_shape=jax.ShapeDtypeStruct((bd, 1, W_A), F32),
        compiler_params=_params("arbitrary", "arbitrary"),
        name="diff_attn_decode",
    )(page_table.reshape(-1), aq.reshape(bd, 1, W_A), ak.reshape(bd, 1, W_A), av.reshape(bd, 1, W_A),
      slope8, lam_vecs, *([cache_kt] * pages), *([cache_v] * pages))
    return out.reshape(bd, W_A)


def _gla_body(q_ref, k_ref, g_ref, v_ref, s0_ref, o_ref, sfin_ref, st_ref, *, chunk):
    n = pl.program_id(0)
    nb = q_ref.shape[0]
    pair = 2 * DK_B

    @pl.when(n == 0)
    def _():
        for b in range(nb):
            for p in range(H_B // 2):
                st_ref[b, p] = s0_ref[b, p * pair:(p + 1) * pair, :].T

    r = lax.broadcasted_iota(jnp.int32, (chunk, chunk), 0)
    c = lax.broadcasted_iota(jnp.int32, (chunk, chunk), 1)
    causal = r >= c
    tri = jnp.where(causal, 1.0, 0.0)
    lane_head = lax.broadcasted_iota(jnp.int32, (1, pair), 1) // DK_B
    mid = chunk // 2
    zero = jnp.zeros((), BF16)
    heads = [(b, h) for b in range(nb) for h in range(H_B)]
    lanes_of = lambda h: slice(h // 2 * pair, (h // 2 + 1) * pair)
    vals_of = lambda h: slice(h * DV_B, (h + 1) * DV_B)
    own_of = lambda h: lane_head == h % 2
    cums = [jnp.dot(tri, g_ref[b], precision=HIGHEST, preferred_element_type=F32) for b in range(nb)]
    q_intra, k_intra, q_inter, k_end, decay = [], [], [], [], []
    for b in range(nb):
        cum = cums[b]
        c_mid = cum[mid - 1:mid, :]
        c_last = cum[chunk - 1:chunk, :]
        q = q_ref[b]
        k = k_ref[b]
        q_intra.append((q * jnp.exp(cum - c_mid)).astype(BF16))
        k_intra.append((k * jnp.exp(c_mid - cum)).astype(BF16))
        q_inter.append((q * jnp.exp(cum)).astype(BF16))
        k_end.append((k * jnp.exp(c_last - cum)).astype(BF16))
        decay.append(jnp.exp(c_last))
    scores = [_dot_nt(jnp.where(own_of(h), q_intra[b][:, lanes_of(h)], zero), k_intra[b][:, lanes_of(h)])
              for b, h in heads]
    states = [[st_ref[b, p] for p in range(H_B // 2)] for b in range(nb)]
    carried = [_dot_nt(jnp.where(own_of(h), q_inter[b][:, lanes_of(h)], zero), states[b][h // 2].astype(BF16))
               for b, h in heads]
    updates = [_dot(v_ref[b, :, vals_of(h)].T.astype(BF16), jnp.where(own_of(h), k_end[b][:, lanes_of(h)], zero))
               for b, h in heads]
    for i, (b, h) in enumerate(heads):
        a = jnp.where(causal, scores[i], 0.0).astype(BF16)
        o_ref[b, :, vals_of(h)] = _dot(a, v_ref[b, :, vals_of(h)].astype(BF16)) + carried[i]
    for b in range(nb):
        for p in range(H_B // 2):
            st_ref[b, p] = (states[b][p] * decay[b][:, p * pair:(p + 1) * pair]
                            + updates[b * H_B + 2 * p] + updates[b * H_B + 2 * p + 1])

    @pl.when(n == pl.num_programs(0) - 1)
    def _():
        for b in range(nb):
            for p in range(H_B // 2):
                sfin_ref[b, p * pair:(p + 1) * pair, :] = st_ref[b, p].T


def gla_prompt(bq, bk, glog, bv, s0, batch):
    m = bq.shape[0]
    seq = m // batch
    chunk = min(GLA_CHUNK, seq)
    blk = lambda w: pl.BlockSpec((batch, chunk, w), lambda n: (0, n, 0))
    state = pl.BlockSpec((batch, W_BK, DV_B), lambda n: (0, 0, 0))
    o, s_fin = pl.pallas_call(
        functools.partial(_gla_body, chunk=chunk),
        grid=(seq // chunk,),
        in_specs=[blk(W_BK), blk(W_BK), blk(W_BK), blk(W_BV), state],
        out_specs=[blk(W_BV), state],
        out_shape=[jax.ShapeDtypeStruct((batch, seq, W_BV), F32),
                   jax.ShapeDtypeStruct((batch, W_BK, DV_B), F32)],
        scratch_shapes=[pltpu.VMEM((batch, H_B // 2, DV_B, 2 * DK_B), F32)],
        compiler_params=_params("arbitrary"),
        name="gla_prompt",
    )(bq.reshape(batch, seq, W_BK), bk.reshape(batch, seq, W_BK), glog.reshape(batch, seq, W_BK),
      bv.reshape(batch, seq, W_BV), s0)
    return o.reshape(m, W_BV), s_fin


def _column(row):
    n = row.shape[1]
    r = lax.broadcasted_iota(jnp.int32, (n, n), 0)
    c = lax.broadcasted_iota(jnp.int32, (n, n), 1)
    return jnp.sum(jnp.where(r == c, jnp.broadcast_to(row, (n, n)), 0.0), axis=1, keepdims=True)


def _gla_step_body(q_ref, k_ref, g_ref, v_ref, s0_ref, o_ref, s_ref):
    pair = 2 * DK_B
    for i in range(q_ref.shape[0]):
        for p in range(H_B // 2):
            sl = slice(p * pair, (p + 1) * pair)
            qc = _column(q_ref[i, :, sl])
            kc = _column(k_ref[i, :, sl])
            dc = jnp.exp(_column(g_ref[i, :, sl]))
            for hh in range(2):
                h = 2 * p + hh
                rs = slice(hh * DK_B, (hh + 1) * DK_B)
                v = v_ref[i, :, h * DV_B:(h + 1) * DV_B]
                s_new = dc[rs] * s0_ref[i, h] + kc[rs] * v
                s_ref[i, h] = s_new
                o_ref[i, :, h * DV_B:(h + 1) * DV_B] = jnp.sum(qc[rs] * s_new, axis=0, keepdims=True)


def gla_step(bq, bk, glog, bv, s0):
    bd = bq.shape[0]
    per_step = math.gcd(GLA_STEP_SEQS, bd)
    row = lambda w: pl.BlockSpec((per_step, 1, w), lambda b: (b, 0, 0))
    state = pl.BlockSpec((per_step, H_B, DK_B, DV_B), lambda b: (b, 0, 0, 0))
    o, s_fin = pl.pallas_call(
        _gla_step_body,
        grid=(bd // per_step,),
        in_specs=[row(W_BK), row(W_BK), row(W_BK), row(W_BV), state],
        out_specs=[row(W_BV), state],
        out_shape=[jax.ShapeDtypeStruct((bd, 1, W_BV), F32), jax.ShapeDtypeStruct(s0.shape, F32)],
        compiler_params=_params("arbitrary"),
        name="gla_step",
    )(bq.reshape(bd, 1, W_BK), bk.reshape(bd, 1, W_BK), glog.reshape(bd, 1, W_BK),
      bv.reshape(bd, 1, W_BV), s0)
    return o.reshape(bd, W_BV), s_fin


def _head_rms(t, g, width):
    parts = []
    for lo in range(0, t.shape[1], width):
        parts.append(_rms(t[:, lo:lo + width], g[:, lo:lo + width]))
    return jnp.concatenate(parts, axis=1)


def _even_mix(x_ref, a_ref, b_ref, br_ref, ga_ref, gb_ref, w_ref, *, lam_init):
    a = _head_rms(a_ref[...], ga_ref[...], DV_A) * (1.0 - lam_init)
    br = br_ref[...]
    b = _head_rms(b_ref[...], gb_ref[...], DV_B) * (br * jax.nn.sigmoid(br))
    merged = jnp.concatenate([a, b], axis=1).astype(BF16)
    return x_ref[...] + _dot(merged, w_ref[...])


def even_output_ffn(prompt_rows, sample_rows, ga, gb, w, lam_init, g, weights, cast_next=None):
    return _ffn_call(functools.partial(_even_mix, lam_init=lam_init), "even_output_ffn",
                     prompt_rows, sample_rows, [ga, gb, w], g, weights, cast_next)


O_Q, O_K, O_V, O_O, O_G = 0, W_C, 2 * W_C, 3 * W_C, 4 * W_C


def _odd_gates(xn, wgate_ref, bias_ref):
    z = _dot(xn, wgate_ref[...]) + bias_ref[...]
    lane = lax.broadcasted_iota(jnp.int32, z.shape, 1)
    return jnp.where(lane < H_C, z, _log_sigmoid(z))


def _odd_in_step_body(x_ref, g_ref, w_ref, wgate_ref, bias_ref, q_ref, k_ref, v_ref, o_ref, gt_ref):
    xn = _rms(x_ref[...], g_ref[...]).astype(BF16)
    proj = lambda lo, n: _dot(xn, w_ref[:, lo:lo + n])
    q_ref[...] = proj(O_Q, W_C).astype(BF16)
    k_ref[...] = (proj(O_K, W_C) * DH_C ** -0.5).astype(BF16)
    v_ref[...] = proj(O_V, W_C)
    o_ref[...] = proj(O_O, W_C)
    gt_ref[...] = _odd_gates(xn, wgate_ref, bias_ref)


def _odd_in_chunk_body(x_ref, g_ref, w_ref, wgate_ref, wqt_ref, wvt_ref, bias_ref,
                       qt_ref, k_ref, vt_ref, o_ref, gt_ref, gtt_ref, *, chunk):
    xn = _rms(x_ref[...], g_ref[...]).astype(BF16)
    proj = lambda lo, n: _dot(xn, w_ref[:, lo:lo + n])
    qt = _dot_nt(wqt_ref[...], xn).astype(BF16)
    vt = _dot_nt(wvt_ref[...], xn).astype(BF16)
    for c in range(qt_ref.shape[1]):
        qt_ref[0, c] = qt[:, c * chunk:(c + 1) * chunk]
        vt_ref[0, c] = vt[:, c * chunk:(c + 1) * chunk]
    k_ref[...] = (proj(O_K, W_C) * DH_C ** -0.5).astype(BF16)
    o_ref[...] = proj(O_O, W_C)
    gates = _odd_gates(xn, wgate_ref, bias_ref)
    gt_ref[...] = gates
    gtt_ref[...] = gates.T[:2 * H_C, :]


def odd_inputs(x, g, w, wgate, bias, prompt_batch=None):
    m, d = x.shape
    tm = _row_tile(m)
    if prompt_batch is None:
        widths = [(W_C, BF16), (W_C, BF16), (W_C, F32), (W_C, F32), (LANES, F32)]
        return pl.pallas_call(
            _odd_in_step_body,
            grid=(m // tm,),
            in_specs=[_rows(tm, d), _resident((1, d)), _resident(w.shape), _resident(wgate.shape),
                      _resident(bias.shape)],
            out_specs=[_rows(tm, n) for n, _ in widths],
            out_shape=[jax.ShapeDtypeStruct((m, n), dt) for n, dt in widths],
            compiler_params=_params("arbitrary"),
            name="odd_inputs",
        )(x, g, w, wgate, bias)
    seq = m // prompt_batch
    chunk = min(MLSTM_CHUNK, seq)
    per_seq = seq // tm
    chunked = pl.BlockSpec((1, tm // chunk, W_C, chunk), lambda i: (i // per_seq, i % per_seq, 0, 0))
    chunked_shape = jax.ShapeDtypeStruct((prompt_batch, seq // chunk, W_C, chunk), BF16)
    wqt = w[:, O_Q:O_Q + W_C].T
    wvt = w[:, O_V:O_V + W_C].T
    return pl.pallas_call(
        functools.partial(_odd_in_chunk_body, chunk=chunk),
        grid=(m // tm,),
        in_specs=[_rows(tm, d), _resident((1, d)), _resident(w.shape), _resident(wgate.shape),
                  _resident(wqt.shape), _resident(wvt.shape), _resident(bias.shape)],
        out_specs=[chunked, _rows(tm, W_C), chunked, _rows(tm, W_C), _rows(tm, LANES),
                   pl.BlockSpec((2 * H_C, tm), lambda i: (0, i))],
        out_shape=[chunked_shape, jax.ShapeDtypeStruct((m, W_C), BF16), chunked_shape,
                   jax.ShapeDtypeStruct((m, W_C), F32), jax.ShapeDtypeStruct((m, LANES), F32),
                   jax.ShapeDtypeStruct((2 * H_C, m), F32)],
        compiler_params=_params("arbitrary"),
        name="odd_inputs",
    )(x, g, w, wgate, wqt, wvt, bias)


def _mlstm_body(qt_ref, k_ref, vt_ref, gt_ref, gtt_ref, c0_ref, n0_ref, m0_ref,
                h_ref, cfin_ref, nfin_ref, mfin_ref, c_sc, n_sc, m_sc, *, chunk):
    n = pl.program_id(1)

    @pl.when(n == 0)
    def _():
        c_sc[...] = c0_ref[0]
        n_sc[...] = n0_ref[0]
        m_sc[...] = m0_ref[0]

    r = lax.broadcasted_iota(jnp.int32, (chunk, chunk), 0)
    c = lax.broadcasted_iota(jnp.int32, (chunk, chunk), 1)
    past = r <= c
    gates = gt_ref[...]
    gates_t = gtt_ref[...]
    cum_col = jnp.dot(jnp.where(r >= c, 1.0, 0.0), gates, precision=HIGHEST, preferred_element_type=F32)
    cum_row = jnp.dot(gates_t, jnp.where(past, 1.0, 0.0), precision=HIGHEST, preferred_element_type=F32)
    ones = jnp.ones((ONES_ROWS, chunk), BF16)
    hs = range(H_C)
    sl = [slice(h * DH_C, (h + 1) * DH_C) for h in hs]
    m_t, dw, pw, wk, a, m_end = [], [], [], [], [], []
    for h in hs:
        i_row = gates_t[h:h + 1, :]
        b_row = cum_row[H_C + h:H_C + h + 1, :]
        b_last = b_row[:, chunk - 1:chunk]
        m_prev = m_sc[h:h + 1, 0:1]
        src_col = gates[:, h:h + 1] - cum_col[:, H_C + h:H_C + h + 1]
        dlog = jnp.where(past, src_col + b_row, -jnp.inf)
        prev_log = b_row + m_prev
        m_t.append(jnp.maximum(prev_log, jnp.max(dlog, axis=0, keepdims=True)))
        dw.append(jnp.exp(dlog - m_t[h]))
        pw.append(jnp.exp(prev_log - m_t[h]))
        m_end.append(m_t[h][:, chunk - 1:chunk])
        wk.append(jnp.exp(b_last - b_row + i_row - m_end[h]))
        a.append(jnp.exp(b_last + m_prev - m_end[h]))
    qt = [qt_ref[0, 0, sl[h], :] for h in hs]
    vt = [vt_ref[0, 0, sl[h], :] for h in hs]
    k = [k_ref[:, sl[h]] for h in hs]
    c_old = [c_sc[h] for h in hs]
    n_old = [n_sc[h:h + 1, :] for h in hs]
    qk = [_dot(k[h], qt[h]) for h in hs]
    cq = [_dot(c_old[h].astype(BF16), qt[h]) for h in hs]
    nq = [_dot(jnp.broadcast_to(n_old[h], (8, DH_C)).astype(BF16), qt[h])[0:1] for h in hs]
    intra = [_dot(jnp.concatenate([vt[h], ones], axis=0), (qk[h] * dw[h]).astype(BF16)) for h in hs]
    c_add = [_dot((vt[h].astype(F32) * wk[h]).astype(BF16), k[h]) for h in hs]
    n_add = [_dot(jnp.broadcast_to(wk[h], (8, chunk)).astype(BF16), k[h])[0:1] for h in hs]
    for h in hs:
        num = intra[h][:DH_C] + pw[h] * cq[h]
        den = intra[h][DH_C:DH_C + 1] + pw[h] * nq[h]
        h_t = num * (1.0 / jnp.maximum(jnp.abs(den), jnp.exp(-m_t[h])))
        h_ref[:, sl[h]] = h_t.T
        c_sc[h] = a[h] * c_old[h] + c_add[h]
        n_sc[h:h + 1, :] = a[h] * n_old[h] + n_add[h]
        m_sc[h:h + 1, :] = jnp.broadcast_to(m_end[h], (1, LANES))

    @pl.when(n == pl.num_programs(1) - 1)
    def _():
        cfin_ref[0] = c_sc[...]
        nfin_ref[0] = n_sc[...]
        mfin_ref[0] = m_sc[...]


def mlstm_prompt(qt, k, vt, gates, gates_t, c0, n0, m0):
    batch, nc, _, chunk = qt.shape
    m = k.shape[0]
    rows = lambda w: pl.BlockSpec((chunk, w), lambda b, n: (b * nc + n, 0))
    chunked = pl.BlockSpec((1, 1, W_C, chunk), lambda b, n: (b, n, 0, 0))
    st_c = pl.BlockSpec((1, H_C, DH_C, DH_C), lambda b, n: (b, 0, 0, 0))
    st_n = pl.BlockSpec((1, H_C, DH_C), lambda b, n: (b, 0, 0))
    st_m = pl.BlockSpec((1, H_C, LANES), lambda b, n: (b, 0, 0))
    return pl.pallas_call(
        functools.partial(_mlstm_body, chunk=chunk),
        grid=(batch, nc),
        in_specs=[chunked, rows(W_C), chunked, rows(LANES),
                  pl.BlockSpec((2 * H_C, chunk), lambda b, n: (0, b * nc + n)), st_c, st_n, st_m],
        out_specs=[rows(W_C), st_c, st_n, st_m],
        out_shape=[jax.ShapeDtypeStruct((m, W_C), F32), jax.ShapeDtypeStruct(c0.shape, F32),
                   jax.ShapeDtypeStruct(n0.shape, F32), jax.ShapeDtypeStruct(m0.shape, F32)],
        scratch_shapes=[pltpu.VMEM((H_C, DH_C, DH_C), F32), pltpu.VMEM((H_C, DH_C), F32),
                        pltpu.VMEM((H_C, LANES), F32)],
        compiler_params=_params("arbitrary", "arbitrary"),
        name="mlstm_prompt",
    )(qt, k, vt, gates, gates_t, c0, n0, m0)


def _mlstm_step_body(q_ref, k_ref, v_ref, gt_ref, c0_ref, n0_ref, m0_ref, h_ref, c_ref, n_ref, m_ref):
    on_diag = (lax.broadcasted_iota(jnp.int32, (DH_C, DH_C), 0)
               == lax.broadcasted_iota(jnp.int32, (DH_C, DH_C), 1))
    for i in range(q_ref.shape[0]):
        gates = gt_ref[i]
        for h in range(H_C):
            sl = slice(h * DH_C, (h + 1) * DH_C)
            i_pre = gates[:, h:h + 1]
            logf = gates[:, H_C + h:H_C + h + 1]
            m_prev = m0_ref[i, h:h + 1, 0:1]
            prev_log = logf + m_prev
            m_t = jnp.maximum(prev_log, i_pre)
            dw = jnp.exp(i_pre - m_t)
            pw = jnp.exp(prev_log - m_t)
            q = q_ref[i, :, sl]
            k = k_ref[i, :, sl]
            v = v_ref[i, :, sl]
            qf = q.astype(F32)
            kf = k.astype(F32)
            c_old = c0_ref[i, h]
            n_old = n0_ref[i, h:h + 1, :]
            sw = jnp.sum(qf * kf, axis=1, keepdims=True) * dw
            cq = _dot_nt(jnp.broadcast_to(q, (8, DH_C)), c_old.astype(BF16))[0:1]
            num = sw * v + pw * cq
            den = sw + pw * jnp.sum(qf * n_old, axis=1, keepdims=True)
            h_ref[i, :, sl] = num / jnp.maximum(jnp.abs(den), jnp.exp(-m_t))
            diag = jnp.where(on_diag, jnp.broadcast_to(dw * v, (DH_C, DH_C)), 0.0).astype(BF16)
            c_ref[i, h] = pw * c_old + _dot(diag, jnp.broadcast_to(kf, (DH_C, DH_C)).astype(BF16))
            n_ref[i, h:h + 1, :] = pw * n_old + dw * kf
            m_ref[i, h:h + 1, :] = jnp.broadcast_to(m_t, (1, LANES))


def mlstm_step(q, k, v, gates, c0, n0, m0):
    bd = q.shape[0]
    per_step = math.gcd(MLSTM_STEP_SEQS, bd)
    row = lambda w: pl.BlockSpec((per_step, 1, w), lambda b: (b, 0, 0))
    st_c = pl.BlockSpec((per_step, H_C, DH_C, DH_C), lambda b: (b, 0, 0, 0))
    st_n = pl.BlockSpec((per_step, H_C, DH_C), lambda b: (b, 0, 0))
    st_m = pl.BlockSpec((per_step, H_C, LANES), lambda b: (b, 0, 0))
    h, c_fin, n_fin, m_fin = pl.pallas_call(
        _mlstm_step_body,
        grid=(bd // per_step,),
        in_specs=[row(W_C), row(W_C), row(W_C), row(LANES), st_c, st_n, st_m],
        out_specs=[row(W_C), st_c, st_n, st_m],
        out_shape=[jax.ShapeDtypeStruct((bd, 1, W_C), F32), jax.ShapeDtypeStruct(c0.shape, F32),
                   jax.ShapeDtypeStruct(n0.shape, F32), jax.ShapeDtypeStruct(m0.shape, F32)],
        compiler_params=_params("arbitrary"),
        name="mlstm_step",
    )(q.reshape(bd, 1, W_C), k.reshape(bd, 1, W_C), v.reshape(bd, 1, W_C), gates.reshape(bd, 1, LANES),
      c0, n0, m0)
    return h.reshape(bd, W_C), c_fin, n_fin, m_fin


def _odd_mix(x_ref, hm_ref, og_ref, gn_ref, w_ref):
    gated = jax.nn.sigmoid(og_ref[...]) * _head_rms(hm_ref[...], gn_ref[...], DH_C)
    return x_ref[...] + _dot(gated.astype(BF16), w_ref[...])


def odd_output_ffn(prompt_rows, sample_rows, gn, w, g, weights, cast_next=None):
    return _ffn_call(_odd_mix, "odd_output_ffn", prompt_rows, sample_rows, [gn, w], g, weights, cast_next)


def _even_weights(w_in, w_out, qk_norm, gate_w2, gate_b, a_norm, b_norm):
    w = w_in.astype(BF16)
    wbr = w_in[:, E_MAIN + GLA_RANK:].astype(BF16)
    wbg = jnp.pad(w_in[:, E_MAIN:E_MAIN + GLA_RANK], ((0, 0), (0, LANES - GLA_RANK))).astype(BF16)
    w2 = jnp.concatenate([gate_w2, jnp.zeros((LANES - GLA_RANK, W_BK), gate_w2.dtype)], axis=0).astype(BF16)
    idx = jnp.arange(MXU_WIDTH) // DK_A
    seg = (idx[:, None] == idx[None, :]).astype(BF16)
    gq = jnp.tile(qk_norm[0], W_A // DK_A).reshape(1, W_A)
    gk = jnp.tile(qk_norm[1], W_A // DK_A).reshape(1, W_A)
    return dict(w=(w, wbr, wbg), w2=w2, seg=seg, gq=gq, gk=gk, gb=gate_b.reshape(1, W_BK),
                w_out=w_out.astype(BF16), ga=a_norm.reshape(1, W_A), gbn=b_norm.reshape(1, W_BV))


def _odd_weights(w_in, w_out, gate_b, norm_g):
    w = w_in.astype(BF16)
    wgate =jnp.pad(w_in[:, O_G:], ((0, 0), (0, LANES - 2 * H_C))).astype(BF16)
    bias =jnp.concatenate([gate_b[0], gate_b[1], jnp.zeros((LANES - 2 * H_C,), gate_b.dtype)]).reshape(1, LANES)
    return dict(w=w, wgate=wgate, bias=bias, w_out=w_out.astype(BF16), g=norm_g.reshape(1, W_C))


def kernel(x_prompt, x_sample, cache_k, cache_v, state_gla, state_mlstm_C, state_mlstm_n, state_mlstm_m,
           page_table, norm_g, ffn_w_gate, ffn_w_up, ffn_w_down, even_w_in, even_w_out, a_qk_norm, a_lambda,
           a_head_norm, b_gate_w2, b_gate_bias, b_head_norm, odd_w_in, odd_w_out, c_gate_bias, c_head_norm):
    bp, seq, d = x_prompt.shape
    bd = x_sample.shape[0]
    depth = norm_g.shape[0]
    assert x_sample.shape[1] == 1
    yp = x_prompt.reshape(bp * seq, d)
    ys = x_sample.reshape(bd, d)
    n_even, n_pool, page = cache_k.shape[:3]
    ck = jnp.transpose(cache_k, (0, 1, 3, 4, 5, 2)).reshape(n_even, n_pool, W_A, page)
    cv = cache_v.reshape(n_even, n_pool, page * H_A, DV_A)
    slopes = jnp.asarray([2.0 ** (-8.0 * (h + 1) / H_A) for h in range(H_A)], F32)
    slope8 = jnp.repeat(slopes, 2).reshape(2 * H_A, 1)
    ffn32 = (ffn_w_gate, ffn_w_up, ffn_w_down)
    w_first = tuple(w[0, 0].astype(BF16) for w in ffn32)
    pos = jnp.arange(seq)[:, None]
    lane = jnp.arange(LANES)[None, :]
    posb = jnp.where(lane >= 6, 0, jnp.where(lane % 2 == 0, pos // POS_SPLIT * POS_SPLIT,
                                             pos % POS_SPLIT)).astype(BF16)

    k_p, v_p, k_s, v_s, gla_p, gla_s = [], [], [], [], [], []
    cm_p, nm_p, mm_p, cm_s, nm_s, mm_s = [], [], [], [], [], []
    for li in range(depth):
        g = norm_g[li].reshape(3, 1, d)
        yp, ys, w_second = ffn_half(yp, ys, g[0], w_first, cast_next=(*ffn32, li, 1))
        cast_next = (*ffn32, li + 1, 0) if li + 1 < depth else None
        if li % 2 == 0:
            e = li // 2
            lam_init = 0.8 - 0.6 * math.exp(-0.3 * li)
            ew = _even_weights(even_w_in[e], even_w_out[e], a_qk_norm[e], b_gate_w2[e], b_gate_bias[e],
                               a_head_norm[e], b_head_norm[e])
            lam_vecs = a_lambda[e]
            aqt, akb, akt, av, avt, bq, bk, bv, gl, br = even_inputs(
                yp, g[1], *ew["w"], ew["seg"], ew["gq"], ew["gk"], ew["w2"], ew["gb"], prompt_batch=bp)
            a_out = diff_attn_prompt(aqt, akb, avt, posb, slopes, lam_vecs, lam_init)
            b_out, s_fin = gla_prompt(bq, bk, gl, bv, jnp.zeros((bp, W_BK, DV_B), F32), bp)
            prompt_rows = [yp, a_out, b_out, br]
            k_p.append(jnp.transpose(akt.reshape(bp, H_A, 2, DK_A, seq), (0, 4, 1, 2, 3)))
            v_p.append(av.reshape(bp, seq, H_A, DV_A))
            gla_p.append(s_fin.reshape(bp, H_B, DK_B, DV_B))
            aq, ak, av, bq, bk, bv, gl, br = even_inputs(
                ys, g[1], *ew["w"], ew["seg"], ew["gq"], ew["gk"], ew["w2"], ew["gb"])
            a_out = diff_attn_decode(page_table, aq, ak, av, slope8, lam_vecs, lam_init, ck, cv, e)
            b_out, s_fin = gla_step(bq, bk, gl, bv, state_gla[e])
            k_s.append(ak.reshape(bd, 1, H_A, 2, DK_A))
            v_s.append(av.reshape(bd, 1, H_A, DV_A))
            gla_s.append(s_fin)
            yp, ys, w_first = even_output_ffn(prompt_rows, [ys, a_out, b_out, br], ew["ga"], ew["gbn"],
                                              ew["w_out"], lam_init, g[2], w_second, cast_next)
        else:
            o = li // 2
            ow = _odd_weights(odd_w_in[o], odd_w_out[o], c_gate_bias[o], c_head_norm[o])
            qt, k, vt, og, gates, gates_t = odd_inputs(yp, g[1], ow["w"], ow["wgate"], ow["bias"],
                                                       prompt_batch=bp)
            h, c_fin, n_fin, m_fin = mlstm_prompt(
                qt, k, vt, gates, gates_t, jnp.zeros((bp, H_C, DH_C, DH_C), F32),
                jnp.zeros((bp, H_C, DH_C), F32), jnp.zeros((bp, H_C, LANES), F32))
            prompt_rows = [yp, h, og]
            cm_p.append(c_fin)
            nm_p.append(n_fin)
            mm_p.append(m_fin[:, :, 0])
            q, k, v, og, gates = odd_inputs(ys, g[1], ow["w"], ow["wgate"], ow["bias"])
            m0 = jnp.broadcast_to(state_mlstm_m[o][:, :, None], (bd, H_C, LANES))
            h, c_fin, n_fin, m_fin = mlstm_step(q, k, v, gates, state_mlstm_C[o], state_mlstm_n[o], m0)
            cm_s.append(c_fin)
            nm_s.append(n_fin)
            mm_s.append(m_fin[:, :, 0])
            yp, ys, w_first = odd_output_ffn(prompt_rows, [ys, h, og], ow["g"], ow["w_out"], g[2], w_second,
                                             cast_next)
    return (yp.reshape(bp, seq, d), ys.reshape(bd, 1, d),
            jnp.stack(k_p), jnp.stack(v_p), jnp.stack(k_s), jnp.stack(v_s),
            jnp.stack(gla_p), jnp.stack(gla_s),
            jnp.stack(cm_p), jnp.stack(nm_p), jnp.stack(mm_p),
            jnp.stack(cm_s), jnp.stack(nm_s), jnp.stack(mm_s))
```
